```python
import math
import jax
import jax.numpy as jnp
from jax import lax
import numpy as np

D_MODEL = 1024
BATCH = 8
SEQ = 2048
DEPTH = 1

GRID_W = 64
CTX_LEN = 256
NORM_EPS = 1e-6
N_MOD = 6

NA_HEADS = 16
NA_HEAD_DIM = 64
NA_WIDTH = NA_HEADS * NA_HEAD_DIM
NA_KH = 8
NA_KW = 16
ROPE_BASE = 10000.0

SSD_WIDTH = 2 * D_MODEL
SSD_HEAD_DIM = 64
SSD_HEADS = SSD_WIDTH // SSD_HEAD_DIM
SSD_GROUPS = 4
SSD_STATE = 128
SSD_CONV = 5
SSD_CHUNK = 128
SSD_XBC = SSD_WIDTH + 2 * SSD_GROUPS * SSD_STATE

N_EXPERTS = 16
EXPERT_FF = 2048
EC_CAPACITY_FACTOR = 2

IN_COLS = 3 * NA_WIDTH + SSD_WIDTH + SSD_XBC + 2 * SSD_HEADS + 2 * D_MODEL

kernel_name = 'hybrid_na_ssd_ec_diffusion_block'


def _rmsnorm(x, g):
    xf = x.astype(jnp.float32)
    xf = xf * lax.rsqrt(jnp.mean(xf * xf, axis=-1, keepdims=True) + NORM_EPS)
    return (xf * g.astype(jnp.float32)).astype(x.dtype)


def _modulate(x, g, shift, scale):
    return _rmsnorm(x, g) * (1 + scale) + shift


def _ada(cvec, w_ada, b_ada):
    mod = jax.nn.silu(cvec) @ w_ada + b_ada
    return jnp.split(mod, N_MOD, axis=-1)


def _split_in(p):
    offs = np.cumsum([NA_WIDTH, NA_WIDTH, NA_WIDTH, SSD_WIDTH, SSD_XBC, 2 * SSD_HEADS])
    return jnp.split(p, [int(o) for o in offs], axis=-1)


def _heads(t):
    return t.reshape(t.shape[0], t.shape[1], NA_HEADS, NA_HEAD_DIM)


def _rotate(t, pos):
    half = t.shape[-1] // 2
    inv_freq = ROPE_BASE ** (-jnp.arange(half, dtype=jnp.float32) / half)
    ang = pos.astype(jnp.float32)[:, None] * inv_freq[None, :]
    cos = jnp.cos(ang)[:, None, :].astype(t.dtype)
    sin = jnp.sin(ang)[:, None, :].astype(t.dtype)
    t1, t2 = t[..., :half], t[..., half:]
    return jnp.concatenate([t1 * cos - t2 * sin, t1 * sin + t2 * cos], axis=-1)


def _rope_2d(t, rows, cols):
    da = t.shape[-1] // 2
    return jnp.concatenate([_rotate(t[..., :da], rows), _rotate(t[..., da:], cols)], axis=-1)


def _neighbourhood_attention(q, k, v, k_ctx, v_ctx, rpb):
    Bsz, L, H, Dh = q.shape
    rows = L // GRID_W
    kh = min(NA_KH, rows)
    n_loc = kh * GRID_W
    scale = Dh ** -0.5
    qg = q.reshape(Bsz, rows, GRID_W, H, Dh)
    kg = k.reshape(Bsz, rows, GRID_W, H, Dh)
    vg = v.reshape(Bsz, rows, GRID_W, H, Dh)
    col = jnp.arange(GRID_W)
    col_start = jnp.clip(col - NA_KW // 2, 0, GRID_W - NA_KW)
    col_ok = (col[None, :] >= col_start[:, None]) & (col[None, :] < col_start[:, None] + NA_KW)
    col_idx = jnp.clip(col[None, :] - col[:, None] + NA_KW - 1, 0, 2 * NA_KW - 2)
    rpb_col = rpb[:, :, col_idx]
    mask = jnp.tile(col_ok, (1, kh))

    def row_block(r):
        r0 = jnp.clip(r - kh // 2, 0, rows - kh)
        q_r = lax.dynamic_index_in_dim(qg, r, axis=1, keepdims=False)
        k_r = lax.dynamic_slice_in_dim(kg, r0, kh, axis=1).reshape(Bsz, n_loc, H, Dh)
        v_r = lax.dynamic_slice_in_dim(vg, r0, kh, axis=1).reshape(Bsz, n_loc, H, Dh)
        bias = jnp.take(rpb_col, r0 + jnp.arange(kh) - r + NA_KH - 1, axis=1)
        bias = jnp.transpose(bias, (0, 2, 1, 3)).reshape(H, GRID_W, n_loc).astype(jnp.float32)
        s_loc = jnp.einsum('bqhd,bkhd->bhqk', q_r, k_r).astype(jnp.float32) * scale + bias
        s_loc = jnp.where(mask, s_loc, -jnp.inf)
        s_ctx = jnp.einsum('bqhd,bkhd->bhqk', q_r, k_ctx).astype(jnp.float32) * scale
        p = jax.nn.softmax(jnp.concatenate([s_loc, s_ctx], axis=-1), axis=-1).astype(v.dtype)
        return (jnp.einsum('bhqk,bkhd->bqhd', p[..., :n_loc], v_r)
                + jnp.einsum('bhqk,bkhd->bqhd', p[..., n_loc:], v_ctx))

    out = lax.map(row_block, jnp.arange(rows))
    return jnp.moveaxis(out, 0, 1).reshape(Bsz, L, H * Dh)


def _context_attention(q, k, v):
    Bsz, T, H, Dh = q.shape
    s = jnp.einsum('bqhd,bkhd->bhqk', q, k).astype(jnp.float32) * Dh ** -0.5
    p = jax.nn.softmax(s, axis=-1).astype(v.dtype)
    return jnp.einsum('bhqk,bkhd->bqhd', p, v).reshape(Bsz, T, H * Dh)


def _dwconv(x, w, b):
    nch = x.shape[-1]
    out = lax.conv_general_dilated(x, w.astype(x.dtype)[:, None, :], window_strides=(1,),
                                   padding=[(SSD_CONV // 2, SSD_CONV // 2)],
                                   dimension_numbers=('NWC', 'WIO', 'NWC'),
                                   feature_group_count=nch)
    return out + b


def _ssd_prep(xbc, dt_raw, conv_w, conv_b, dtb_f, dtb_b):
    Bsz, L, _ = xbc.shape
    xbc = jax.nn.silu(_dwconv(xbc, conv_w, conv_b))
    xs, Bm, Cm = jnp.split(xbc, [SSD_WIDTH, SSD_WIDTH + SSD_GROUPS * SSD_STATE], axis=-1)
    xs = xs.reshape(Bsz, L, SSD_HEADS, SSD_HEAD_DIM)
    Bm = Bm.reshape(Bsz, L, SSD_GROUPS, SSD_STATE)
    Cm = Cm.reshape(Bsz, L, SSD_GROUPS, SSD_STATE)
    dt_f = jax.nn.softplus(dt_raw[..., :SSD_HEADS] + dtb_f)
    dt_b = jax.nn.softplus(dt_raw[..., SSD_HEADS:] + dtb_b)
    return xs, Bm, Cm, dt_f, dt_b


def _ssd_scan(xs, dt, A, Bm, Cm, h0, return_y):
    Bsz, L, H, P = xs.shape
    nc = L // SSD_CHUNK
    R = H // SSD_GROUPS
    x = (xs * dt[..., None]).reshape(Bsz, nc, SSD_CHUNK, SSD_GROUPS, R, P)
    a = (dt * A).astype(jnp.float32).reshape(Bsz, nc, SSD_CHUNK, SSD_GROUPS, R)
    a_cs = jnp.cumsum(jnp.moveaxis(a, 2, -1), axis=-1)
    Bc = Bm.reshape(Bsz, nc, SSD_CHUNK, SSD_GROUPS, SSD_STATE)
    decay_to_end = jnp.exp(a_cs[..., -1:] - a_cs).astype(xs.dtype)
    states = jnp.einsum('bclgn,bcgrl,bclgrp->bcgrpn', Bc, decay_to_end, x)
    chunk_decay = jnp.exp(a_cs[..., -1]).astype(xs.dtype)

    def step(h, inp):
        dec, st = inp
        return dec[..., None, None] * h + st, h

    h_last, h_prev = lax.scan(step, h0.reshape(Bsz, SSD_GROUPS, R, P, SSD_STATE),
                              (jnp.moveaxis(chunk_decay, 1, 0), jnp.moveaxis(states, 1, 0)))
    h_last = h_last.reshape(Bsz, H, P, SSD_STATE)
    if not return_y:
        return None, h_last
    Cc = Cm.reshape(Bsz, nc, SSD_CHUNK, SSD_GROUPS, SSD_STATE)
    lower = jnp.tril(jnp.ones((SSD_CHUNK, SSD_CHUNK), dtype=bool))
    seg = a_cs[..., :, None] - a_cs[..., None, :]
    Lm = jnp.exp(jnp.where(lower, seg, -jnp.inf)).astype(xs.dtype)
    cb = jnp.einsum('bclgn,bcsgn->bcgls', Cc, Bc)
    y_diag = jnp.einsum('bcgls,bcgrls,bcsgrp->bclgrp', cb, Lm, x)
    y_off = jnp.einsum('bclgn,bcgrpn,bcgrl->bclgrp', Cc, jnp.moveaxis(h_prev, 0, 1),
                       jnp.exp(a_cs).astype(xs.dtype))
    return (y_diag + y_off).reshape(Bsz, L, H, P), h_last


def _bidir_ssd(xs, Bm, Cm, dt_f, dt_b, A_f, A_b, d_skip, h0_f, h0_b, return_y):
    flip = lambda t: jnp.flip(t, axis=1)
    y_f, h_f = _ssd_scan(xs, dt_f, A_f, Bm, Cm, h0_f, return_y)
    y_b, h_b = _ssd_scan(flip(xs), flip(dt_b), A_b, flip(Bm), flip(Cm), h0_b, return_y)
    if not return_y:
        return None, h_f, h_b
    return y_f + flip(y_b) + d_skip[:, None] * xs, h_f, h_b


def _ssd_gated_norm(y, z, g):
    Bsz, L = y.shape[:2]
    u = (y.reshape(Bsz, L, SSD_WIDTH) * jax.nn.silu(z)).astype(jnp.float32)
    u = u.reshape(Bsz, L, SSD_GROUPS, SSD_WIDTH // SSD_GROUPS)
    u = u * lax.rsqrt(jnp.mean(u * u, axis=-1, keepdims=True) + NORM_EPS)
    return (u.reshape(Bsz, L, SSD_WIDTH) * g.astype(jnp.float32)).astype(y.dtype)


def _merge(o_na, y_ssd, gates, w_br_na, w_br_ssd, w_out):
    g_na, g_ssd = jnp.split(gates, 2, axis=-1)
    u = jax.nn.sigmoid(g_na) * (o_na @ w_br_na) + jax.nn.sigmoid(g_ssd) * (y_ssd @ w_br_ssd)
    return u @ w_out


def _token_mixer(h, hc, w_in, na_rpb, conv_w, conv_b, a_log_f, a_log_b, dtb_f, dtb_b, d_skip,
                 ssd_norm, w_br_na, w_br_ssd, w_out, update_ctx):
    Bsz, L, _ = h.shape
    q, k, v, z, xbc, dt_raw, gates = _split_in(h @ w_in)
    qc, kc, vc, zc, xbcc, dtc, gates_c = _split_in(hc @ w_in)
    pos = jnp.arange(L)
    rows, cols = pos // GRID_W, pos % GRID_W
    o_na = _neighbourhood_attention(_rope_2d(_heads(q), rows, cols), _rope_2d(_heads(k), rows, cols),
                                    _heads(v), _heads(kc), _heads(vc), na_rpb)
    A_f, A_b = -jnp.exp(a_log_f), -jnp.exp(a_log_b)
    h0 = jnp.zeros((Bsz, SSD_HEADS, SSD_HEAD_DIM, SSD_STATE), h.dtype)
    y_c, hf_c, hb_c = _bidir_ssd(*_ssd_prep(xbcc, dtc, conv_w, conv_b, dtb_f, dtb_b),
                                 A_f, A_b, d_skip, h0, h0, update_ctx)
    y, _, _ = _bidir_ssd(*_ssd_prep(xbc, dt_raw, conv_w, conv_b, dtb_f, dtb_b),
                         A_f, A_b, d_skip, hf_c, hb_c, True)
    mix = _merge(o_na, _ssd_gated_norm(y, z, ssd_norm), gates, w_br_na, w_br_ssd, w_out)
    if not update_ctx:
        return mix, None
    o_na_c = _context_attention(_heads(qc), _heads(kc), _heads(vc))
    mix_c = _merge(o_na_c, _ssd_gated_norm(y_c, zc, ssd_norm), gates_c, w_br_na, w_br_ssd, w_out)
    return mix, mix_c


def _ec_moe(h, w_router, w_eg, w_eu, w_ed):
    Bsz, T, _ = h.shape
    cap = EC_CAPACITY_FACTOR * T // N_EXPERTS
    aff = jax.nn.softmax((h @ w_router).astype(jnp.float32), axis=-1)
    gate, idx = lax.top_k(jnp.transpose(aff, (0, 2, 1)), cap)
    b_idx = jnp.arange(Bsz)[:, None, None]
    xg = h[b_idx, idx]
    hid = jax.nn.silu(jnp.einsum('becd,edf->becf', xg, w_eg)) * jnp.einsum('becd,edf->becf', xg, w_eu)
    yo = jnp.einsum('becf,efd->becd', hid, w_ed) * gate[..., None].astype(h.dtype)
    return jnp.zeros_like(h).at[b_idx, idx].add(yo)


def _layer(x, ctx, c, c_ctx, w_ada, b_ada, n_pre_mix, n_post_mix, n_pre_ffn, n_post_ffn, w_in, na_rpb,
           conv_w, conv_b, a_log_f, a_log_b, dtb_f, dtb_b, d_skip, ssd_norm, w_br_na, w_br_ssd, w_out,
           w_router, w_eg, w_eu, w_ed, update_ctx):
    sh1, sc1, ga1, sh2, sc2, ga2 = _ada(c[:, None, :], w_ada, b_ada)
    csh1, csc1, cga1, csh2, csc2, cga2 = _ada(c_ctx, w_ada, b_ada)
    mix, mix_c = _token_mixer(_modulate(x, n_pre_mix, sh1, sc1), _modulate(ctx, n_pre_mix, csh1, csc1),
                              w_in, na_rpb, conv_w, conv_b, a_log_f, a_log_b, dtb_f, dtb_b, d_skip,
                              ssd_norm, w_br_na, w_br_ssd, w_out, update_ctx)
    x = x + ga1 * _rmsnorm(mix, n_post_mix)
    x = x + ga2 * _rmsnorm(_ec_moe(_modulate(x, n_pre_ffn, sh2, sc2), w_router, w_eg, w_eu, w_ed), n_post_ffn)
    if update_ctx:
        ctx = ctx + cga1 * _rmsnorm(mix_c, n_post_mix)
        ctx = ctx + cga2 * _rmsnorm(_ec_moe(_modulate(ctx, n_pre_ffn, csh2, csc2), w_router, w_eg, w_eu, w_ed),
                                    n_post_ffn)
    return x, ctx


def setup_inputs(seed: int = 0) -> dict:
    key = jax.random.key(seed)
    ks = jax.random.split(key, 26)
    f32 = jnp.float32

    def nrm(k, shape, s):
        return jax.random.normal(k, shape, f32) * s

    dt0 = jnp.exp(jax.random.uniform(ks[14], (DEPTH, 2, SSD_HEADS), f32, math.log(1e-3), math.log(1e-1)))
    dt_bias = dt0 + jnp.log(-jnp.expm1(-dt0))
    a_log = jnp.log(jax.random.uniform(ks[15], (DEPTH, 2, SSD_HEADS), f32, 1.0, 16.0))
    return {
        'x': nrm(ks[0], (BATCH, SEQ, D_MODEL), 1.0),
        'c': nrm(ks[1], (BATCH, D_MODEL), 1.0),
        'ctx': nrm(ks[2], (BATCH, CTX_LEN, D_MODEL), 1.0),
        'c_ctx': nrm(ks[3], (D_MODEL,), 1.0),
        'w_ada': nrm(ks[4], (DEPTH, D_MODEL, N_MOD * D_MODEL), 0.5 * D_MODEL ** -0.5),
        'b_ada': nrm(ks[5], (DEPTH, N_MOD * D_MODEL), 0.02),
        'norm_pre_mix': 1.0 + nrm(ks[6], (DEPTH, D_MODEL), 0.02),
        'norm_post_mix': 1.0 + nrm(ks[7], (DEPTH, D_MODEL), 0.02),
        'norm_pre_ffn': 1.0 + nrm(ks[8], (DEPTH, D_MODEL), 0.02),
        'norm_post_ffn': 1.0 + nrm(ks[9], (DEPTH, D_MODEL), 0.02),
        'w_in': nrm(ks[10], (DEPTH, D_MODEL, IN_COLS), D_MODEL ** -0.5),
        'na_rpb': nrm(ks[11], (DEPTH, NA_HEADS, 2 * NA_KH - 1, 2 * NA_KW - 1), 0.1),
        'ssd_conv_w': nrm(ks[12], (DEPTH, SSD_CONV, SSD_XBC), SSD_CONV ** -0.5),
        'ssd_conv_b': nrm(ks[13], (DEPTH, SSD_XBC), 0.02),
        'ssd_a_log_fwd': a_log[:, 0],
        'ssd_a_log_bwd': a_log[:, 1],
        'ssd_dt_bias_fwd': dt_bias[:, 0],
        'ssd_dt_bias_bwd': dt_bias[:, 1],
        'ssd_d_skip': 1.0 + nrm(ks[16], (DEPTH, SSD_HEADS), 0.02),
        'ssd_norm': 1.0 + nrm(ks[17], (DEPTH, SSD_WIDTH), 0.02),
        'w_branch_na': nrm(ks[18], (DEPTH, NA_WIDTH, D_MODEL), NA_WIDTH ** -0.5),
        'w_branch_ssd': nrm(ks[19], (DEPTH, SSD_WIDTH, D_MODEL), SSD_WIDTH ** -0.5),
        'w_out': nrm(ks[20], (DEPTH, D_MODEL, D_MODEL), D_MODEL ** -0.5),
        'w_router': nrm(ks[21], (DEPTH, D_MODEL, N_EXPERTS), D_MODEL ** -0.5),
        'w_exp_gate': nrm(ks[22], (DEPTH, N_EXPERTS, D_MODEL, EXPERT_FF), D_MODEL ** -0.5),
        'w_exp_up': nrm(ks[23], (DEPTH, N_EXPERTS, D_MODEL, EXPERT_FF), D_MODEL ** -0.5),
        'w_exp_down': nrm(ks[24], (DEPTH, N_EXPERTS, EXPERT_FF, D_MODEL), EXPERT_FF ** -0.5),
    }


def reference(x, c, ctx, c_ctx, w_ada, b_ada, norm_pre_mix, norm_post_mix, norm_pre_ffn, norm_post_ffn,
              w_in, na_rpb, ssd_conv_w, ssd_conv_b, ssd_a_log_fwd, ssd_a_log_bwd, ssd_dt_bias_fwd,
              ssd_dt_bias_bwd, ssd_d_skip, ssd_norm, w_branch_na, w_branch_ssd, w_out, w_router,
              w_exp_gate, w_exp_up, w_exp_down):
    for i in range(DEPTH):
        x, ctx = _layer(x, ctx, c, c_ctx, w_ada[i], b_ada[i], norm_pre_mix[i], norm_post_mix[i],
                        norm_pre_ffn[i], norm_post_ffn[i], w_in[i], na_rpb[i], ssd_conv_w[i], ssd_conv_b[i],
                        ssd_a_log_fwd[i], ssd_a_log_bwd[i], ssd_dt_bias_fwd[i], ssd_dt_bias_bwd[i],
                        ssd_d_skip[i], ssd_norm[i], w_branch_na[i], w_branch_ssd[i], w_out[i], w_router[i],
                        w_exp_gate[i], w_exp_up[i], w_exp_down[i], i < DEPTH - 1)
    return x
```

```python
import functools
import math

import jax
import jax.numpy as jnp
import numpy as np
from jax import lax
from jax.experimental import pallas as pl
from jax.experimental.pallas import tpu as pltpu

F32 = jnp.float32
BF16 = jnp.bfloat16

D_MODEL = 1024
GRID_W = 64
NORM_EPS = 1e-6
N_MOD = 6
NA_HEADS = 16
NA_HEAD_DIM = 64
NA_KH = 8
NA_KW = 16
ROPE_BASE = 10000.0
SSD_WIDTH = 2048
SSD_HEAD_DIM = 64
SSD_HEADS = 32
SSD_GROUPS = 4
SSD_STATE = 128
SSD_CONV = 5
SSD_CHUNK = 128
N_EXPERTS = 16
EXPERT_FF = 2048
EC_CAPACITY_FACTOR = 2

VMEM_LIMIT_BYTES = 56 * 1024 * 1024
LANES = 128

COL_Q, COL_K, COL_V, COL_Z, COL_X, COL_B, COL_C, COL_G = 0, 1024, 2048, 3072, 5120, 7168, 7680, 8192
LAT_COLS = 10240
CTX_K, CTX_V, CTX_X, CTX_B, CTX_C = 0, 1024, 2048, 4096, 4608
CTX_COLS = 5120


def _params(semantics):
    return pltpu.CompilerParams(dimension_semantics=semantics, vmem_limit_bytes=VMEM_LIMIT_BYTES)


def _silu(v):
    return v * jax.nn.sigmoid(v)


def _softplus(v):
    return jnp.maximum(v, 0.0) + jnp.log1p(jnp.exp(-jnp.abs(v)))


def _ada_kernel(c_ref, w_ref, b_ref, o_ref):
    o_ref[...] = jnp.dot(_silu(c_ref[...]), w_ref[...], preferred_element_type=F32) + b_ref[...]


def _ada(cc, w_ada, b_ada):
    rows, d = cc.shape
    n = w_ada.shape[1]
    tn = 1536
    return pl.pallas_call(
        _ada_kernel,
        grid=(n // tn,),
        in_specs=[pl.BlockSpec((rows, d), lambda j: (0, 0)),
                  pl.BlockSpec((d, tn), lambda j: (0, j)),
                  pl.BlockSpec((1, tn), lambda j: (0, j))],
        out_specs=pl.BlockSpec((rows, tn), lambda j: (0, j)),
        out_shape=jax.ShapeDtypeStruct((rows, n), F32),
        compiler_params=_params(("arbitrary",)),
        name="ada",
    )(cc, w_ada, b_ada.reshape(1, n))


def _rope_tables(seq):
    lane = np.arange(LANES)
    axis = (lane % NA_HEAD_DIM) // 32
    within = lane % 32
    half = 16
    inv_freq = ROPE_BASE ** (-(within % half).astype(np.float64) / half)
    pos = np.arange(seq)
    coord = np.where(axis[None, :] == 0, (pos // GRID_W)[:, None], (pos % GRID_W)[:, None])
    ang = coord.astype(np.float32) * inv_freq.astype(np.float32)[None, :]
    cos, sin = jnp.cos(jnp.asarray(ang, F32)), jnp.sin(jnp.asarray(ang, F32))
    first = jnp.asarray(within < half)[None, :]
    return cos, jnp.where(first, -sin, 0.0), jnp.where(first, 0.0, sin)


def _inproj_kernel(x_ref, sh_ref, sc_ref, g_ref, w_ref, wdt_ref, cos_ref, s1_ref, s2_ref,
                   o_ref, dt_ref, h_scr, *, rope_tiles):
    j = pl.program_id(2)

    @pl.when(j == 0)
    def _():
        x = x_ref[0]
        xn = x * lax.rsqrt(jnp.mean(x * x, axis=-1, keepdims=True) + NORM_EPS) * g_ref[...]
        h = (xn * (1.0 + sc_ref[0]) + sh_ref[0]).astype(BF16)
        h_scr[...] = h
        dt_ref[0] = jnp.dot(h, wdt_ref[...], preferred_element_type=F32)

    acc = jnp.dot(h_scr[...], w_ref[...], preferred_element_type=F32)

    @pl.when(j >= rope_tiles)
    def _():
        o_ref[0] = acc.astype(o_ref.dtype)

    if rope_tiles:
        @pl.when(j < rope_tiles)
        def _():
            scale = jnp.where(j == 0, NA_HEAD_DIM ** -0.5, 1.0).astype(F32)
            cos, s1, s2 = cos_ref[...] * scale, s1_ref[...] * scale, s2_ref[...] * scale
            for c in range(acc.shape[1] // LANES):
                t = acc[:, c * LANES:(c + 1) * LANES]
                r = t * cos + pltpu.roll(t, LANES - 16, 1) * s1 + pltpu.roll(t, 16, 1) * s2
                o_ref[0, :, c * LANES:(c + 1) * LANES] = r.astype(o_ref.dtype)


def _inproj(x, mod3, mod_row0, g_pre, w, w_dt, rope, rope_tiles):
    bsz, seq, d = x.shape
    n = w.shape[1]
    ndt = w_dt.shape[1]
    tm, tn = 1024, 1024
    cos, s1, s2 = rope
    kern = functools.partial(_inproj_kernel, rope_tiles=rope_tiles)
    return pl.pallas_call(
        kern,
        grid=(bsz, seq // tm, n // tn),
        in_specs=[pl.BlockSpec((1, tm, d), lambda b, i, j: (b, i, 0)),
                  pl.BlockSpec((1, 1, d), lambda b, i, j: (b + mod_row0, 0, 0)),
                  pl.BlockSpec((1, 1, d), lambda b, i, j: (b + mod_row0, 0, 1)),
                  pl.BlockSpec((1, d), lambda b, i, j: (0, 0)),
                  pl.BlockSpec((d, tn), lambda b, i, j: (0, j)),
                  pl.BlockSpec((d, ndt), lambda b, i, j: (0, 0)),
                  pl.BlockSpec((tm, LANES), lambda b, i, j: (i, 0)),
                  pl.BlockSpec((tm, LANES), lambda b, i, j: (i, 0)),
                  pl.BlockSpec((tm, LANES), lambda b, i, j: (i, 0))],
        out_specs=[pl.BlockSpec((1, tm, tn), lambda b, i, j: (b, i, j)),
                   pl.BlockSpec((1, tm, ndt), lambda b, i, j: (b, i, 0))],
        out_shape=[jax.ShapeDtypeStruct((bsz, seq, n), BF16),
                   jax.ShapeDtypeStruct((bsz, seq, ndt), F32)],
        scratch_shapes=[pltpu.VMEM((tm, d), BF16)],
        compiler_params=_params(("parallel", "parallel", "arbitrary")),
        name="inproj",
    )(x, mod3, mod3, g_pre, w, w_dt, cos, s1, s2)


NA_QROWS = 4
NA_KROWS = 12
NA_QBLK = NA_QROWS * GRID_W
NA_KBLK = NA_KROWS * GRID_W
NA_MASKED = -1e30


def _na_key_start(g, rows):
    return min(max(NA_QROWS * g - NA_KH // 2, 0), rows - NA_KROWS)


def _na_bias(rpb, rows):
    heads = rpb.shape[0]
    col = np.arange(GRID_W)
    col_start = np.clip(col - NA_KW // 2, 0, GRID_W - NA_KW)
    col_ok = (col[None, :] >= col_start[:, None]) & (col[None, :] < col_start[:, None] + NA_KW)
    col_idx = np.clip(col[None, :] - col[:, None] + NA_KW - 1, 0, 2 * NA_KW - 2)
    nblk = rows // NA_QROWS
    out = []
    for g in (0, 1, nblk - 1):
        r = NA_QROWS * g + np.arange(NA_QROWS)
        kr = _na_key_start(g, rows) + np.arange(NA_KROWS)
        r0 = np.clip(r - NA_KH // 2, 0, rows - NA_KH)
        row_ok = (kr[None, :] >= r0[:, None]) & (kr[None, :] < r0[:, None] + NA_KH)
        row_idx = np.clip(kr[None, :] - r[:, None] + NA_KH - 1, 0, 2 * NA_KH - 2)
        b = rpb[:, row_idx[:, None, :, None], col_idx[None, :, None, :]]
        ok = row_ok[:, None, :, None] & col_ok[None, :, None, :]
        out.append(jnp.where(ok[None], b, NA_MASKED).reshape(heads, NA_QBLK, NA_KBLK))
    return jnp.stack(out).astype(F32)


def _na_kernel(q_ref, k_ref, v_ref, kc_ref, vc_ref, bias_ref, o_ref, *, rows):
    nblk = rows // NA_QROWS
    first_head = lax.broadcasted_iota(jnp.int32, (1, LANES), 1) < NA_HEAD_DIM
    kc, vc = kc_ref[0], vc_ref[0]
    nt = (((1,), (1,)), ((), ()))
    for g in range(nblk):
        k0 = _na_key_start(g, rows) * GRID_W
        geom = 0 if g == 0 else (2 if g == nblk - 1 else 1)
        q = q_ref[0, g * NA_QBLK:(g + 1) * NA_QBLK, :]
        kw = k_ref[0, k0:k0 + NA_KBLK, :]
        vw = v_ref[0, k0:k0 + NA_KBLK, :]
        outs = []
        for hh in range(2):
            qm = jnp.where(first_head if hh == 0 else ~first_head, q, jnp.zeros_like(q))
            s = lax.dot_general(qm, kw, nt, preferred_element_type=F32) + bias_ref[geom, hh]
            sc = lax.dot_general(qm, kc, nt, preferred_element_type=F32)
            m = jnp.maximum(jnp.max(s, axis=-1, keepdims=True), jnp.max(sc, axis=-1, keepdims=True))
            p, pc = jnp.exp(s - m), jnp.exp(sc - m)
            denom = jnp.sum(p, axis=-1, keepdims=True) + jnp.sum(pc, axis=-1, keepdims=True)
            o = (jnp.dot(p.astype(BF16), vw, preferred_element_type=F32)
                 + jnp.dot(pc.astype(BF16), vc, preferred_element_type=F32))
            outs.append(o / denom)
        o_ref[0, g * NA_QBLK:(g + 1) * NA_QBLK, :] = jnp.where(first_head, outs[0], outs[1]).astype(o_ref.dtype)


def _na(slab, slab_ctx, bias):
    bsz, seq, _ = slab.shape
    tctx = slab_ctx.shape[1]
    pairs = NA_HEADS // 2
    blk = lambda col0: (lambda hp, b: (b, 0, col0 // LANES + hp))
    return pl.pallas_call(
        functools.partial(_na_kernel, rows=seq // GRID_W),
        grid=(pairs, bsz),
        in_specs=[pl.BlockSpec((1, seq, LANES), blk(COL_Q)),
                  pl.BlockSpec((1, seq, LANES), blk(COL_K)),
                  pl.BlockSpec((1, seq, LANES), blk(COL_V)),
                  pl.BlockSpec((1, tctx, LANES), blk(CTX_K)),
                  pl.BlockSpec((1, tctx, LANES), blk(CTX_V)),
                  pl.BlockSpec((3, 2, NA_QBLK, NA_KBLK), lambda hp, b: (0, hp, 0, 0))],
        out_specs=pl.BlockSpec((1, seq, LANES), lambda hp, b: (b, 0, hp)),
        out_shape=jax.ShapeDtypeStruct((bsz, seq, NA_HEADS * NA_HEAD_DIM), BF16),
        compiler_params=_params(("parallel", "parallel")),
        name="na",
    )(slab, slab, slab, slab_ctx, slab_ctx, bias)


SSD_GHEADS = SSD_HEADS // SSD_GROUPS
SSD_GWIDTH = SSD_GHEADS * SSD_HEAD_DIM
CONV_HALO = 8


def _split3(v):
    hi = v.astype(BF16)
    r1 = v - hi.astype(F32)
    mid = r1.astype(BF16)
    lo = (r1 - mid.astype(F32)).astype(BF16)
    return hi, mid, lo


def _ssd_kernel(xs_ref, b_ref, c_ref, z_ref, dt_ref, xsc_ref, bc_ref, cc_ref, dtc_ref,
                cwx_ref, cwb_ref, cwc_ref, cbx_ref, cbb_ref, cbc_ref, dtb_ref, alog_ref, dskip_ref, nrm_ref,
                y_ref,
                padx, padb, padc, xs_s, b_s, c_s, dt_s, acs_s, ex_s, hprev_s, sb_s, decb_s, hf_s, hb_s,
                *, seq, tctx):
    Q = SSD_CHUNK
    GH = SSD_GHEADS
    rows = lax.broadcasted_iota(jnp.int32, (Q, Q), 0)
    cols = lax.broadcasted_iota(jnp.int32, (Q, Q), 1)
    tril = jnp.where(rows >= cols, 1.0, 0.0).astype(BF16)
    lane = lax.broadcasted_iota(jnp.int32, (1, LANES), 1)
    is_fwd = lane < GH
    first_head = lane < SSD_HEAD_DIM
    a_coef = -jnp.exp(alog_ref[...])
    dt_bias = dtb_ref[...]

    def expand(f, lane0):
        tiles = [jnp.where(first_head, f[:, lane0 + 2 * j:lane0 + 2 * j + 1], f[:, lane0 + 2 * j + 1:lane0 + 2 * j + 2])
                 for j in range(GH // 2)]
        return jnp.concatenate(tiles, axis=1)

    def prep(x_raw, b_raw, c_raw, dt_raw, n):
        for pad, raw in ((padx, x_raw), (padb, b_raw), (padc, c_raw)):
            width = pad.shape[1]
            pad[0:CONV_HALO, :] = jnp.zeros((CONV_HALO, width), F32)
            pad[CONV_HALO + n:2 * CONV_HALO + n, :] = jnp.zeros((CONV_HALO, width), F32)

        def stage(c, carry):
            r0 = pl.multiple_of(c * Q, Q)
            for pad, raw in ((padx, x_raw), (padb, b_raw), (padc, c_raw)):
                pad[pl.ds(r0 + CONV_HALO, Q), :] = raw[0, pl.ds(r0, Q), :].astype(F32)
            return carry

        lax.fori_loop(0, n // Q, stage, 0)

        def body(c, carry):
            r0 = pl.multiple_of(c * Q, Q)
            for pad, w_ref, bias_ref, dst in ((padx, cwx_ref, cbx_ref, xs_s), (padb, cwb_ref, cbb_ref, b_s),
                                              (padc, cwc_ref, cbc_ref, c_s)):
                win = pad[pl.ds(r0, Q + 2 * CONV_HALO), :]
                acc = bias_ref[...] + win[CONV_HALO - 2:CONV_HALO - 2 + Q, :] * w_ref[0:1, :]
                for k in range(1, SSD_CONV):
                    acc = acc + win[CONV_HALO - 2 + k:CONV_HALO - 2 + k + Q, :] * w_ref[k:k + 1, :]
                dst[pl.ds(r0, Q), :] = _silu(acc)
            dt = _softplus(dt_raw[0, pl.ds(r0, Q), :] + dt_bias)
            a = dt * a_coef
            cs = jnp.dot(tril, jnp.concatenate(_split3(a), axis=1), preferred_element_type=F32)
            acs = cs[:, 0:LANES] + cs[:, LANES:2 * LANES] + cs[:, 2 * LANES:3 * LANES]
            dt_s[pl.ds(r0, Q), :] = dt
            acs_s[pl.ds(r0, Q), :] = acs
            ex_s[pl.ds(r0, Q), :] = acs - a
            return carry

        lax.fori_loop(0, n // Q, body, 0)

    def chunk_states(n, store_prev):
        def body(c, carry):
            r0 = pl.multiple_of(c * Q, Q)
            acs, ex, dt = acs_s[pl.ds(r0, Q), :], ex_s[pl.ds(r0, Q), :], dt_s[pl.ds(r0, Q), :]
            tot = acs[Q - 1:Q, :]
            w = dt * jnp.exp(jnp.where(is_fwd, tot - acs, ex))
            xs = xs_s[pl.ds(r0, Q), :]
            xdec = jnp.concatenate([xs * expand(w, 0), xs * expand(w, GH)], axis=1).astype(BF16)
            b_t = b_s[pl.ds(r0, Q), :].T.astype(BF16)
            s = jnp.dot(b_t, xdec, preferred_element_type=F32)
            dec = jnp.exp(tot)
            if store_prev:
                hprev_s[c] = hf_s[...].astype(BF16)
            hf_s[...] = expand(dec, 0) * hf_s[...] + s[:, :SSD_GWIDTH]
            sb_s[c] = s[:, SSD_GWIDTH:]
            decb_s[c] = jnp.broadcast_to(expand(dec, GH), (8, SSD_GWIDTH))
            return carry

        lax.fori_loop(0, n // Q, body, 0)

    def backward_step(c):
        hb_s[...] = decb_s[c][0:1, :] * hb_s[...] + sb_s[c]

    hf_s[...] = jnp.zeros(hf_s.shape, F32)
    hb_s[...] = jnp.zeros(hb_s.shape, F32)

    prep(xsc_ref, bc_ref, cc_ref, dtc_ref, tctx)
    chunk_states(tctx, False)
    nctx = tctx // Q

    def ctx_back(i, carry):
        backward_step(nctx - 1 - i)
        return carry

    lax.fori_loop(0, nctx, ctx_back, 0)

    prep(xs_ref, b_ref, c_ref, dt_ref, seq)
    chunk_states(seq, True)
    nch = seq // Q
    dskip = dskip_ref[...]
    gain = nrm_ref[...]

    def ybody(i, carry):
        c = nch - 1 - i
        r0 = pl.multiple_of(c * Q, Q)
        acs, ex, dt = acs_s[pl.ds(r0, Q), :], ex_s[pl.ds(r0, Q), :], dt_s[pl.ds(r0, Q), :]
        acs_t, ex_t, dt_t = acs.T, ex.T, dt.T
        xs = xs_s[pl.ds(r0, Q), :]
        xsb = xs.astype(BF16)
        cm = c_s[pl.ds(r0, Q), :].astype(BF16)
        b_t = b_s[pl.ds(r0, Q), :].T.astype(BF16)
        cb = jnp.dot(cm, b_t, preferred_element_type=F32)
        tiles = []
        for j in range(GH // 2):
            xpair = xsb[:, j * LANES:(j + 1) * LANES]
            ys = []
            for hh in range(2):
                hf, hb = 2 * j + hh, GH + 2 * j + hh
                arg_f = acs[:, hf:hf + 1] - acs_t[hf:hf + 1, :]
                arg_b = ex_t[hb:hb + 1, :] - ex[:, hb:hb + 1]
                e = jnp.exp(jnp.where(rows >= cols, arg_f, arg_b))
                dt_f, dt_b = dt_t[hf:hf + 1, :], dt_t[hb:hb + 1, :]
                fac = jnp.where(rows > cols, dt_f, jnp.where(rows < cols, dt_b, dt_f + dt_b))
                m = (cb * e * fac).astype(BF16)
                ys.append(jnp.dot(m, xpair, preferred_element_type=F32))
            tiles.append(jnp.where(first_head, ys[0], ys[1]))
        y = jnp.concatenate(tiles, axis=1)
        tot = acs[Q - 1:Q, :]
        e_out = jnp.exp(jnp.where(is_fwd, acs, tot - ex))
        y = y + jnp.dot(cm, hprev_s[c], preferred_element_type=F32) * expand(e_out, 0)
        y = y + jnp.dot(cm, hb_s[...].astype(BF16), preferred_element_type=F32) * expand(e_out, GH)
        y = y + dskip * xs
        u = y * _silu(z_ref[0, pl.ds(r0, Q), :].astype(F32))
        u = u * lax.rsqrt(jnp.mean(u * u, axis=-1, keepdims=True) + NORM_EPS) * gain
        y_ref[0, pl.ds(r0, Q), :] = u.astype(y_ref.dtype)
        backward_step(c)
        return carry

    lax.fori_loop(0, nch, ybody, 0)


def _ssd(slab, dt, slab_ctx, dt_ctx, cw, cb, dtb, alog, dskip, nrm):
    bsz, seq, _ = slab.shape
    tctx = slab_ctx.shape[1]
    gw, ns = SSD_GWIDTH, SSD_STATE
    nch = seq // SSD_CHUNK
    cwx, cwb, cwc = cw
    cbx, cbb, cbc = cb

    def tok(width, col0, n):
        return pl.BlockSpec((1, n, width), lambda b, g: (b, 0, col0 // width + g))

    def par(rows, width):
        return pl.BlockSpec((rows, width), lambda b, g: (0, g))

    pad = seq + 2 * CONV_HALO
    return pl.pallas_call(
        functools.partial(_ssd_kernel, seq=seq, tctx=tctx),
        grid=(bsz, SSD_GROUPS),
        in_specs=[tok(gw, COL_X, seq), tok(ns, COL_B, seq), tok(ns, COL_C, seq), tok(gw, COL_Z, seq),
                  tok(LANES, 0, seq),
                  tok(gw, CTX_X, tctx), tok(ns, CTX_B, tctx), tok(ns, CTX_C, tctx), tok(LANES, 0, tctx),
                  par(SSD_CONV, gw), par(SSD_CONV, ns), par(SSD_CONV, ns),
                  par(1, gw), par(1, ns), par(1, ns),
                  par(1, LANES), par(1, LANES), par(1, gw), par(1, gw)],
        out_specs=pl.BlockSpec((1, seq, gw), lambda b, g: (b, 0, g)),
        out_shape=jax.ShapeDtypeStruct((bsz, seq, SSD_WIDTH), BF16),
        scratch_shapes=[pltpu.VMEM((pad, gw), F32), pltpu.VMEM((pad, ns), F32), pltpu.VMEM((pad, ns), F32),
                        pltpu.VMEM((seq, gw), F32), pltpu.VMEM((seq, ns), F32), pltpu.VMEM((seq, ns), F32),
                        pltpu.VMEM((seq, LANES), F32), pltpu.VMEM((seq, LANES), F32), pltpu.VMEM((seq, LANES), F32),
                        pltpu.VMEM((nch, ns, gw), BF16), pltpu.VMEM((nch, ns, gw), F32),
                        pltpu.VMEM((nch, 8, gw), F32),
                        pltpu.VMEM((ns, gw), F32), pltpu.VMEM((ns, gw), F32)],
        compiler_params=_params(("parallel", "parallel")),
        name="ssd",
    )(slab, slab, slab, slab, dt, slab_ctx, slab_ctx, slab_ctx, dt_ctx,
      cwx, cwb, cwc, cbx, cbb, cbc, dtb, alog, dskip, nrm)


def _rms(v, gain):
    return v * lax.rsqrt(jnp.mean(v * v, axis=-1, keepdims=True) + NORM_EPS) * gain


def _merge_kernel(ona_ref, y_ref, gna_ref, gssd_ref, x_ref, ga1_ref, sh2_ref, sc2_ref,
                  wna_ref, wssd_ref, wout_ref, npost_ref, npre_ref, wrh_ref, wrl_ref,
                  x1_ref, h2_ref, aff_ref):
    a = jnp.dot(ona_ref[0], wna_ref[...], preferred_element_type=F32)
    s = jnp.dot(y_ref[0], wssd_ref[...], preferred_element_type=F32)
    u = jax.nn.sigmoid(gna_ref[0].astype(F32)) * a + jax.nn.sigmoid(gssd_ref[0].astype(F32)) * s
    mix = jnp.dot(u.astype(BF16), wout_ref[...], preferred_element_type=F32)
    x1 = x_ref[0] + ga1_ref[0] * _rms(mix, npost_ref[...])
    x1_ref[0] = x1
    h2 = _rms(x1, npre_ref[...]) * (1.0 + sc2_ref[0]) + sh2_ref[0]
    hi = h2.astype(BF16)
    h2_ref[0] = hi
    lo = (h2 - hi.astype(F32)).astype(BF16)
    nt = (((1,), (1,)), ((), ()))
    wrh, wrl = wrh_ref[...], wrl_ref[...]
    logits = (lax.dot_general(wrh, hi, nt, preferred_element_type=F32)
              + lax.dot_general(wrh, lo, nt, preferred_element_type=F32)
              + lax.dot_general(wrl, hi, nt, preferred_element_type=F32))
    e = jnp.exp(logits - jnp.max(logits, axis=0, keepdims=True))
    aff_ref[0] = e / jnp.sum(e, axis=0, keepdims=True)


def _merge(o_na, y_ssd, slab, x, mod3, w_na, w_ssd, w_o, n_post, n_pre, wr_hi, wr_lo):
    bsz, seq, d = x.shape
    tm = 512
    ne = wr_hi.shape[0]
    tok = lambda width, blk: pl.BlockSpec((1, tm, width), lambda b, i: (b, i, blk))
    modv = lambda k: pl.BlockSpec((1, 1, d), lambda b, i: (b, 0, k))
    full = lambda r, c_: pl.BlockSpec((r, c_), lambda b, i: (0, 0))
    return pl.pallas_call(
        _merge_kernel,
        grid=(bsz, seq // tm),
        in_specs=[tok(d, 0), tok(SSD_WIDTH, 0), tok(d, COL_G // d), tok(d, COL_G // d + 1), tok(d, 0),
                  modv(2), modv(3), modv(4),
                  full(d, d), full(SSD_WIDTH, d), full(d, d), full(1, d), full(1, d), full(ne, d), full(ne, d)],
        out_specs=[tok(d, 0), tok(d, 0), pl.BlockSpec((1, ne, tm), lambda b, i: (b, 0, i))],
        out_shape=[jax.ShapeDtypeStruct((bsz, seq, d), F32), jax.ShapeDtypeStruct((bsz, seq, d), BF16),
                   jax.ShapeDtypeStruct((bsz, ne, seq), F32)],
        compiler_params=_params(("parallel", "parallel")),
        name="merge",
    )(o_na, y_ssd, slab, slab, x, mod3, mod3, mod3, w_na, w_ssd, w_o, n_post, n_pre, wr_hi, wr_lo)


def _prefix_count(mask_bf16, strict_upper):
    r, s = mask_bf16.shape
    offset = jnp.zeros((r, 1), F32)
    parts = []
    for j in range(s // LANES):
        seg = mask_bf16[:, j * LANES:(j + 1) * LANES]
        parts.append(jnp.dot(seg, strict_upper, preferred_element_type=F32) + offset)
        offset = offset + jnp.sum(seg.astype(F32), axis=1, keepdims=True)
    return jnp.concatenate(parts, axis=1)


def _route_kernel(aff_ref, h2_ref, pos_ref, xg_ref, gate_ref, *, cap):
    aff = aff_ref[0]
    bits = pltpu.bitcast(aff, jnp.int32)
    ne, t = aff.shape
    thr = jnp.zeros((ne, 1), jnp.int32)
    for bit in range(30, -1, -1):
        cand = thr | (1 << bit)
        cnt = jnp.sum(jnp.where(bits >= cand, 1.0, 0.0), axis=1, keepdims=True)
        thr = jnp.where(cnt >= cap, cand, thr)
    above = bits > thr
    tied = bits == thr
    need = cap - jnp.sum(jnp.where(above, 1.0, 0.0), axis=1, keepdims=True)
    ri = lax.broadcasted_iota(jnp.int32, (LANES, LANES), 0)
    ci = lax.broadcasted_iota(jnp.int32, (LANES, LANES), 1)
    strict_upper = jnp.where(ri < ci, 1.0, 0.0).astype(BF16)
    tie_rank = _prefix_count(jnp.where(tied, 1.0, 0.0).astype(BF16), strict_upper)
    sel = above | (tied & (tie_rank < need))
    slot = _prefix_count(jnp.where(sel, 1.0, 0.0).astype(BF16), strict_upper)
    pos = jnp.where(sel, slot.astype(jnp.int32), -1)
    pos_ref[0] = pos
    h2 = h2_ref[0]
    slot_id = lax.broadcasted_iota(jnp.int32, (cap, t), 0)
    for e in range(ne):
        hit = slot_id == pos[e:e + 1, :]
        onehot = jnp.where(hit, 1.0, 0.0).astype(BF16)
        xg_ref[e] = jnp.dot(onehot, h2, preferred_element_type=F32).astype(xg_ref.dtype)
        gate_ref[e] = jnp.sum(jnp.where(hit, aff[e:e + 1, :], 0.0), axis=1, keepdims=True)


def _route(aff_t, h2):
    bsz, ne, t = aff_t.shape
    d = h2.shape[2]
    cap = EC_CAPACITY_FACTOR * t // ne
    return pl.pallas_call(
        functools.partial(_route_kernel, cap=cap),
        grid=(bsz,),
        in_specs=[pl.BlockSpec((1, ne, t), lambda b: (b, 0, 0)),
                  pl.BlockSpec((1, t, d), lambda b: (b, 0, 0))],
        out_specs=[pl.BlockSpec((1, ne, t), lambda b: (b, 0, 0)),
                   pl.BlockSpec((ne, cap, d), lambda b: (0, b, 0)),
                   pl.BlockSpec((ne, cap, 1), lambda b: (0, b, 0))],
        out_shape=[jax.ShapeDtypeStruct((bsz, ne, t), jnp.int32),
                   jax.ShapeDtypeStruct((ne, bsz * cap, d), BF16),
                   jax.ShapeDtypeStruct((ne, bsz * cap, 1), F32)],
        compiler_params=_params(("parallel",)),
        name="route",
    )(aff_t, h2)


EXPERT_FF_TILE = 512
EXPERT_ROW_TILE = 512


def _expert_kernel(xg_ref, gate_ref, wg_ref, wu_ref, wd_ref, yo_ref, acc_ref):
    f = pl.program_id(1)
    wg, wu, wd = wg_ref[0].astype(BF16), wu_ref[0].astype(BF16), wd_ref[0].astype(BF16)
    n = xg_ref.shape[1]
    rt = min(EXPERT_ROW_TILE, n)
    for r in range(n // rt):
        rows = pl.ds(r * rt, rt)
        xg = xg_ref[0, rows, :]
        hid = _silu(jnp.dot(xg, wg, preferred_element_type=F32)) * jnp.dot(xg, wu, preferred_element_type=F32)
        part = jnp.dot(hid.astype(BF16), wd, preferred_element_type=F32)

        @pl.when(f == 0)
        def _():
            acc_ref[rows, :] = part

        @pl.when(f > 0)
        def _():
            acc_ref[rows, :] += part

    @pl.when(f == pl.num_programs(1) - 1)
    def _():
        cap = yo_ref.shape[1]
        for b in range(yo_ref.shape[0]):
            rows = pl.ds(b * cap, cap)
            yo_ref[b] = (acc_ref[rows, :] * gate_ref[0, rows, :]).astype(yo_ref.dtype)


def _experts(xg, gate, w_g, w_u, w_d, bsz):
    ne, n, d = xg.shape
    cap = n // bsz
    ff = w_g.shape[2]
    tf = EXPERT_FF_TILE
    return pl.pallas_call(
        _expert_kernel,
        grid=(ne, ff // tf),
        in_specs=[pl.BlockSpec((1, n, d), lambda e, f: (e, 0, 0)),
                  pl.BlockSpec((1, n, 1), lambda e, f: (e, 0, 0)),
                  pl.BlockSpec((1, d, tf), lambda e, f: (e, 0, f)),
                  pl.BlockSpec((1, d, tf), lambda e, f: (e, 0, f)),
                  pl.BlockSpec((1, tf, d), lambda e, f: (e, f, 0))],
        out_specs=pl.BlockSpec((bsz, cap, d), lambda e, f: (0, e, 0)),
        out_shape=jax.ShapeDtypeStruct((bsz, ne * cap, d), BF16),
        scratch_shapes=[pltpu.VMEM((n, d), F32)],
        compiler_params=_params(("parallel", "arbitrary")),
        name="experts",
    )(xg, gate, w_g, w_u, w_d)


def _combine_kernel(pos_ref, yo_ref, x1_ref, ga2_ref, npost_ref, o_ref, *, cap):
    pos = pos_ref[0]
    slot_id = lax.broadcasted_iota(jnp.int32, (1, cap), 1)
    onehot = jnp.concatenate(
        [jnp.where(pos[:, e:e + 1] == slot_id, 1.0, 0.0).astype(BF16) for e in range(pos.shape[1])], axis=1)
    moe = jnp.dot(onehot, yo_ref[0], preferred_element_type=F32)
    o_ref[0] = x1_ref[0] + ga2_ref[0] * _rms(moe, npost_ref[...])


def _combine(pos_t, yo, x1, mod3, n_post):
    bsz, seq, d = x1.shape
    ne = pos_t.shape[2]
    cap = yo.shape[1] // ne
    tm = 512
    return pl.pallas_call(
        functools.partial(_combine_kernel, cap=cap),
        grid=(bsz, seq // tm),
        in_specs=[pl.BlockSpec((1, tm, ne), lambda b, i: (b, i, 0)),
                  pl.BlockSpec((1, ne * cap, d), lambda b, i: (b, 0, 0)),
                  pl.BlockSpec((1, tm, d), lambda b, i: (b, i, 0)),
                  pl.BlockSpec((1, 1, d), lambda b, i: (b, 0, 5)),
                  pl.BlockSpec((1, d), lambda b, i: (0, 0))],
        out_specs=pl.BlockSpec((1, tm, d), lambda b, i: (b, i, 0)),
        out_shape=jax.ShapeDtypeStruct((bsz, seq, d), F32),
        compiler_params=_params(("parallel", "parallel")),
        name="combine",
    )(pos_t, yo, x1, mod3, n_post)


def kernel(x, c, ctx, c_ctx, w_ada, b_ada, norm_pre_mix, norm_post_mix, norm_pre_ffn, norm_post_ffn,
           w_in, na_rpb, ssd_conv_w, ssd_conv_b, ssd_a_log_fwd, ssd_a_log_bwd, ssd_dt_bias_fwd,
           ssd_dt_bias_bwd, ssd_d_skip, ssd_norm, w_branch_na, w_branch_ssd, w_out, w_router,
           w_exp_gate, w_exp_up, w_exp_down):
    mod, slab, o_na, y_ssd = _front(x, c, ctx, c_ctx, w_ada[0], b_ada[0], norm_pre_mix[0], w_in[0], na_rpb[0],
                                    ssd_conv_w[0], ssd_conv_b[0], ssd_a_log_fwd[0], ssd_a_log_bwd[0],
                                    ssd_dt_bias_fwd[0], ssd_dt_bias_bwd[0], ssd_d_skip[0], ssd_norm[0])
    return _back(x, mod, slab, o_na, y_ssd, norm_post_mix[0], norm_pre_ffn[0], norm_post_ffn[0],
                 w_branch_na[0], w_branch_ssd[0], w_out[0], w_router[0], w_exp_gate[0], w_exp_up[0], w_exp_down[0])


def _back(x, mod, slab, o_na, y_ssd, norm_post_mix, norm_pre_ffn, norm_post_ffn,
          w_branch_na, w_branch_ssd, w_out, w_router, w_exp_gate, w_exp_up, w_exp_down):
    bsz, seq, d = x.shape
    mod3 = mod.reshape(16, 1, N_MOD * d)
    wr_t = w_router.T
    wr_hi = wr_t.astype(BF16)
    wr_lo = (wr_t - wr_hi.astype(F32)).astype(BF16)
    x1, h2, aff_t = _merge(o_na, y_ssd, slab, x, mod3, w_branch_na.astype(BF16), w_branch_ssd.astype(BF16),
                           w_out.astype(BF16), norm_post_mix.reshape(1, d), norm_pre_ffn.reshape(1, d),
                           wr_hi, wr_lo)
    pos, xg, gate = _route(aff_t, h2)
    yo = _experts(xg, gate, w_exp_gate, w_exp_up, w_exp_down, bsz)
    return _combine(jnp.swapaxes(pos, 1, 2), yo, x1, mod3, norm_post_ffn.reshape(1, d))


def _group_lanes(fwd, bwd):
    r = fwd.shape[0]
    tiles = []
    for g in range(SSD_GROUPS):
        sl = slice(g * SSD_GHEADS, (g + 1) * SSD_GHEADS)
        tiles += [fwd[:, sl], bwd[:, sl], jnp.zeros((r, LANES - 2 * SSD_GHEADS), fwd.dtype)]
    return jnp.concatenate(tiles, axis=1)


def _front(x, c, ctx, c_ctx, w_ada, b_ada, norm_pre_mix, w_in, na_rpb, ssd_conv_w, ssd_conv_b,
           ssd_a_log_fwd, ssd_a_log_bwd, ssd_dt_bias_fwd, ssd_dt_bias_bwd, ssd_d_skip, ssd_norm):
    bsz, seq, d = x.shape
    tctx = ctx.shape[1]
    assert bsz <= 8
    cc =jnp.zeros((16, d), F32).at[:bsz].set(c).at[8].set(c_ctx)
    mod = _ada(cc, w_ada, b_ada)
    mod3 = mod.reshape(16, 1, N_MOD * d)

    dt0 = COL_G
    w_lat = jnp.concatenate([w_in[:, :dt0], w_in[:, dt0 + 2 * SSD_HEADS:]], axis=1).astype(BF16)
    w_ctx = jnp.concatenate([w_in[:, COL_K:COL_Z], w_in[:, COL_X:dt0]], axis=1).astype(BF16)
    w_dt = _group_lanes(w_in[:, dt0:dt0 + SSD_HEADS], w_in[:, dt0 + SSD_HEADS:dt0 + 2 * SSD_HEADS]).astype(BF16)
    rope = _rope_tables(seq)
    g_pre = norm_pre_mix.reshape(1, d)

    slab, dt = _inproj(x, mod3, 0, g_pre, w_lat, w_dt, rope, 2)
    nctx = bsz * tctx
    ctx_rows = -(-nctx // 1024) * 1024
    ctx_flat = ctx.reshape(1, nctx, d)
    if ctx_rows != nctx:
        ctx_flat = jnp.pad(ctx_flat, ((0, 0), (0, ctx_rows - nctx), (0, 0)))
    slab_c, dt_c = _inproj(ctx_flat, mod3, 8, g_pre, w_ctx, w_dt, rope, 0)
    slab_c = slab_c[0, :nctx].reshape(bsz, tctx, CTX_COLS)
    dt_c = dt_c[0, :nctx].reshape(bsz, tctx, -1)

    o_na = _na(slab, slab_c, _na_bias(na_rpb, seq // GRID_W))

    nx, nb = SSD_WIDTH, SSD_GROUPS * SSD_STATE
    cw = (ssd_conv_w[:, :nx], ssd_conv_w[:, nx:nx + nb], ssd_conv_w[:, nx + nb:])
    cb = tuple(v.reshape(1, -1) for v in (ssd_conv_b[:nx], ssd_conv_b[nx:nx + nb], ssd_conv_b[nx + nb:]))
    dtb = _group_lanes(ssd_dt_bias_fwd.reshape(1, -1), ssd_dt_bias_bwd.reshape(1, -1))
    alog = _group_lanes(ssd_a_log_fwd.reshape(1, -1), ssd_a_log_bwd.reshape(1, -1))
    dskip = jnp.repeat(ssd_d_skip, SSD_HEAD_DIM).reshape(1, -1)
    y_ssd = _ssd(slab, dt, slab_c, dt_c, cw, cb, dtb, alog, dskip, ssd_norm.reshape(1, -1))
    return mod, slab, o_na, y_ssd
```

```python
import functools
import math

import jax
import jax.numpy as jnp
import numpy as np
from jax import lax
from jax.experimental import pallas as pl
from jax.experimental.pallas import tpu as pltpu

F32 = jnp.float32
BF16 = jnp.bfloat16

D_MODEL = 1024
GRID_W = 64
NORM_EPS = 1e-6
N_MOD = 6
NA_HEADS = 16
NA_HEAD_DIM = 64
NA_KH = 8
NA_KW = 16
ROPE_BASE = 10000.0
SSD_WIDTH = 2048
SSD_HEAD_DIM = 64
SSD_HEADS = 32
SSD_GROUPS = 4
SSD_STATE = 128
SSD_CONV = 5
SSD_CHUNK = 128
N_EXPERTS = 16
EXPERT_FF = 2048
EC_CAPACITY_FACTOR = 2

VMEM_LIMIT_BYTES = 56 * 1024 * 1024
LANES = 128

COL_Q, COL_K, COL_V, COL_Z, COL_X, COL_B, COL_C, COL_G = 0, 1024, 2048, 3072, 5120, 7168, 7680, 8192
LAT_COLS = 10240
CTX_K, CTX_V, CTX_X, CTX_B, CTX_C = 0, 1024, 2048, 4096, 4608
CTX_COLS = 5120


def _params(semantics):
    return pltpu.CompilerParams(dimension_semantics=semantics, vmem_limit_bytes=VMEM_LIMIT_BYTES)


def _silu(v):
    return v * jax.nn.sigmoid(v)


def _softplus(v):
    return jnp.maximum(v, 0.0) + jnp.log1p(jnp.exp(-jnp.abs(v)))


def _ada_kernel(c_ref, w_ref, b_ref, o_ref):
    o_ref[...] = jnp.dot(_silu(c_ref[...]), w_ref[...], preferred_element_type=F32) + b_ref[...]


def _ada(cc, w_ada, b_ada):
    rows, d = cc.shape
    n = w_ada.shape[1]
    tn = 1536
    return pl.pallas_call(
        _ada_kernel,
        grid=(n // tn,),
        in_specs=[pl.BlockSpec((rows, d), lambda j: (0, 0)),
                  pl.BlockSpec((d, tn), lambda j: (0, j)),
                  pl.BlockSpec((1, tn), lambda j: (0, j))],
        out_specs=pl.BlockSpec((rows, tn), lambda j: (0, j)),
        out_shape=jax.ShapeDtypeStruct((rows, n), F32),
        compiler_params=_params(("arbitrary",)),
        name="ada",
    )(cc, w_ada, b_ada.reshape(1, n))


def _rope_tables(seq):
    lane = np.arange(LANES)
    axis = (lane % NA_HEAD_DIM) // 32
    within = lane % 32
    half = 16
    inv_freq = ROPE_BASE ** (-(within % half).astype(np.float64) / half)
    pos = np.arange(seq)
    coord = np.where(axis[None, :] == 0, (pos // GRID_W)[:, None], (pos % GRID_W)[:, None])
    ang = coord.astype(np.float32) * inv_freq.astype(np.float32)[None, :]
    cos, sin = jnp.cos(jnp.asarray(ang, F32)), jnp.sin(jnp.asarray(ang, F32))
    first = jnp.asarray(within < half)[None, :]
    return cos, jnp.where(first, -sin, 0.0), jnp.where(first, 0.0, sin)


def _inproj_kernel(x_ref, sh_ref, sc_ref, g_ref, w_ref, wdt_ref, cos_ref, s1_ref, s2_ref,
                   o_ref, dt_ref, h_scr, *, rope_tiles):
    j = pl.program_id(2)

    @pl.when(j == 0)
    def _():
        x = x_ref[0]
        xn = x * lax.rsqrt(jnp.mean(x * x, axis=-1, keepdims=True) + NORM_EPS) * g_ref[...]
        h = (xn * (1.0 + sc_ref[0]) + sh_ref[0]).astype(BF16)
        h_scr[...] = h
        dt_ref[0] = jnp.dot(h, wdt_ref[...], preferred_element_type=F32)

    acc = jnp.dot(h_scr[...], w_ref[...], preferred_element_type=F32)

    @pl.when(j >= rope_tiles)
    def _():
        o_ref[0] = acc.astype(o_ref.dtype)

    if rope_tiles:
        @pl.when(j < rope_tiles)
        def _():
            scale = jnp.where(j == 0, NA_HEAD_DIM ** -0.5, 1.0).astype(F32)
            cos, s1, s2 = cos_ref[...] * scale, s1_ref[...] * scale, s2_ref[...] * scale
            for c in range(acc.shape[1] // LANES):
                t = acc[:, c * LANES:(c + 1) * LANES]
                r = t * cos + pltpu.roll(t, LANES - 16, 1) * s1 + pltpu.roll(t, 16, 1) * s2
                o_ref[0, :, c * LANES:(c + 1) * LANES] = r.astype(o_ref.dtype)


def _inproj(x, mod3, mod_row0, g_pre, w, w_dt, rope, rope_tiles):
    bsz, seq, d = x.shape
    n = w.shape[1]
    ndt = w_dt.shape[1]
    tm, tn = 1024, 1024
    cos, s1, s2 = rope
    kern = functools.partial(_inproj_kernel, rope_tiles=rope_tiles)
    return pl.pallas_call(
        kern,
        grid=(bsz, seq // tm, n // tn),
        in_specs=[pl.BlockSpec((1, tm, d), lambda b, i, j: (b, i, 0)),
                  pl.BlockSpec((1, 1, d), lambda b, i, j: (b + mod_row0, 0, 0)),
                  pl.BlockSpec((1, 1, d), lambda b, i, j: (b + mod_row0, 0, 1)),
                  pl.BlockSpec((1, d), lambda b, i, j: (0, 0)),
                  pl.BlockSpec((d, tn), lambda b, i, j: (0, j)),
                  pl.BlockSpec((d, ndt), lambda b, i, j: (0, 0)),
                  pl.BlockSpec((tm, LANES), lambda b, i, j: (i, 0)),
                  pl.BlockSpec((tm, LANES), lambda b, i, j: (i, 0)),
                  pl.BlockSpec((tm, LANES), lambda b, i, j: (i, 0))],
        out_specs=[pl.BlockSpec((1, tm, tn), lambda b, i, j: (b, i, j)),
                   pl.BlockSpec((1, tm, ndt), lambda b, i, j: (b, i, 0))],
        out_shape=[jax.ShapeDtypeStruct((bsz, seq, n), BF16),
                   jax.ShapeDtypeStruct((bsz, seq, ndt), F32)],
        scratch_shapes=[pltpu.VMEM((tm, d), BF16)],
        compiler_params=_params(("parallel", "parallel", "arbitrary")),
        name="inproj",
    )(x, mod3, mod3, g_pre, w, w_dt, cos, s1, s2)


NA_QROWS = 4
NA_KROWS = 12
NA_QBLK = NA_QROWS * GRID_W
NA_KBLK = NA_KROWS * GRID_W
NA_MASKED = -1e30


def _na_key_start(g, rows):
    return min(max(NA_QROWS * g - NA_KH // 2, 0), rows - NA_KROWS)


def _na_bias(rpb, rows):
    heads = rpb.shape[0]
    col = np.arange(GRID_W)
    col_start = np.clip(col - NA_KW // 2, 0, GRID_W - NA_KW)
    col_ok = (col[None, :] >= col_start[:, None]) & (col[None, :] < col_start[:, None] + NA_KW)
    edge = GRID_W - NA_KW
    padded = jnp.pad(rpb, ((0, 0), (0, 0), (edge, edge)), mode="edge")
    toe = jnp.stack([padded[:, :, GRID_W - 1 - cq:2 * GRID_W - 1 - cq] for cq in range(GRID_W)], axis=2)
    nblk = rows // NA_QROWS
    out = []
    for g in (0, 1, nblk - 1):
        r = NA_QROWS * g + np.arange(NA_QROWS)
        kr = _na_key_start(g, rows) + np.arange(NA_KROWS)
        r0 = np.clip(r - NA_KH // 2, 0, rows - NA_KH)
        row_ok = (kr[None, :] >= r0[:, None]) & (kr[None, :] < r0[:, None] + NA_KH)
        row_idx = np.clip(kr[None, :] - r[:, None] + NA_KH - 1, 0, 2 * NA_KH - 2)
        b = jnp.stack([jnp.stack([toe[:, int(row_idx[dr, i])] for i in range(NA_KROWS)], axis=2)
                       for dr in range(NA_QROWS)], axis=1)
        ok = row_ok[:, None, :, None] & col_ok[None, :, None, :]
        out.append(jnp.where(ok[None], b, NA_MASKED).reshape(heads, NA_QBLK, NA_KBLK))
    return jnp.stack(out).astype(F32)


def _na_kernel(q_ref, k_ref, v_ref, kc_ref, vc_ref, bias_ref, o_ref, *, rows):
    nblk = rows // NA_QROWS
    first_head = lax.broadcasted_iota(jnp.int32, (1, LANES), 1) < NA_HEAD_DIM
    kc, vc = kc_ref[0], vc_ref[0]
    nt = (((1,), (1,)), ((), ()))
    for g in range(nblk):
        k0 = _na_key_start(g, rows) * GRID_W
        geom = 0 if g == 0 else (2 if g == nblk - 1 else 1)
        q = q_ref[0, g * NA_QBLK:(g + 1) * NA_QBLK, :]
        kw = k_ref[0, k0:k0 + NA_KBLK, :]
        vw = v_ref[0, k0:k0 + NA_KBLK, :]
        outs = []
        for hh in range(2):
            qm = jnp.where(first_head if hh == 0 else ~first_head, q, jnp.zeros_like(q))
            s = lax.dot_general(qm, kw, nt, preferred_element_type=F32) + bias_ref[geom, hh]
            sc = lax.dot_general(qm, kc, nt, preferred_element_type=F32)
            m = jnp.maximum(jnp.max(s, axis=-1, keepdims=True), jnp.max(sc, axis=-1, keepdims=True))
            p, pc = jnp.exp(s - m), jnp.exp(sc - m)
            denom = jnp.sum(p, axis=-1, keepdims=True) + jnp.sum(pc, axis=-1, keepdims=True)
            o = (jnp.dot(p.astype(BF16), vw, preferred_element_type=F32)
                 + jnp.dot(pc.astype(BF16), vc, preferred_element_type=F32))
            outs.append(o / denom)
        o_ref[0, g * NA_QBLK:(g + 1) * NA_QBLK, :] = jnp.where(first_head, outs[0], outs[1]).astype(o_ref.dtype)


def _na(slab, slab_ctx, bias):
    bsz, seq, _ = slab.shape
    tctx = slab_ctx.shape[1]
    pairs = NA_HEADS // 2
    blk = lambda col0: (lambda hp, b: (b, 0, col0 // LANES + hp))
    return pl.pallas_call(
        functools.partial(_na_kernel, rows=seq // GRID_W),
        grid=(pairs, bsz),
        in_specs=[pl.BlockSpec((1, seq, LANES), blk(COL_Q)),
                  pl.BlockSpec((1, seq, LANES), blk(COL_K)),
                  pl.BlockSpec((1, seq, LANES), blk(COL_V)),
                  pl.BlockSpec((1, tctx, LANES), blk(CTX_K)),
                  pl.BlockSpec((1, tctx, LANES), blk(CTX_V)),
                  pl.BlockSpec((3, 2, NA_QBLK, NA_KBLK), lambda hp, b: (0, hp, 0, 0))],
        out_specs=pl.BlockSpec((1, seq, LANES), lambda hp, b: (b, 0, hp)),
        out_shape=jax.ShapeDtypeStruct((bsz, seq, NA_HEADS * NA_HEAD_DIM), BF16),
        compiler_params=_params(("parallel", "parallel")),
        name="na",
    )(slab, slab, slab, slab_ctx, slab_ctx, bias)


SSD_GHEADS = SSD_HEADS // SSD_GROUPS
SSD_GWIDTH = SSD_GHEADS * SSD_HEAD_DIM
CONV_HALO = 8


def _split3(v):
    hi = v.astype(BF16)
    r1 = v - hi.astype(F32)
    mid = r1.astype(BF16)
    lo = (r1 - mid.astype(F32)).astype(BF16)
    return hi, mid, lo


def _ssd_kernel(xs_ref, b_ref, c_ref, z_ref, dt_ref, xsc_ref, bc_ref, cc_ref, dtc_ref,
                cwx_ref, cwb_ref, cwc_ref, cbx_ref, cbb_ref, cbc_ref, dtb_ref, alog_ref, dskip_ref, nrm_ref,
                y_ref,
                padx, padb, padc, xs_s, b_s, c_s, dt_s, acs_s, ex_s, hprev_s, sb_s, decb_s, hf_s, hb_s,
                *, seq, tctx):
    Q = SSD_CHUNK
    GH = SSD_GHEADS
    rows = lax.broadcasted_iota(jnp.int32, (Q, Q), 0)
    cols = lax.broadcasted_iota(jnp.int32, (Q, Q), 1)
    tril = jnp.where(rows >= cols, 1.0, 0.0).astype(BF16)
    lane = lax.broadcasted_iota(jnp.int32, (1, LANES), 1)
    is_fwd = lane < GH
    first_head = lane < SSD_HEAD_DIM
    a_coef = -jnp.exp(alog_ref[...])
    dt_bias = dtb_ref[...]

    def expand(f, lane0):
        tiles = [jnp.where(first_head, f[:, lane0 + 2 * j:lane0 + 2 * j + 1], f[:, lane0 + 2 * j + 1:lane0 + 2 * j + 2])
                 for j in range(GH // 2)]
        return jnp.concatenate(tiles, axis=1)

    def prep(x_raw, b_raw, c_raw, dt_raw, n):
        for pad, raw in ((padx, x_raw), (padb, b_raw), (padc, c_raw)):
            width = pad.shape[1]
            pad[0:CONV_HALO, :] = jnp.zeros((CONV_HALO, width), F32)
            pad[CONV_HALO + n:2 * CONV_HALO + n, :] = jnp.zeros((CONV_HALO, width), F32)

        def stage(c, carry):
            r0 = pl.multiple_of(c * Q, Q)
            for pad, raw in ((padx, x_raw), (padb, b_raw), (padc, c_raw)):
                pad[pl.ds(r0 + CONV_HALO, Q), :] = raw[0, pl.ds(r0, Q), :].astype(F32)
            return carry

        lax.fori_loop(0, n // Q, stage, 0)

        def body(c, carry):
            r0 = pl.multiple_of(c * Q, Q)
            for pad, w_ref, bias_ref, dst in ((padx, cwx_ref, cbx_ref, xs_s), (padb, cwb_ref, cbb_ref, b_s),
                                              (padc, cwc_ref, cbc_ref, c_s)):
                win = pad[pl.ds(r0, Q + 2 * CONV_HALO), :]
                acc = bias_ref[...] + win[CONV_HALO - 2:CONV_HALO - 2 + Q, :] * w_ref[0:1, :]
                for k in range(1, SSD_CONV):
                    acc = acc + win[CONV_HALO - 2 + k:CONV_HALO - 2 + k + Q, :] * w_ref[k:k + 1, :]
                dst[pl.ds(r0, Q), :] = _silu(acc)
            dt = _softplus(dt_raw[0, pl.ds(r0, Q), :] + dt_bias)
            a = dt * a_coef
            cs = jnp.dot(tril, jnp.concatenate(_split3(a), axis=1), preferred_element_type=F32)
            acs = cs[:, 0:LANES] + cs[:, LANES:2 * LANES] + cs[:, 2 * LANES:3 * LANES]
            dt_s[pl.ds(r0, Q), :] = dt
            acs_s[pl.ds(r0, Q), :] = acs
            ex_s[pl.ds(r0, Q), :] = acs - a
            return carry

        lax.fori_loop(0, n // Q, body, 0)

    def chunk_states(n, store_prev):
        def body(c, carry):
            r0 = pl.multiple_of(c * Q, Q)
            acs, ex, dt = acs_s[pl.ds(r0, Q), :], ex_s[pl.ds(r0, Q), :], dt_s[pl.ds(r0, Q), :]
            tot = acs[Q - 1:Q, :]
            w = dt * jnp.exp(jnp.where(is_fwd, tot - acs, ex))
            xs = xs_s[pl.ds(r0, Q), :]
            xdec = jnp.concatenate([xs * expand(w, 0), xs * expand(w, GH)], axis=1).astype(BF16)
            b_t = b_s[pl.ds(r0, Q), :].T.astype(BF16)
            s = jnp.dot(b_t, xdec, preferred_element_type=F32)
            dec = jnp.exp(tot)
            if store_prev:
                hprev_s[c] = hf_s[...].astype(BF16)
            hf_s[...] = expand(dec, 0) * hf_s[...] + s[:, :SSD_GWIDTH]
            sb_s[c] = s[:, SSD_GWIDTH:]
            decb_s[c] = jnp.broadcast_to(expand(dec, GH), (8, SSD_GWIDTH))
            return carry

        lax.fori_loop(0, n // Q, body, 0)

    def backward_step(c):
        hb_s[...] = decb_s[c][0:1, :] * hb_s[...] + sb_s[c]

    hf_s[...] = jnp.zeros(hf_s.shape, F32)
    hb_s[...] = jnp.zeros(hb_s.shape, F32)

    prep(xsc_ref, bc_ref, cc_ref, dtc_ref, tctx)
    chunk_states(tctx, False)
    nctx = tctx // Q

    def ctx_back(i, carry):
        backward_step(nctx - 1 - i)
        return carry

    lax.fori_loop(0, nctx, ctx_back, 0)

    prep(xs_ref, b_ref, c_ref, dt_ref, seq)
    chunk_states(seq, True)
    nch = seq // Q
    dskip = dskip_ref[...]
    gain = nrm_ref[...]

    def ybody(i, carry):
        c = nch - 1 - i
        r0 = pl.multiple_of(c * Q, Q)
        acs, ex, dt = acs_s[pl.ds(r0, Q), :], ex_s[pl.ds(r0, Q), :], dt_s[pl.ds(r0, Q), :]
        acs_t, ex_t, dt_t = acs.T, ex.T, dt.T
        xs = xs_s[pl.ds(r0, Q), :]
        xsb = xs.astype(BF16)
        cm = c_s[pl.ds(r0, Q), :].astype(BF16)
        b_t = b_s[pl.ds(r0, Q), :].T.astype(BF16)
        cb = jnp.dot(cm, b_t, preferred_element_type=F32)
        tiles = []
        for j in range(GH // 2):
            xpair = xsb[:, j * LANES:(j + 1) * LANES]
            ys = []
            for hh in range(2):
                hf, hb = 2 * j + hh, GH + 2 * j + hh
                arg_f = acs[:, hf:hf + 1] - acs_t[hf:hf + 1, :]
                arg_b = ex_t[hb:hb + 1, :] - ex[:, hb:hb + 1]
                e = jnp.exp(jnp.where(rows >= cols, arg_f, arg_b))
                dt_f, dt_b = dt_t[hf:hf + 1, :], dt_t[hb:hb + 1, :]
                fac = jnp.where(rows > cols, dt_f, jnp.where(rows < cols, dt_b, dt_f + dt_b))
                m = (cb * e * fac).astype(BF16)
                ys.append(jnp.dot(m, xpair, preferred_element_type=F32))
            tiles.append(jnp.where(first_head, ys[0], ys[1]))
        y = jnp.concatenate(tiles, axis=1)
        tot = acs[Q - 1:Q, :]
        e_out = jnp.exp(jnp.where(is_fwd, acs, tot - ex))
        y = y + jnp.dot(cm, hprev_s[c], preferred_element_type=F32) * expand(e_out, 0)
        y = y + jnp.dot(cm, hb_s[...].astype(BF16), preferred_element_type=F32) * expand(e_out, GH)
        y = y + dskip * xs
        u = y * _silu(z_ref[0, pl.ds(r0, Q), :].astype(F32))
        u = u * lax.rsqrt(jnp.mean(u * u, axis=-1, keepdims=True) + NORM_EPS) * gain
        y_ref[0, pl.ds(r0, Q), :] = u.astype(y_ref.dtype)
        backward_step(c)
        return carry

    lax.fori_loop(0, nch, ybody, 0)


def _ssd(slab, dt, slab_ctx, dt_ctx, cw, cb, dtb, alog, dskip, nrm):
    bsz, seq, _ = slab.shape
    tctx = slab_ctx.shape[1]
    gw, ns = SSD_GWIDTH, SSD_STATE
    nch = seq // SSD_CHUNK
    cwx, cwb, cwc = cw
    cbx, cbb, cbc = cb

    def tok(width, col0, n):
        return pl.BlockSpec((1, n, width), lambda b, g: (b, 0, col0 // width + g))

    def par(rows, width):
        return pl.BlockSpec((rows, width), lambda b, g: (0, g))

    pad = seq + 2 * CONV_HALO
    return pl.pallas_call(
        functools.partial(_ssd_kernel, seq=seq, tctx=tctx),
        grid=(bsz, SSD_GROUPS),
        in_specs=[tok(gw, COL_X, seq), tok(ns, COL_B, seq), tok(ns, COL_C, seq), tok(gw, COL_Z, seq),
                  tok(LANES, 0, seq),
                  tok(gw, CTX_X, tctx), tok(ns, CTX_B, tctx), tok(ns, CTX_C, tctx), tok(LANES, 0, tctx),
                  par(SSD_CONV, gw), par(SSD_CONV, ns), par(SSD_CONV, ns),
                  par(1, gw), par(1, ns), par(1, ns),
                  par(1, LANES), par(1, LANES), par(1, gw), par(1, gw)],
        out_specs=pl.BlockSpec((1, seq, gw), lambda b, g: (b, 0, g)),
        out_shape=jax.ShapeDtypeStruct((bsz, seq, SSD_WIDTH), BF16),
        scratch_shapes=[pltpu.VMEM((pad, gw), F32), pltpu.VMEM((pad, ns), F32), pltpu.VMEM((pad, ns), F32),
                        pltpu.VMEM((seq, gw), F32), pltpu.VMEM((seq, ns), F32), pltpu.VMEM((seq, ns), F32),
                        pltpu.VMEM((seq, LANES), F32), pltpu.VMEM((seq, LANES), F32), pltpu.VMEM((seq, LANES), F32),
                        pltpu.VMEM((nch, ns, gw), BF16), pltpu.VMEM((nch, ns, gw), F32),
                        pltpu.VMEM((nch, 8, gw), F32),
                        pltpu.VMEM((ns, gw), F32), pltpu.VMEM((ns, gw), F32)],
        compiler_params=_params(("parallel", "parallel")),
        name="ssd",
    )(slab, slab, slab, slab, dt, slab_ctx, slab_ctx, slab_ctx, dt_ctx,
      cwx, cwb, cwc, cbx, cbb, cbc, dtb, alog, dskip, nrm)


def _rms(v, gain):
    return v * lax.rsqrt(jnp.mean(v * v, axis=-1, keepdims=True) + NORM_EPS) * gain


def _merge_kernel(ona_ref, y_ref, gna_ref, gssd_ref, x_ref, ga1_ref, sh2_ref, sc2_ref,
                  wna_ref, wssd_ref, wout_ref, npost_ref, npre_ref, wrh_ref, wrl_ref,
                  x1_ref, h2_ref, aff_ref):
    a = jnp.dot(ona_ref[0], wna_ref[...], preferred_element_type=F32)
    s = jnp.dot(y_ref[0], wssd_ref[...], preferred_element_type=F32)
    u = jax.nn.sigmoid(gna_ref[0].astype(F32)) * a + jax.nn.sigmoid(gssd_ref[0].astype(F32)) * s
    mix = jnp.dot(u.astype(BF16), wout_ref[...], preferred_element_type=F32)
    x1 = x_ref[0] + ga1_ref[0] * _rms(mix, npost_ref[...])
    x1_ref[0] = x1
    h2 = _rms(x1, npre_ref[...]) * (1.0 + sc2_ref[0]) + sh2_ref[0]
    hi = h2.astype(BF16)
    h2_ref[0] = hi
    lo = (h2 - hi.astype(F32)).astype(BF16)
    nt = (((1,), (1,)), ((), ()))
    wrh, wrl = wrh_ref[...], wrl_ref[...]
    logits = (lax.dot_general(wrh, hi, nt, preferred_element_type=F32)
              + lax.dot_general(wrh, lo, nt, preferred_element_type=F32)
              + lax.dot_general(wrl, hi, nt, preferred_element_type=F32))
    e = jnp.exp(logits - jnp.max(logits, axis=0, keepdims=True))
    aff_ref[0] = e / jnp.sum(e, axis=0, keepdims=True)


def _merge(o_na, y_ssd, slab, x, mod3, w_na, w_ssd, w_o, n_post, n_pre, wr_hi, wr_lo):
    bsz, seq, d = x.shape
    tm = 512
    ne = wr_hi.shape[0]
    tok = lambda width, blk: pl.BlockSpec((1, tm, width), lambda b, i: (b, i, blk))
    modv = lambda k: pl.BlockSpec((1, 1, d), lambda b, i: (b, 0, k))
    full = lambda r, c_: pl.BlockSpec((r, c_), lambda b, i: (0, 0))
    return pl.pallas_call(
        _merge_kernel,
        grid=(bsz, seq // tm),
        in_specs=[tok(d, 0), tok(SSD_WIDTH, 0), tok(d, COL_G // d), tok(d, COL_G // d + 1), tok(d, 0),
                  modv(2), modv(3), modv(4),
                  full(d, d), full(SSD_WIDTH, d), full(d, d), full(1, d), full(1, d), full(ne, d), full(ne, d)],
        out_specs=[tok(d, 0), tok(d, 0), pl.BlockSpec((1, ne, tm), lambda b, i: (b, 0, i))],
        out_shape=[jax.ShapeDtypeStruct((bsz, seq, d), F32), jax.ShapeDtypeStruct((bsz, seq, d), BF16),
                   jax.ShapeDtypeStruct((bsz, ne, seq), F32)],
        compiler_params=_params(("parallel", "parallel")),
        name="merge",
    )(o_na, y_ssd, slab, slab, x, mod3, mod3, mod3, w_na, w_ssd, w_o, n_post, n_pre, wr_hi, wr_lo)


def _prefix_count(mask_bf16, strict_upper):
    r, s = mask_bf16.shape
    offset = jnp.zeros((r, 1), F32)
    parts = []
    for j in range(s // LANES):
        seg = mask_bf16[:, j * LANES:(j + 1) * LANES]
        parts.append(jnp.dot(seg, strict_upper, preferred_element_type=F32) + offset)
        offset = offset + jnp.sum(seg.astype(F32), axis=1, keepdims=True)
    return jnp.concatenate(parts, axis=1)


def _route_kernel(aff_ref, h2_ref, pos_ref, xg_ref, gate_ref, *, cap):
    aff = aff_ref[0]
    bits = pltpu.bitcast(aff, jnp.int32)
    ne, t = aff.shape
    thr = jnp.zeros((ne, 1), jnp.int32)
    for bit in range(30, -1, -1):
        cand = thr | (1 << bit)
        cnt = jnp.sum(jnp.where(bits >= cand, 1.0, 0.0), axis=1, keepdims=True)
        thr = jnp.where(cnt >= cap, cand, thr)
    above = bits > thr
    tied = bits == thr
    need = cap - jnp.sum(jnp.where(above, 1.0, 0.0), axis=1, keepdims=True)
    ri = lax.broadcasted_iota(jnp.int32, (LANES, LANES), 0)
    ci = lax.broadcasted_iota(jnp.int32, (LANES, LANES), 1)
    strict_upper = jnp.where(ri < ci, 1.0, 0.0).astype(BF16)
    tie_rank = _prefix_count(jnp.where(tied, 1.0, 0.0).astype(BF16), strict_upper)
    sel = above | (tied & (tie_rank < need))
    slot = _prefix_count(jnp.where(sel, 1.0, 0.0).astype(BF16), strict_upper)
    pos = jnp.where(sel, slot.astype(jnp.int32), -1)
    pos_ref[0] = pos
    h2 = h2_ref[0]
    slot_id = lax.broadcasted_iota(jnp.int32, (cap, t), 0)
    for e in range(ne):
        hit = slot_id == pos[e:e + 1, :]
        onehot = jnp.where(hit, 1.0, 0.0).astype(BF16)
        xg_ref[e] = jnp.dot(onehot, h2, preferred_element_type=F32).astype(xg_ref.dtype)
        gate_ref[e] = jnp.sum(jnp.where(hit, aff[e:e + 1, :], 0.0), axis=1, keepdims=True)


def _route(aff_t, h2):
    bsz, ne, t = aff_t.shape
    d = h2.shape[2]
    cap = EC_CAPACITY_FACTOR * t // ne
    return pl.pallas_call(
        functools.partial(_route_kernel, cap=cap),
        grid=(bsz,),
        in_specs=[pl.BlockSpec((1, ne, t), lambda b: (b, 0, 0)),
                  pl.BlockSpec((1, t, d), lambda b: (b, 0, 0))],
        out_specs=[pl.BlockSpec((1, ne, t), lambda b: (b, 0, 0)),
                   pl.BlockSpec((ne, cap, d), lambda b: (0, b, 0)),
                   pl.BlockSpec((ne, cap, 1), lambda b: (0, b, 0))],
        out_shape=[jax.ShapeDtypeStruct((bsz, ne, t), jnp.int32),
                   jax.ShapeDtypeStruct((ne, bsz * cap, d), BF16),
                   jax.ShapeDtypeStruct((ne, bsz * cap, 1), F32)],
        compiler_params=_params(("parallel",)),
        name="route",
    )(aff_t, h2)


EXPERT_FF_TILE = 512
EXPERT_ROW_TILE = 512


def _expert_kernel(xg_ref, gate_ref, wg_ref, wu_ref, wd_ref, yo_ref, acc_ref):
    f = pl.program_id(1)
    wg, wu, wd = wg_ref[0].astype(BF16), wu_ref[0].astype(BF16), wd_ref[0].astype(BF16)
    n = xg_ref.shape[1]
    rt = min(EXPERT_ROW_TILE, n)
    for r in range(n // rt):
        rows = pl.ds(r * rt, rt)
        xg = xg_ref[0, rows, :]
        hid = _silu(jnp.dot(xg, wg, preferred_element_type=F32)) * jnp.dot(xg, wu, preferred_element_type=F32)
        part = jnp.dot(hid.astype(BF16), wd, preferred_element_type=F32)

        @pl.when(f == 0)
        def _():
            acc_ref[rows, :] = part

        @pl.when(f > 0)
        def _():
            acc_ref[rows, :] += part

    @pl.when(f == pl.num_programs(1) - 1)
    def _():
        cap = yo_ref.shape[1]
        for b in range(yo_ref.shape[0]):
            rows = pl.ds(b * cap, cap)
            yo_ref[b] = (acc_ref[rows, :] * gate_ref[0, rows, :]).astype(yo_ref.dtype)


def _experts(xg, gate, w_g, w_u, w_d, bsz):
    ne, n, d = xg.shape
    cap = n // bsz
    ff = w_g.shape[2]
    tf = EXPERT_FF_TILE
    return pl.pallas_call(
        _expert_kernel,
        grid=(ne, ff // tf),
        in_specs=[pl.BlockSpec((1, n, d), lambda e, f: (e, 0, 0)),
                  pl.BlockSpec((1, n, 1), lambda e, f: (e, 0, 0)),
                  pl.BlockSpec((1, d, tf), lambda e, f: (e, 0, f)),
                  pl.BlockSpec((1, d, tf), lambda e, f: (e, 0, f)),
                  pl.BlockSpec((1, tf, d), lambda e, f: (e, f, 0))],
        out_specs=pl.BlockSpec((bsz, cap, d), lambda e, f: (0, e, 0)),
        out_shape=jax.ShapeDtypeStruct((bsz, ne * cap, d), BF16),
        scratch_shapes=[pltpu.VMEM((n, d), F32)],
        compiler_params=_params(("parallel", "arbitrary")),
        name="experts",
    )(xg, gate, w_g, w_u, w_d)


def _combine_kernel(pos_ref, yo_ref, x1_ref, ga2_ref, npost_ref, o_ref, *, cap):
    pos = pos_ref[0]
    slot_id = lax.broadcasted_iota(jnp.int32, (1, cap), 1)
    onehot = jnp.concatenate(
        [jnp.where(pos[:, e:e + 1] == slot_id, 1.0, 0.0).astype(BF16) for e in range(pos.shape[1])], axis=1)
    moe = jnp.dot(onehot, yo_ref[0], preferred_element_type=F32)
    o_ref[0] = x1_ref[0] + ga2_ref[0] * _rms(moe, npost_ref[...])


def _combine(pos_t, yo, x1, mod3, n_post):
    bsz, seq, d = x1.shape
    ne = pos_t.shape[2]
    cap = yo.shape[1] // ne
    tm = 512
    return pl.pallas_call(
        functools.partial(_combine_kernel, cap=cap),
        grid=(bsz, seq // tm),
        in_specs=[pl.BlockSpec((1, tm, ne), lambda b, i: (b, i, 0)),
                  pl.BlockSpec((1, ne * cap, d), lambda b, i: (b, 0, 0)),
                  pl.BlockSpec((1, tm, d), lambda b, i: (b, i, 0)),
                  pl.BlockSpec((1, 1, d), lambda b, i: (b, 0, 5)),
                  pl.BlockSpec((1, d), lambda b, i: (0, 0))],
        out_specs=pl.BlockSpec((1, tm, d), lambda b, i: (b, i, 0)),
        out_shape=jax.ShapeDtypeStruct((bsz, seq, d), F32),
        compiler_params=_params(("parallel", "parallel")),
        name="combine",
    )(pos_t, yo, x1, mod3, n_post)


def kernel(x, c, ctx, c_ctx, w_ada, b_ada, norm_pre_mix, norm_post_mix, norm_pre_ffn, norm_post_ffn,
           w_in, na_rpb, ssd_conv_w, ssd_conv_b, ssd_a_log_fwd, ssd_a_log_bwd, ssd_dt_bias_fwd,
           ssd_dt_bias_bwd, ssd_d_skip, ssd_norm, w_branch_na, w_branch_ssd, w_out, w_router,
           w_exp_gate, w_exp_up, w_exp_down):
    mod, slab, o_na, y_ssd = _front(x, c, ctx, c_ctx, w_ada[0], b_ada[0], norm_pre_mix[0], w_in[0], na_rpb[0],
                                    ssd_conv_w[0], ssd_conv_b[0], ssd_a_log_fwd[0], ssd_a_log_bwd[0],
                                    ssd_dt_bias_fwd[0], ssd_dt_bias_bwd[0], ssd_d_skip[0], ssd_norm[0])
    return _back(x, mod, slab, o_na, y_ssd, norm_post_mix[0], norm_pre_ffn[0], norm_post_ffn[0],
                 w_branch_na[0], w_branch_ssd[0], w_out[0], w_router[0], w_exp_gate[0], w_exp_up[0], w_exp_down[0])


def _back(x, mod, slab, o_na, y_ssd, norm_post_mix, norm_pre_ffn, norm_post_ffn,
          w_branch_na, w_branch_ssd, w_out, w_router, w_exp_gate, w_exp_up, w_exp_down):
    bsz, seq, d = x.shape
    mod3 = mod.reshape(16, 1, N_MOD * d)
    wr_t = w_router.T
    wr_hi = wr_t.astype(BF16)
    wr_lo = (wr_t - wr_hi.astype(F32)).astype(BF16)
    x1, h2, aff_t = _merge(o_na, y_ssd, slab, x, mod3, w_branch_na.astype(BF16), w_branch_ssd.astype(BF16),
                           w_out.astype(BF16), norm_post_mix.reshape(1, d), norm_pre_ffn.reshape(1, d),
                           wr_hi, wr_lo)
    pos, xg, gate = _route(aff_t, h2)
    yo = _experts(xg, gate, w_exp_gate, w_exp_up, w_exp_down, bsz)
    return _combine(jnp.swapaxes(pos, 1, 2), yo, x1, mod3, norm_post_ffn.reshape(1, d))


def _group_lanes(fwd, bwd):
    r = fwd.shape[0]
    tiles = []
    for g in range(SSD_GROUPS):
        sl = slice(g * SSD_GHEADS, (g + 1) * SSD_GHEADS)
        tiles += [fwd[:, sl], bwd[:, sl], jnp.zeros((r, LANES - 2 * SSD_GHEADS), fwd.dtype)]
    return jnp.concatenate(tiles, axis=1)


def _front(x, c, ctx, c_ctx, w_ada, b_ada, norm_pre_mix, w_in, na_rpb, ssd_conv_w, ssd_conv_b,
           ssd_a_log_fwd, ssd_a_log_bwd, ssd_dt_bias_fwd, ssd_dt_bias_bwd, ssd_d_skip, ssd_norm):
    bsz, seq, d = x.shape
    tctx = ctx.shape[1]
    assert bsz <= 8
    cc =jnp.zeros((16, d), F32).at[:bsz].set(c).at[8].set(c_ctx)
    mod = _ada(cc, w_ada, b_ada)
    mod3 = mod.reshape(16, 1, N_MOD * d)

    dt0 = COL_G
    w_lat = jnp.concatenate([w_in[:, :dt0], w_in[:, dt0 + 2 * SSD_HEADS:]], axis=1).astype(BF16)
    w_ctx = jnp.concatenate([w_in[:, COL_K:COL_Z], w_in[:, COL_X:dt0]], axis=1).astype(BF16)
    w_dt = _group_lanes(w_in[:, dt0:dt0 + SSD_HEADS], w_in[:, dt0 + SSD_HEADS:dt0 + 2 * SSD_HEADS]).astype(BF16)
    rope = _rope_tables(seq)
    g_pre = norm_pre_mix.reshape(1, d)

    slab, dt = _inproj(x, mod3, 0, g_pre, w_lat, w_dt, rope, 2)
    nctx = bsz * tctx
    ctx_rows = -(-nctx // 1024) * 1024
    ctx_flat = ctx.reshape(1, nctx, d)
    if ctx_rows != nctx:
        ctx_flat = jnp.pad(ctx_flat, ((0, 0), (0, ctx_rows - nctx), (0, 0)))
    slab_c, dt_c = _inproj(ctx_flat, mod3, 8, g_pre, w_ctx, w_dt, rope, 0)
    slab_c = slab_c[0, :nctx].reshape(bsz, tctx, CTX_COLS)
    dt_c = dt_c[0, :nctx].reshape(bsz, tctx, -1)

    o_na = _na(slab, slab_c, _na_bias(na_rpb, seq // GRID_W))

    nx, nb = SSD_WIDTH, SSD_GROUPS * SSD_STATE
    cw = (ssd_conv_w[:, :nx], ssd_conv_w[:, nx:nx + nb], ssd_conv_w[:, nx + nb:])
    cb = tuple(v.reshape(1, -1) for v in (ssd_conv_b[:nx], ssd_conv_b[nx:nx + nb], ssd_conv_b[nx + nb:]))
    dtb = _group_lanes(ssd_dt_bias_fwd.reshape(1, -1), ssd_dt_bias_bwd.reshape(1, -1))
    alog = _group_lanes(ssd_a_log_fwd.reshape(1, -1), ssd_a_log_bwd.reshape(1, -1))
    dskip = jnp.repeat(ssd_d_skip, SSD_HEAD_DIM).reshape(1, -1)
    y_ssd = _ssd(slab, dt, slab_c, dt_c, cw, cb, dtb, alog, dskip, ssd_norm.reshape(1, -1))
    return mod, slab, o_na, y_ssd
```

```python
import functools
import math

import jax
import jax.numpy as jnp
import numpy as np
from jax import lax
from jax.experimental import pallas as pl
from jax.experimental.pallas import tpu as pltpu

F32 = jnp.float32
BF16 = jnp.bfloat16

D_MODEL = 1024
GRID_W = 64
NORM_EPS = 1e-6
N_MOD = 6
NA_HEADS = 16
NA_HEAD_DIM = 64
NA_KH = 8
NA_KW = 16
ROPE_BASE = 10000.0
SSD_WIDTH = 2048
SSD_HEAD_DIM = 64
SSD_HEADS = 32
SSD_GROUPS = 4
SSD_STATE = 128
SSD_CONV = 5
SSD_CHUNK = 128
N_EXPERTS = 16
EXPERT_FF = 2048
EC_CAPACITY_FACTOR = 2

VMEM_LIMIT_BYTES = 56 * 1024 * 1024
LANES = 128

COL_Q, COL_K, COL_V, COL_Z, COL_X, COL_B, COL_C, COL_G = 0, 1024, 2048, 3072, 5120, 7168, 7680, 8192
LAT_COLS = 10240
CTX_K, CTX_V, CTX_X, CTX_B, CTX_C = 0, 1024, 2048, 4096, 4608
CTX_COLS = 5120


def _params(semantics):
    return pltpu.CompilerParams(dimension_semantics=semantics, vmem_limit_bytes=VMEM_LIMIT_BYTES)


def _silu(v):
    return v * jax.nn.sigmoid(v)


def _softplus(v):
    return jnp.maximum(v, 0.0) + jnp.log1p(jnp.exp(-jnp.abs(v)))


def _ada_kernel(c_ref, w_ref, b_ref, o_ref):
    o_ref[...] = jnp.dot(_silu(c_ref[...]), w_ref[...], preferred_element_type=F32) + b_ref[...]


def _ada(cc, w_ada, b_ada):
    rows, d = cc.shape
    n = w_ada.shape[1]
    tn = 1536
    return pl.pallas_call(
        _ada_kernel,
        grid=(n // tn,),
        in_specs=[pl.BlockSpec((rows, d), lambda j: (0, 0)),
                  pl.BlockSpec((d, tn), lambda j: (0, j)),
                  pl.BlockSpec((1, tn), lambda j: (0, j))],
        out_specs=pl.BlockSpec((rows, tn), lambda j: (0, j)),
        out_shape=jax.ShapeDtypeStruct((rows, n), F32),
        compiler_params=_params(("arbitrary",)),
        name="ada",
    )(cc, w_ada, b_ada.reshape(1, n))


def _rope_tables(seq):
    lane = np.arange(LANES)
    axis = (lane % NA_HEAD_DIM) // 32
    within = lane % 32
    half = 16
    inv_freq = ROPE_BASE ** (-(within % half).astype(np.float64) / half)
    pos = np.arange(seq)
    coord = np.where(axis[None, :] == 0, (pos // GRID_W)[:, None], (pos % GRID_W)[:, None])
    ang = coord.astype(np.float32) * inv_freq.astype(np.float32)[None, :]
    cos, sin = jnp.cos(jnp.asarray(ang, F32)), jnp.sin(jnp.asarray(ang, F32))
    first = jnp.asarray(within < half)[None, :]
    return cos, jnp.where(first, -sin, 0.0), jnp.where(first, 0.0, sin)


def _inproj_kernel(x_ref, sh_ref, sc_ref, g_ref, w_ref, wdt_ref, cos_ref, s1_ref, s2_ref,
                   o_ref, dt_ref, h_scr, *, rope_tiles):
    j = pl.program_id(2)

    @pl.when(j == 0)
    def _():
        x = x_ref[0]
        xn = x * lax.rsqrt(jnp.mean(x * x, axis=-1, keepdims=True) + NORM_EPS) * g_ref[...]
        h = (xn * (1.0 + sc_ref[0]) + sh_ref[0]).astype(BF16)
        h_scr[...] = h
        dt_ref[0] = lax.dot_general(wdt_ref[...], h, (((1,), (1,)), ((), ())), preferred_element_type=F32)

    acc = jnp.dot(h_scr[...], w_ref[...], preferred_element_type=F32)

    @pl.when(j >= rope_tiles)
    def _():
        o_ref[0] = acc.astype(o_ref.dtype)

    if rope_tiles:
        @pl.when(j < rope_tiles)
        def _():
            scale = jnp.where(j == 0, NA_HEAD_DIM ** -0.5, 1.0).astype(F32)
            cos, s1, s2 = cos_ref[...] * scale, s1_ref[...] * scale, s2_ref[...] * scale
            for c in range(acc.shape[1] // LANES):
                t = acc[:, c * LANES:(c + 1) * LANES]
                r = t * cos + pltpu.roll(t, LANES - 16, 1) * s1 + pltpu.roll(t, 16, 1) * s2
                o_ref[0, :, c * LANES:(c + 1) * LANES] = r.astype(o_ref.dtype)


def _inproj(x, mod3, mod_row0, g_pre, w, w_dt, rope, rope_tiles):
    bsz, seq, d = x.shape
    n = w.shape[1]
    ndt = w_dt.shape[0]
    tm, tn = 1024, 1024
    cos, s1, s2 = rope
    kern = functools.partial(_inproj_kernel, rope_tiles=rope_tiles)
    return pl.pallas_call(
        kern,
        grid=(bsz, seq // tm, n // tn),
        in_specs=[pl.BlockSpec((1, tm, d), lambda b, i, j: (b, i, 0)),
                  pl.BlockSpec((1, 1, d), lambda b, i, j: (b + mod_row0, 0, 0)),
                  pl.BlockSpec((1, 1, d), lambda b, i, j: (b + mod_row0, 0, 1)),
                  pl.BlockSpec((1, d), lambda b, i, j: (0, 0)),
                  pl.BlockSpec((d, tn), lambda b, i, j: (0, j)),
                  pl.BlockSpec((ndt, d), lambda b, i, j: (0, 0)),
                  pl.BlockSpec((tm, LANES), lambda b, i, j: (i, 0)),
                  pl.BlockSpec((tm, LANES), lambda b, i, j: (i, 0)),
                  pl.BlockSpec((tm, LANES), lambda b, i, j: (i, 0))],
        out_specs=[pl.BlockSpec((1, tm, tn), lambda b, i, j: (b, i, j)),
                   pl.BlockSpec((1, ndt, tm), lambda b, i, j: (b, 0, i))],
        out_shape=[jax.ShapeDtypeStruct((bsz, seq, n), BF16),
                   jax.ShapeDtypeStruct((bsz, ndt, seq), F32)],
        scratch_shapes=[pltpu.VMEM((tm, d), BF16)],
        compiler_params=_params(("parallel", "parallel", "arbitrary")),
        name="inproj",
    )(x, mod3, mod3, g_pre, w, w_dt, cos, s1, s2)


NA_QROWS = 4
NA_KROWS = 12
NA_QBLK = NA_QROWS * GRID_W
NA_KBLK = NA_KROWS * GRID_W
NA_MASKED = -1e30


def _na_key_start(g, rows):
    return min(max(NA_QROWS * g - NA_KH // 2, 0), rows - NA_KROWS)


def _na_bias(rpb, rows):
    heads = rpb.shape[0]
    col = np.arange(GRID_W)
    col_start = np.clip(col - NA_KW // 2, 0, GRID_W - NA_KW)
    col_ok = (col[None, :] >= col_start[:, None]) & (col[None, :] < col_start[:, None] + NA_KW)
    edge = GRID_W - NA_KW
    padded = jnp.pad(rpb, ((0, 0), (0, 0), (edge, edge)), mode="edge")
    toe = jnp.stack([padded[:, :, GRID_W - 1 - cq:2 * GRID_W - 1 - cq] for cq in range(GRID_W)], axis=2)
    nblk = rows // NA_QROWS
    out = []
    for g in (0, 1, nblk - 1):
        r = NA_QROWS * g + np.arange(NA_QROWS)
        kr = _na_key_start(g, rows) + np.arange(NA_KROWS)
        r0 = np.clip(r - NA_KH // 2, 0, rows - NA_KH)
        row_ok = (kr[None, :] >= r0[:, None]) & (kr[None, :] < r0[:, None] + NA_KH)
        row_idx = np.clip(kr[None, :] - r[:, None] + NA_KH - 1, 0, 2 * NA_KH - 2)
        b = jnp.stack([jnp.stack([toe[:, int(row_idx[dr, i])] for i in range(NA_KROWS)], axis=2)
                       for dr in range(NA_QROWS)], axis=1)
        ok = row_ok[:, None, :, None] & col_ok[None, :, None, :]
        out.append(jnp.where(ok[None], b, NA_MASKED).reshape(heads, NA_QBLK, NA_KBLK))
    return jnp.stack(out).astype(F32)


def _na_kernel(q_ref, k_ref, v_ref, kc_ref, vc_ref, bias_ref, o_ref, *, rows):
    nblk = rows // NA_QROWS
    first_head = lax.broadcasted_iota(jnp.int32, (1, LANES), 1) < NA_HEAD_DIM
    kc, vc = kc_ref[0], vc_ref[0]
    nt = (((1,), (1,)), ((), ()))
    for g in range(nblk):
        k0 = _na_key_start(g, rows) * GRID_W
        geom = 0 if g == 0 else (2 if g == nblk - 1 else 1)
        q = q_ref[0, g * NA_QBLK:(g + 1) * NA_QBLK, :]
        kw = k_ref[0, k0:k0 + NA_KBLK, :]
        vw = v_ref[0, k0:k0 + NA_KBLK, :]
        outs = []
        for hh in range(2):
            qm = jnp.where(first_head if hh == 0 else ~first_head, q, jnp.zeros_like(q))
            s = lax.dot_general(qm, kw, nt, preferred_element_type=F32) + bias_ref[geom, hh]
            sc = lax.dot_general(qm, kc, nt, preferred_element_type=F32)
            m = jnp.maximum(jnp.max(s, axis=-1, keepdims=True), jnp.max(sc, axis=-1, keepdims=True))
            p, pc = jnp.exp(s - m), jnp.exp(sc - m)
            denom = jnp.sum(p, axis=-1, keepdims=True) + jnp.sum(pc, axis=-1, keepdims=True)
            o = (jnp.dot(p.astype(BF16), vw, preferred_element_type=F32)
                 + jnp.dot(pc.astype(BF16), vc, preferred_element_type=F32))
            outs.append(o / denom)
        o_ref[0, g * NA_QBLK:(g + 1) * NA_QBLK, :] = jnp.where(first_head, outs[0], outs[1]).astype(o_ref.dtype)


def _na(slab, slab_ctx, bias):
    bsz, seq, _ = slab.shape
    tctx = slab_ctx.shape[1]
    pairs = NA_HEADS // 2
    blk = lambda col0: (lambda hp, b: (b, 0, col0 // LANES + hp))
    return pl.pallas_call(
        functools.partial(_na_kernel, rows=seq // GRID_W),
        grid=(pairs, bsz),
        in_specs=[pl.BlockSpec((1, seq, LANES), blk(COL_Q)),
                  pl.BlockSpec((1, seq, LANES), blk(COL_K)),
                  pl.BlockSpec((1, seq, LANES), blk(COL_V)),
                  pl.BlockSpec((1, tctx, LANES), blk(CTX_K)),
                  pl.BlockSpec((1, tctx, LANES), blk(CTX_V)),
                  pl.BlockSpec((3, 2, NA_QBLK, NA_KBLK), lambda hp, b: (0, hp, 0, 0))],
        out_specs=pl.BlockSpec((1, seq, LANES), lambda hp, b: (b, 0, hp)),
        out_shape=jax.ShapeDtypeStruct((bsz, seq, NA_HEADS * NA_HEAD_DIM), BF16),
        compiler_params=_params(("parallel", "parallel")),
        name="na",
    )(slab, slab, slab, slab_ctx, slab_ctx, bias)


SSD_GHEADS = SSD_HEADS // SSD_GROUPS
SSD_GWIDTH = SSD_GHEADS * SSD_HEAD_DIM
CONV_HALO = 8


SSD_ROWS = 2 * SSD_GHEADS
PACK_V, PACK_W, PACK_E = 0, 3 * SSD_ROWS, 5 * SSD_ROWS


def _ssd_selectors():
    k = np.arange(LANES)[:, None]
    row = k % SSD_ROWS

    def sel(first, terms, head_of_col):
        live = (k >= first) & (k < first + terms * SSD_ROWS)
        return jnp.asarray((live & (row == head_of_col[None, :])).astype(np.float32), BF16)

    col = np.arange(2 * SSD_GWIDTH)
    head_dir = np.where(col < SSD_GWIDTH, col // SSD_HEAD_DIM, SSD_GHEADS + (col - SSD_GWIDTH) // SSD_HEAD_DIM)
    sel_v = sel(PACK_V, 3, np.arange(SSD_ROWS * SSD_CHUNK) // SSD_CHUNK)
    return sel_v, sel(PACK_W, 2, head_dir), sel(PACK_E, 2, head_dir)


def _ssd_kernel(xs_ref, b_ref, c_ref, z_ref, dt_ref, xsc_ref, bc_ref, cc_ref, dtc_ref,
                cwx_ref, cwb_ref, cwc_ref, cbx_ref, cbb_ref, cbc_ref, dtb_ref, alog_ref, dskip_ref, nrm_ref,
                selv_ref, selw_ref, sele_ref,
                y_ref,
                padx, padb, padc, xs_s, bt_s, c_s, pack_s, vrow_s, dtrow_s, hprev_s, sb_s, decb_s, hf_s, hb_s,
                *, seq, tctx):
    Q = SSD_CHUNK
    GH = SSD_GHEADS
    GW = SSD_GWIDTH
    rows = lax.broadcasted_iota(jnp.int32, (Q, Q), 0)
    cols = lax.broadcasted_iota(jnp.int32, (Q, Q), 1)
    cum_rhs = jnp.concatenate([jnp.where(rows <= cols, 1.0, 0.0), jnp.ones((Q, Q), F32)], axis=1).astype(BF16)
    is_fwd = lax.broadcasted_iota(jnp.int32, (SSD_ROWS, 1), 0) < GH
    first_head = lax.broadcasted_iota(jnp.int32, (1, LANES), 1) < SSD_HEAD_DIM
    a_coef = -jnp.exp(alog_ref[...])
    dt_bias = dtb_ref[...]

    def bf_terms(v, n):
        out, rem = [], v
        for _ in range(n):
            t = rem.astype(BF16).astype(F32)
            out.append(t)
            rem = rem - t
        return out

    def prep(x_raw, b_raw, c_raw, dt_raw, n, store_prev):
        for pad, raw in ((padx, x_raw), (padb, b_raw), (padc, c_raw)):
            width = pad.shape[1]
            pad[0:CONV_HALO, :] = jnp.zeros((CONV_HALO, width), F32)
            pad[CONV_HALO + n:2 * CONV_HALO + n, :] = jnp.zeros((CONV_HALO, width), F32)

        def stage(c, carry):
            r0 = pl.multiple_of(c * Q, Q)
            for pad, raw in ((padx, x_raw), (padb, b_raw), (padc, c_raw)):
                pad[pl.ds(r0 + CONV_HALO, Q), :] = raw[0, pl.ds(r0, Q), :].astype(F32)
            return carry

        lax.fori_loop(0, n // Q, stage, 0)

        def conv(pad, w_ref, bias_ref, r0):
            first = CONV_HALO - SSD_CONV // 2
            tiles = []
            for lo in range(0, pad.shape[1], LANES):
                win = pad[pl.ds(r0, Q + 2 * CONV_HALO), lo:lo + LANES]
                acc = bias_ref[:, lo:lo + LANES] + win[first:first + Q, :] * w_ref[0:1, lo:lo + LANES]
                for k in range(1, SSD_CONV):
                    acc = acc + win[first + k:first + k + Q, :] * w_ref[k:k + 1, lo:lo + LANES]
                tiles.append(_silu(acc))
            return tiles[0] if len(tiles) == 1 else jnp.concatenate(tiles, axis=1)

        def body(c, carry):
            r0 = pl.multiple_of(c * Q, Q)
            dt = _softplus(dt_raw[0, :, pl.ds(r0, Q)] + dt_bias)
            a = dt * a_coef
            cs = jnp.dot(jnp.concatenate([t.astype(BF16) for t in bf_terms(a, 3)], axis=0), cum_rhs,
                         preferred_element_type=F32)
            cs = cs[0:SSD_ROWS] + cs[SSD_ROWS:2 * SSD_ROWS] + cs[2 * SSD_ROWS:3 * SSD_ROWS]
            acs, tot = cs[:, :Q], cs[:, Q:]
            ex = acs - a
            v = jnp.where(is_fwd, acs, ex)
            w = dt * jnp.exp(jnp.where(is_fwd, tot - acs, ex))
            e = jnp.exp(jnp.where(is_fwd, acs, tot - ex))
            vrow_s[c] = v
            dtrow_s[c] = dt
            packed = jnp.concatenate(bf_terms(v, 3) + bf_terms(w, 2) + bf_terms(e, 2)
                                     + [jnp.zeros((SSD_ROWS, Q), F32)], axis=0)
            pk = packed.T.astype(BF16)
            pack_s[pl.ds(r0, Q), :] = pk
            wexp = jnp.dot(pk, selw_ref[...], preferred_element_type=F32)
            edge = jnp.concatenate([pk[0:16, :], pk[Q - 16:Q, :]], axis=0)
            dec = jnp.dot(edge, sele_ref[...], preferred_element_type=F32)
            b_t = conv(padb, cwb_ref, cbb_ref, r0).T.astype(BF16)
            bt_s[c] = b_t
            c_s[pl.ds(r0, Q), :] = conv(padc, cwc_ref, cbc_ref, r0).astype(BF16)
            xs = conv(padx, cwx_ref, cbx_ref, r0)
            xs_s[pl.ds(r0, Q), :] = xs
            xdec = jnp.concatenate([xs * wexp[:, :GW], xs * wexp[:, GW:]], axis=1).astype(BF16)
            s = jnp.dot(b_t, xdec, preferred_element_type=F32)
            if store_prev:
                hprev_s[c] = hf_s[...].astype(BF16)
            hf_s[...] = dec[31:32, :GW] * hf_s[...] + s[:, :GW]
            sb_s[c] = s[:, GW:]
            decb_s[c] = jnp.broadcast_to(dec[0:1, GW:], (8, GW))
            return carry

        lax.fori_loop(0, n // Q, body, 0, unroll=2)

    def backward_step(c):
        hb_s[...] = decb_s[c][0:1, :] * hb_s[...] + sb_s[c]

    hf_s[...] = jnp.zeros(hf_s.shape, F32)
    hb_s[...] = jnp.zeros(hb_s.shape, F32)

    prep(xsc_ref, bc_ref, cc_ref, dtc_ref, tctx, False)
    nctx = tctx // Q

    def ctx_back(i, carry):
        backward_step(nctx - 1 - i)
        return carry

    lax.fori_loop(0, nctx, ctx_back, 0)

    prep(xs_ref, b_ref, c_ref, dt_ref, seq, True)
    nch = seq // Q
    dskip = dskip_ref[...]
    gain = nrm_ref[...]

    def ybody(i, carry):
        c = nch - 1 - i
        r0 = pl.multiple_of(c * Q, Q)
        pk = pack_s[pl.ds(r0, Q), :]
        vcol = jnp.dot(pk, selv_ref[...], preferred_element_type=F32)
        eexp = jnp.dot(pk, sele_ref[...], preferred_element_type=F32)
        v_t, dt_t = vrow_s[c], dtrow_s[c]
        xs = xs_s[pl.ds(r0, Q), :]
        xsb = xs.astype(BF16)
        cm = c_s[pl.ds(r0, Q), :]
        cb = jnp.dot(cm, bt_s[c], preferred_element_type=F32)
        y_off = (jnp.dot(cm, hprev_s[c], preferred_element_type=F32) * eexp[:, :GW]
                 + jnp.dot(cm, hb_s[...].astype(BF16), preferred_element_type=F32) * eexp[:, GW:])
        gate = _silu(z_ref[0, pl.ds(r0, Q), :].astype(F32))
        tiles = []
        for j in range(GH // 2):
            xpair = xsb[:, j * LANES:(j + 1) * LANES]
            ys = []
            for hh in range(2):
                hf, hb = 2 * j + hh, GH + 2 * j + hh
                arg_f = vcol[:, hf * Q:(hf + 1) * Q] - v_t[hf:hf + 1, :]
                arg_b = v_t[hb:hb + 1, :] - vcol[:, hb * Q:(hb + 1) * Q]
                e = jnp.exp(jnp.where(rows >= cols, arg_f, arg_b))
                dt_f, dt_b = dt_t[hf:hf + 1, :], dt_t[hb:hb + 1, :]
                fac = jnp.where(rows > cols, dt_f, jnp.where(rows < cols, dt_b, dt_f + dt_b))
                m = (cb * e * fac).astype(BF16)
                ys.append(jnp.dot(m, xpair, preferred_element_type=F32))
            tiles.append(jnp.where(first_head, ys[0], ys[1]))
        y = jnp.concatenate(tiles, axis=1) + y_off + dskip * xs
        u = y * gate
        u = u * lax.rsqrt(jnp.mean(u * u, axis=-1, keepdims=True) + NORM_EPS) * gain
        y_ref[0, pl.ds(r0, Q), :] = u.astype(y_ref.dtype)
        backward_step(c)
        return carry

    lax.fori_loop(0, nch, ybody, 0, unroll=2)


def _ssd(slab, dt, slab_ctx, dt_ctx, cw, cb, dtb, alog, dskip, nrm):
    bsz, seq, _ = slab.shape
    tctx = slab_ctx.shape[1]
    gw, ns = SSD_GWIDTH, SSD_STATE
    nch = seq // SSD_CHUNK
    cwx, cwb, cwc = cw
    cbx, cbb, cbc = cb

    def tok(width, col0, n):
        return pl.BlockSpec((1, n, width), lambda b, g: (b, 0, col0 // width + g))

    def par(rows, width):
        return pl.BlockSpec((rows, width), lambda b, g: (0, g))

    def head_rows(n):
        return pl.BlockSpec((1, SSD_ROWS, n), lambda b, g: (b, g, 0))

    def const(a):
        return pl.BlockSpec(a.shape, lambda b, g: (0, 0))

    sel_v, sel_w, sel_e = _ssd_selectors()
    pad = seq + 2 * CONV_HALO
    return pl.pallas_call(
        functools.partial(_ssd_kernel, seq=seq, tctx=tctx),
        grid=(bsz, SSD_GROUPS),
        in_specs=[tok(gw, COL_X, seq), tok(ns, COL_B, seq), tok(ns, COL_C, seq), tok(gw, COL_Z, seq),
                  head_rows(seq),
                  tok(gw, CTX_X, tctx), tok(ns, CTX_B, tctx), tok(ns, CTX_C, tctx), head_rows(tctx),
                  par(SSD_CONV, gw), par(SSD_CONV, ns), par(SSD_CONV, ns),
                  par(1, gw), par(1, ns), par(1, ns),
                  pl.BlockSpec((SSD_ROWS, LANES), lambda b, g: (g, 0)),
                  pl.BlockSpec((SSD_ROWS, LANES), lambda b, g: (g, 0)),
                  par(1, gw), par(1, gw), const(sel_v), const(sel_w), const(sel_e)],
        out_specs=pl.BlockSpec((1, seq, gw), lambda b, g: (b, 0, g)),
        out_shape=jax.ShapeDtypeStruct((bsz, seq, SSD_WIDTH), BF16),
        scratch_shapes=[pltpu.VMEM((pad, gw), F32), pltpu.VMEM((pad, ns), F32), pltpu.VMEM((pad, ns), F32),
                        pltpu.VMEM((seq, gw), F32), pltpu.VMEM((nch, ns, SSD_CHUNK), BF16),
                        pltpu.VMEM((seq, ns), BF16), pltpu.VMEM((seq, LANES), BF16),
                        pltpu.VMEM((nch, SSD_ROWS, SSD_CHUNK), F32), pltpu.VMEM((nch, SSD_ROWS, SSD_CHUNK), F32),
                        pltpu.VMEM((nch, ns, gw), BF16), pltpu.VMEM((nch, ns, gw), F32),
                        pltpu.VMEM((nch, 8, gw), F32),
                        pltpu.VMEM((ns, gw), F32), pltpu.VMEM((ns, gw), F32)],
        compiler_params=_params(("parallel", "parallel")),
        name="ssd",
    )(slab, slab, slab, slab, dt, slab_ctx, slab_ctx, slab_ctx, dt_ctx,
      cwx, cwb, cwc, cbx, cbb, cbc, dtb, alog, dskip, nrm, sel_v, sel_w, sel_e)


def _rms(v, gain):
    return v * lax.rsqrt(jnp.mean(v * v, axis=-1, keepdims=True) + NORM_EPS) * gain


def _merge_kernel(ona_ref, y_ref, gna_ref, gssd_ref, x_ref, ga1_ref, sh2_ref, sc2_ref,
                  wna_ref, wssd_ref, wout_ref, npost_ref, npre_ref, wrh_ref, wrl_ref,
                  x1_ref, h2_ref, aff_ref):
    a = jnp.dot(ona_ref[0], wna_ref[...], preferred_element_type=F32)
    s = jnp.dot(y_ref[0], wssd_ref[...], preferred_element_type=F32)
    u = jax.nn.sigmoid(gna_ref[0].astype(F32)) * a + jax.nn.sigmoid(gssd_ref[0].astype(F32)) * s
    mix = jnp.dot(u.astype(BF16), wout_ref[...], preferred_element_type=F32)
    x1 = x_ref[0] + ga1_ref[0] * _rms(mix, npost_ref[...])
    x1_ref[0] = x1
    h2 = _rms(x1, npre_ref[...]) * (1.0 + sc2_ref[0]) + sh2_ref[0]
    hi = h2.astype(BF16)
    h2_ref[0] = hi
    lo = (h2 - hi.astype(F32)).astype(BF16)
    nt = (((1,), (1,)), ((), ()))
    wrh, wrl = wrh_ref[...], wrl_ref[...]
    logits = (lax.dot_general(wrh, hi, nt, preferred_element_type=F32)
              + lax.dot_general(wrh, lo, nt, preferred_element_type=F32)
              + lax.dot_general(wrl, hi, nt, preferred_element_type=F32))
    e = jnp.exp(logits - jnp.max(logits, axis=0, keepdims=True))
    aff_ref[0] = e / jnp.sum(e, axis=0, keepdims=True)


def _merge(o_na, y_ssd, slab, x, mod3, w_na, w_ssd, w_o, n_post, n_pre, wr_hi, wr_lo):
    bsz, seq, d = x.shape
    tm = 512
    ne = wr_hi.shape[0]
    tok = lambda width, blk: pl.BlockSpec((1, tm, width), lambda b, i: (b, i, blk))
    modv = lambda k: pl.BlockSpec((1, 1, d), lambda b, i: (b, 0, k))
    full = lambda r, c_: pl.BlockSpec((r, c_), lambda b, i: (0, 0))
    return pl.pallas_call(
        _merge_kernel,
        grid=(bsz, seq // tm),
        in_specs=[tok(d, 0), tok(SSD_WIDTH, 0), tok(d, COL_G // d), tok(d, COL_G // d + 1), tok(d, 0),
                  modv(2), modv(3), modv(4),
                  full(d, d), full(SSD_WIDTH, d), full(d, d), full(1, d), full(1, d), full(ne, d), full(ne, d)],
        out_specs=[tok(d, 0), tok(d, 0), pl.BlockSpec((1, ne, tm), lambda b, i: (b, 0, i))],
        out_shape=[jax.ShapeDtypeStruct((bsz, seq, d), F32), jax.ShapeDtypeStruct((bsz, seq, d), BF16),
                   jax.ShapeDtypeStruct((bsz, ne, seq), F32)],
        compiler_params=_params(("parallel", "parallel")),
        name="merge",
    )(o_na, y_ssd, slab, slab, x, mod3, mod3, mod3, w_na, w_ssd, w_o, n_post, n_pre, wr_hi, wr_lo)


def _prefix_count(mask_bf16, strict_upper):
    r, s = mask_bf16.shape
    offset = jnp.zeros((r, 1), F32)
    parts = []
    for j in range(s // LANES):
        seg = mask_bf16[:, j * LANES:(j + 1) * LANES]
        parts.append(jnp.dot(seg, strict_upper, preferred_element_type=F32) + offset)
        offset = offset + jnp.sum(seg.astype(F32), axis=1, keepdims=True)
    return jnp.concatenate(parts, axis=1)


def _route_kernel(aff_ref, h2_ref, pos_ref, xg_ref, gate_ref, *, cap):
    aff = aff_ref[0]
    bits = pltpu.bitcast(aff, jnp.int32)
    ne, t = aff.shape
    thr = jnp.zeros((ne, 1), jnp.int32)
    for bit in range(30, -1, -1):
        cand = thr | (1 << bit)
        cnt = jnp.sum(jnp.where(bits >= cand, 1.0, 0.0), axis=1, keepdims=True)
        thr = jnp.where(cnt >= cap, cand, thr)
    above = bits > thr
    tied = bits == thr
    need = cap - jnp.sum(jnp.where(above, 1.0, 0.0), axis=1, keepdims=True)
    ri = lax.broadcasted_iota(jnp.int32, (LANES, LANES), 0)
    ci = lax.broadcasted_iota(jnp.int32, (LANES, LANES), 1)
    strict_upper = jnp.where(ri < ci, 1.0, 0.0).astype(BF16)
    tie_rank = _prefix_count(jnp.where(tied, 1.0, 0.0).astype(BF16), strict_upper)
    sel = above | (tied & (tie_rank < need))
    slot = _prefix_count(jnp.where(sel, 1.0, 0.0).astype(BF16), strict_upper)
    pos = jnp.where(sel, slot.astype(jnp.int32), -1)
    pos_ref[0] = pos
    h2 = h2_ref[0]
    slot_id = lax.broadcasted_iota(jnp.int32, (cap, t), 0)
    for e in range(ne):
        hit = slot_id == pos[e:e + 1, :]
        onehot = jnp.where(hit, 1.0, 0.0).astype(BF16)
        xg_ref[e] = jnp.dot(onehot, h2, preferred_element_type=F32).astype(xg_ref.dtype)
        gate_ref[e] = jnp.sum(jnp.where(hit, aff[e:e + 1, :], 0.0), axis=1, keepdims=True)


def _route(aff_t, h2):
    bsz, ne, t = aff_t.shape
    d = h2.shape[2]
    cap = EC_CAPACITY_FACTOR * t // ne
    return pl.pallas_call(
        functools.partial(_route_kernel, cap=cap),
        grid=(bsz,),
        in_specs=[pl.BlockSpec((1, ne, t), lambda b: (b, 0, 0)),
                  pl.BlockSpec((1, t, d), lambda b: (b, 0, 0))],
        out_specs=[pl.BlockSpec((1, ne, t), lambda b: (b, 0, 0)),
                   pl.BlockSpec((ne, cap, d), lambda b: (0, b, 0)),
                   pl.BlockSpec((ne, cap, 1), lambda b: (0, b, 0))],
        out_shape=[jax.ShapeDtypeStruct((bsz, ne, t), jnp.int32),
                   jax.ShapeDtypeStruct((ne, bsz * cap, d), BF16),
                   jax.ShapeDtypeStruct((ne, bsz * cap, 1), F32)],
        compiler_params=_params(("parallel",)),
        name="route",
    )(aff_t, h2)


EXPERT_FF_TILE = 512
EXPERT_ROW_TILE = 512


def _expert_kernel(xg_ref, gate_ref, wg_ref, wu_ref, wd_ref, yo_ref, acc_ref):
    f = pl.program_id(1)
    wg, wu, wd = wg_ref[0].astype(BF16), wu_ref[0].astype(BF16), wd_ref[0].astype(BF16)
    n = xg_ref.shape[1]
    rt = min(EXPERT_ROW_TILE, n)
    for r in range(n // rt):
        rows = pl.ds(r * rt, rt)
        xg = xg_ref[0, rows, :]
        hid = _silu(jnp.dot(xg, wg, preferred_element_type=F32)) * jnp.dot(xg, wu, preferred_element_type=F32)
        part = jnp.dot(hid.astype(BF16), wd, preferred_element_type=F32)

        @pl.when(f == 0)
        def _():
            acc_ref[rows, :] = part

        @pl.when(f > 0)
        def _():
            acc_ref[rows, :] += part

    @pl.when(f == pl.num_programs(1) - 1)
    def _():
        cap = yo_ref.shape[1]
        for b in range(yo_ref.shape[0]):
            rows = pl.ds(b * cap, cap)
            yo_ref[b] = (acc_ref[rows, :] * gate_ref[0, rows, :]).astype(yo_ref.dtype)


def _experts(xg, gate, w_g, w_u, w_d, bsz):
    ne, n, d = xg.shape
    cap = n // bsz
    ff = w_g.shape[2]
    tf = EXPERT_FF_TILE
    return pl.pallas_call(
        _expert_kernel,
        grid=(ne, ff // tf),
        in_specs=[pl.BlockSpec((1, n, d), lambda e, f: (e, 0, 0)),
                  pl.BlockSpec((1, n, 1), lambda e, f: (e, 0, 0)),
                  pl.BlockSpec((1, d, tf), lambda e, f: (e, 0, f)),
                  pl.BlockSpec((1, d, tf), lambda e, f: (e, 0, f)),
                  pl.BlockSpec((1, tf, d), lambda e, f: (e, f, 0))],
        out_specs=pl.BlockSpec((bsz, cap, d), lambda e, f: (0, e, 0)),
        out_shape=jax.ShapeDtypeStruct((bsz, ne * cap, d), BF16),
        scratch_shapes=[pltpu.VMEM((n, d), F32)],
        compiler_params=_params(("parallel", "arbitrary")),
        name="experts",
    )(xg, gate, w_g, w_u, w_d)


def _combine_kernel(pos_ref, yo_ref, x1_ref, ga2_ref, npost_ref, o_ref, *, cap):
    pos = pos_ref[0]
    slot_id = lax.broadcasted_iota(jnp.int32, (1, cap), 1)
    onehot = jnp.concatenate(
        [jnp.where(pos[:, e:e + 1] == slot_id, 1.0, 0.0).astype(BF16) for e in range(pos.shape[1])], axis=1)
    moe = jnp.dot(onehot, yo_ref[0], preferred_element_type=F32)
    o_ref[0] = x1_ref[0] + ga2_ref[0] * _rms(moe, npost_ref[...])


def _combine(pos_t, yo, x1, mod3, n_post):
    bsz, seq, d = x1.shape
    ne = pos_t.shape[2]
    cap = yo.shape[1] // ne
    tm = 512
    return pl.pallas_call(
        functools.partial(_combine_kernel, cap=cap),
        grid=(bsz, seq // tm),
        in_specs=[pl.BlockSpec((1, tm, ne), lambda b, i: (b, i, 0)),
                  pl.BlockSpec((1, ne * cap, d), lambda b, i: (b, 0, 0)),
                  pl.BlockSpec((1, tm, d), lambda b, i: (b, i, 0)),
                  pl.BlockSpec((1, 1, d), lambda b, i: (b, 0, 5)),
                  pl.BlockSpec((1, d), lambda b, i: (0, 0))],
        out_specs=pl.BlockSpec((1, tm, d), lambda b, i: (b, i, 0)),
        out_shape=jax.ShapeDtypeStruct((bsz, seq, d), F32),
        compiler_params=_params(("parallel", "parallel")),
        name="combine",
    )(pos_t, yo, x1, mod3, n_post)


def kernel(x, c, ctx, c_ctx, w_ada, b_ada, norm_pre_mix, norm_post_mix, norm_pre_ffn, norm_post_ffn,
           w_in, na_rpb, ssd_conv_w, ssd_conv_b, ssd_a_log_fwd, ssd_a_log_bwd, ssd_dt_bias_fwd,
           ssd_dt_bias_bwd, ssd_d_skip, ssd_norm, w_branch_na, w_branch_ssd, w_out, w_router,
           w_exp_gate, w_exp_up, w_exp_down):
    mod, slab, o_na, y_ssd = _front(x, c, ctx, c_ctx, w_ada[0], b_ada[0], norm_pre_mix[0], w_in[0], na_rpb[0],
                                    ssd_conv_w[0], ssd_conv_b[0], ssd_a_log_fwd[0], ssd_a_log_bwd[0],
                                    ssd_dt_bias_fwd[0], ssd_dt_bias_bwd[0], ssd_d_skip[0], ssd_norm[0])
    return _back(x, mod, slab, o_na, y_ssd, norm_post_mix[0], norm_pre_ffn[0], norm_post_ffn[0],
                 w_branch_na[0], w_branch_ssd[0], w_out[0], w_router[0], w_exp_gate[0], w_exp_up[0], w_exp_down[0])


def _back(x, mod, slab, o_na, y_ssd, norm_post_mix, norm_pre_ffn, norm_post_ffn,
          w_branch_na, w_branch_ssd, w_out, w_router, w_exp_gate, w_exp_up, w_exp_down):
    bsz, seq, d = x.shape
    mod3 = mod.reshape(16, 1, N_MOD * d)
    wr_t = w_router.T
    wr_hi = wr_t.astype(BF16)
    wr_lo = (wr_t - wr_hi.astype(F32)).astype(BF16)
    x1, h2, aff_t = _merge(o_na, y_ssd, slab, x, mod3, w_branch_na.astype(BF16), w_branch_ssd.astype(BF16),
                           w_out.astype(BF16), norm_post_mix.reshape(1, d), norm_pre_ffn.reshape(1, d),
                           wr_hi, wr_lo)
    pos, xg, gate = _route(aff_t, h2)
    yo = _experts(xg, gate, w_exp_gate, w_exp_up, w_exp_down, bsz)
    return _combine(jnp.swapaxes(pos, 1, 2), yo, x1, mod3, norm_post_ffn.reshape(1, d))


def _group_rows(fwd, bwd):
    rows = []
    for g in range(SSD_GROUPS):
        sl = slice(g * SSD_GHEADS, (g + 1) * SSD_GHEADS)
        rows += [fwd[sl], bwd[sl]]
    return jnp.concatenate(rows, axis=0)


def _front(x, c, ctx, c_ctx, w_ada, b_ada, norm_pre_mix, w_in, na_rpb, ssd_conv_w, ssd_conv_b,
           ssd_a_log_fwd, ssd_a_log_bwd, ssd_dt_bias_fwd, ssd_dt_bias_bwd, ssd_d_skip, ssd_norm):
    bsz, seq, d = x.shape
    tctx = ctx.shape[1]
    assert bsz <= 8
    cc =jnp.zeros((16, d), F32).at[:bsz].set(c).at[8].set(c_ctx)
    mod = _ada(cc, w_ada, b_ada)
    mod3 = mod.reshape(16, 1, N_MOD * d)

    dt0 = COL_G
    w_lat = jnp.concatenate([w_in[:, :dt0], w_in[:, dt0 + 2 * SSD_HEADS:]], axis=1).astype(BF16)
    w_ctx = jnp.concatenate([w_in[:, COL_K:COL_Z], w_in[:, COL_X:dt0]], axis=1).astype(BF16)
    w_dt = _group_rows(w_in[:, dt0:dt0 + SSD_HEADS].T, w_in[:, dt0 + SSD_HEADS:dt0 + 2 * SSD_HEADS].T).astype(BF16)
    rope = _rope_tables(seq)
    g_pre = norm_pre_mix.reshape(1, d)

    slab, dt = _inproj(x, mod3, 0, g_pre, w_lat, w_dt, rope, 2)
    nctx = bsz * tctx
    ctx_rows = -(-nctx // 1024) * 1024
    ctx_flat = ctx.reshape(1, nctx, d)
    if ctx_rows != nctx:
        ctx_flat = jnp.pad(ctx_flat, ((0, 0), (0, ctx_rows - nctx), (0, 0)))
    slab_c, dt_c = _inproj(ctx_flat, mod3, 8, g_pre, w_ctx, w_dt, rope, 0)
    slab_c = slab_c[0, :nctx].reshape(bsz, tctx, CTX_COLS)
    dt_c = jnp.swapaxes(dt_c[0, :, :nctx].reshape(-1, bsz, tctx), 0, 1)

    o_na = _na(slab, slab_c, _na_bias(na_rpb, seq // GRID_W))

    nx, nb = SSD_WIDTH, SSD_GROUPS * SSD_STATE
    cw = (ssd_conv_w[:, :nx], ssd_conv_w[:, nx:nx + nb], ssd_conv_w[:, nx + nb:])
    cb = tuple(v.reshape(1, -1) for v in (ssd_conv_b[:nx], ssd_conv_b[nx:nx + nb], ssd_conv_b[nx + nb:]))
    lanes = lambda v: jnp.broadcast_to(v[:, None], (v.shape[0], LANES))
    dtb = _group_rows(lanes(ssd_dt_bias_fwd), lanes(ssd_dt_bias_bwd))
    alog = _group_rows(lanes(ssd_a_log_fwd), lanes(ssd_a_log_bwd))
    dskip = jnp.repeat(ssd_d_skip, SSD_HEAD_DIM).reshape(1, -1)
    y_ssd = _ssd(slab, dt, slab_c, dt_c, cw, cb, dtb, alog, dskip, ssd_norm.reshape(1, -1))
    return mod, slab, o_na, y_ssd
```

```python
import functools
import math

import jax
import jax.numpy as jnp
import numpy as np
from jax import lax
from jax.experimental import pallas as pl
from jax.experimental.pallas import tpu as pltpu

F32 = jnp.float32
BF16 = jnp.bfloat16

D_MODEL = 1024
GRID_W = 64
NORM_EPS = 1e-6
N_MOD = 6
NA_HEADS = 16
NA_HEAD_DIM = 64
NA_KH = 8
NA_KW = 16
ROPE_BASE = 10000.0
SSD_WIDTH = 2048
SSD_HEAD_DIM = 64
SSD_HEADS = 32
SSD_GROUPS = 4
SSD_STATE = 128
SSD_CONV = 5
SSD_CHUNK = 128
N_EXPERTS = 16
EXPERT_FF = 2048
EC_CAPACITY_FACTOR = 2

VMEM_LIMIT_BYTES = 56 * 1024 * 1024
LANES = 128

COL_Q, COL_K, COL_V, COL_Z, COL_X, COL_B, COL_C, COL_G = 0, 1024, 2048, 3072, 5120, 7168, 7680, 8192
LAT_COLS = 10240
CTX_K, CTX_V, CTX_X, CTX_B, CTX_C = 0, 1024, 2048, 4096, 4608
CTX_COLS = 5120


def _params(semantics):
    return pltpu.CompilerParams(dimension_semantics=semantics, vmem_limit_bytes=VMEM_LIMIT_BYTES)


def _silu(v):
    return v * jax.nn.sigmoid(v)


def _softplus(v):
    return jnp.maximum(v, 0.0) + jnp.log1p(jnp.exp(-jnp.abs(v)))


def _ada_kernel(c_ref, w_ref, b_ref, o_ref):
    o_ref[...] = jnp.dot(_silu(c_ref[...]), w_ref[...], preferred_element_type=F32) + b_ref[...]


def _ada(cc, w_ada, b_ada):
    rows, d = cc.shape
    n = w_ada.shape[1]
    tn = 1536
    return pl.pallas_call(
        _ada_kernel,
        grid=(n // tn,),
        in_specs=[pl.BlockSpec((rows, d), lambda j: (0, 0)),
                  pl.BlockSpec((d, tn), lambda j: (0, j)),
                  pl.BlockSpec((1, tn), lambda j: (0, j))],
        out_specs=pl.BlockSpec((rows, tn), lambda j: (0, j)),
        out_shape=jax.ShapeDtypeStruct((rows, n), F32),
        compiler_params=_params(("arbitrary",)),
        name="ada",
    )(cc, w_ada, b_ada.reshape(1, n))


def _rope_tables(seq):
    lane = np.arange(LANES)
    axis = (lane % NA_HEAD_DIM) // 32
    within = lane % 32
    half = 16
    inv_freq = ROPE_BASE ** (-(within % half).astype(np.float64) / half)
    pos = np.arange(seq)
    coord = np.where(axis[None, :] == 0, (pos // GRID_W)[:, None], (pos % GRID_W)[:, None])
    ang = coord.astype(np.float32) * inv_freq.astype(np.float32)[None, :]
    cos, sin = jnp.cos(jnp.asarray(ang, F32)), jnp.sin(jnp.asarray(ang, F32))
    first = jnp.asarray(within < half)[None, :]
    return cos, jnp.where(first, -sin, 0.0), jnp.where(first, 0.0, sin)


def _inproj_kernel(x_ref, sh_ref, sc_ref, g_ref, w_ref, wdt_ref, cos_ref, s1_ref, s2_ref,
                   o_ref, dt_ref, h_scr, *, rope_tiles):
    j = pl.program_id(2)

    @pl.when(j == 0)
    def _():
        x = x_ref[0]
        xn = x * lax.rsqrt(jnp.mean(x * x, axis=-1, keepdims=True) + NORM_EPS) * g_ref[...]
        h = (xn * (1.0 + sc_ref[0]) + sh_ref[0]).astype(BF16)
        h_scr[...] = h
        dt_ref[0] = lax.dot_general(wdt_ref[...], h, (((1,), (1,)), ((), ())), preferred_element_type=F32)

    h = h_scr[...]
    chunks = range(0, o_ref.shape[2], INPROJ_CHUNK)

    @pl.when(j >= rope_tiles)
    def _():
        for c in chunks:
            acc = jnp.dot(h, w_ref[:, c:c + INPROJ_CHUNK], preferred_element_type=F32)
            o_ref[0, :, c:c + INPROJ_CHUNK] = acc.astype(o_ref.dtype)

    if rope_tiles:
        @pl.when(j < rope_tiles)
        def _():
            scale = jnp.where(j == 0, NA_HEAD_DIM ** -0.5, 1.0).astype(F32)
            cos, s1, s2 = cos_ref[...] * scale, s1_ref[...] * scale, s2_ref[...] * scale
            for c in chunks:
                acc = jnp.dot(h, w_ref[:, c:c + INPROJ_CHUNK], preferred_element_type=F32)
                for l in range(0, INPROJ_CHUNK, LANES):
                    t = acc[:, l:l + LANES]
                    r = t * cos + pltpu.roll(t, LANES - 16, 1) * s1 + pltpu.roll(t, 16, 1) * s2
                    o_ref[0, :, c + l:c + l + LANES] = r.astype(o_ref.dtype)


INPROJ_CHUNK = 256


def _inproj(x, mod3, mod_row0, g_pre, w, w_dt, rope, rope_tiles, n_tiles, w_tile):
    bsz, seq, d = x.shape
    ndt = w_dt.shape[0]
    tm, tn = 1024, 1024
    n = n_tiles * tn
    cos, s1, s2 = rope
    kern = functools.partial(_inproj_kernel, rope_tiles=rope_tiles)
    return pl.pallas_call(
        kern,
        grid=(bsz, seq // tm, n_tiles),
        in_specs=[pl.BlockSpec((1, tm, d), lambda b, i, j: (b, i, 0)),
                  pl.BlockSpec((1, 1, d), lambda b, i, j: (b + mod_row0, 0, 0)),
                  pl.BlockSpec((1, 1, d), lambda b, i, j: (b + mod_row0, 0, 1)),
                  pl.BlockSpec((1, d), lambda b, i, j: (0, 0)),
                  pl.BlockSpec((d, tn), lambda b, i, j: (0, w_tile(j))),
                  pl.BlockSpec((ndt, d), lambda b, i, j: (0, 0)),
                  pl.BlockSpec((tm, LANES), lambda b, i, j: (i, 0)),
                  pl.BlockSpec((tm, LANES), lambda b, i, j: (i, 0)),
                  pl.BlockSpec((tm, LANES), lambda b, i, j: (i, 0))],
        out_specs=[pl.BlockSpec((1, tm, tn), lambda b, i, j: (b, i, j)),
                   pl.BlockSpec((1, ndt, tm), lambda b, i, j: (b, 0, i))],
        out_shape=[jax.ShapeDtypeStruct((bsz, seq, n), BF16),
                   jax.ShapeDtypeStruct((bsz, ndt, seq), F32)],
        scratch_shapes=[pltpu.VMEM((tm, d), BF16)],
        compiler_params=_params(("parallel", "parallel", "arbitrary")),
        name="inproj",
    )(x, mod3, mod3, g_pre, w, w_dt, cos, s1, s2)


NA_QROWS = 4
NA_KROWS = 12
NA_QBLK = NA_QROWS * GRID_W
NA_KBLK = NA_KROWS * GRID_W
NA_MASKED = -1e30


def _na_key_start(g, rows):
    return min(max(NA_QROWS * g - NA_KH // 2, 0), rows - NA_KROWS)


def _na_bias(rpb, rows):
    heads = rpb.shape[0]
    col = np.arange(GRID_W)
    col_start = np.clip(col - NA_KW // 2, 0, GRID_W - NA_KW)
    col_ok = (col[None, :] >= col_start[:, None]) & (col[None, :] < col_start[:, None] + NA_KW)
    edge = GRID_W - NA_KW
    padded = jnp.pad(rpb, ((0, 0), (0, 0), (edge, edge)), mode="edge")
    n = 2 * GRID_W - 1
    tiled = jnp.tile(padded, (1, 1, GRID_W + 1))[..., :GRID_W * (n + 1)].reshape(heads, -1, GRID_W, n + 1)
    toe = tiled[:, :, ::-1, :GRID_W]
    toe = jnp.pad(toe, ((0, 0), (NA_QROWS, NA_QROWS), (0, 0), (0, 0)), mode="edge")
    nblk = rows // NA_QROWS
    out = []
    for g in (0, 1, nblk - 1):
        r = NA_QROWS * g + np.arange(NA_QROWS)
        kr = _na_key_start(g, rows) + np.arange(NA_KROWS)
        r0 = np.clip(r - NA_KH // 2, 0, rows - NA_KH)
        row_ok = (kr[None, :] >= r0[:, None]) & (kr[None, :] < r0[:, None] + NA_KH)
        first = int(kr[0] - r[0]) + NA_KH - 1 + NA_QROWS
        b = jnp.stack([toe[:, first - dr:first - dr + NA_KROWS] for dr in range(NA_QROWS)], axis=1)
        b = jnp.transpose(b, (0, 1, 3, 2, 4))
        ok = row_ok[:, None, :, None] & col_ok[None, :, None, :]
        out.append(jnp.where(ok[None], b, NA_MASKED).reshape(heads, NA_QBLK, NA_KBLK))
    return jnp.stack(out).astype(F32)


def _na_kernel(q_ref, k_ref, v_ref, kc_ref, vc_ref, bias_ref, o_ref, *, rows):
    nblk = rows // NA_QROWS
    first_head = lax.broadcasted_iota(jnp.int32, (1, LANES), 1) < NA_HEAD_DIM
    kc, vc = kc_ref[0], vc_ref[0]
    nt = (((1,), (1,)), ((), ()))
    for g in range(nblk):
        k0 = _na_key_start(g, rows) * GRID_W
        geom = 0 if g == 0 else (2 if g == nblk - 1 else 1)
        q = q_ref[0, g * NA_QBLK:(g + 1) * NA_QBLK, :]
        kw = k_ref[0, k0:k0 + NA_KBLK, :]
        vw = v_ref[0, k0:k0 + NA_KBLK, :]
        outs = []
        for hh in range(2):
            qm = jnp.where(first_head if hh == 0 else ~first_head, q, jnp.zeros_like(q))
            s = lax.dot_general(qm, kw, nt, preferred_element_type=F32) + bias_ref[geom, hh]
            sc = lax.dot_general(qm, kc, nt, preferred_element_type=F32)
            m = jnp.maximum(jnp.max(s, axis=-1, keepdims=True), jnp.max(sc, axis=-1, keepdims=True))
            p, pc = jnp.exp(s - m), jnp.exp(sc - m)
            denom = jnp.sum(p, axis=-1, keepdims=True) + jnp.sum(pc, axis=-1, keepdims=True)
            o = (jnp.dot(p.astype(BF16), vw, preferred_element_type=F32)
                 + jnp.dot(pc.astype(BF16), vc, preferred_element_type=F32))
            outs.append(o / denom)
        o_ref[0, g * NA_QBLK:(g + 1) * NA_QBLK, :] = jnp.where(first_head, outs[0], outs[1]).astype(o_ref.dtype)


def _na(slab, slab_ctx, bias):
    bsz, seq, _ = slab.shape
    tctx = slab_ctx.shape[1]
    pairs = NA_HEADS // 2
    blk = lambda col0: (lambda hp, b: (b, 0, col0 // LANES + hp))
    return pl.pallas_call(
        functools.partial(_na_kernel, rows=seq // GRID_W),
        grid=(pairs, bsz),
        in_specs=[pl.BlockSpec((1, seq, LANES), blk(COL_Q)),
                  pl.BlockSpec((1, seq, LANES), blk(COL_K)),
                  pl.BlockSpec((1, seq, LANES), blk(COL_V)),
                  pl.BlockSpec((1, tctx, LANES), blk(CTX_K)),
                  pl.BlockSpec((1, tctx, LANES), blk(CTX_V)),
                  pl.BlockSpec((3, 2, NA_QBLK, NA_KBLK), lambda hp, b: (0, hp, 0, 0))],
        out_specs=pl.BlockSpec((1, seq, LANES), lambda hp, b: (b, 0, hp)),
        out_shape=jax.ShapeDtypeStruct((bsz, seq, NA_HEADS * NA_HEAD_DIM), BF16),
        compiler_params=_params(("parallel", "parallel")),
        name="na",
    )(slab, slab, slab, slab_ctx, slab_ctx, bias)


SSD_GHEADS = SSD_HEADS // SSD_GROUPS
SSD_GWIDTH = SSD_GHEADS * SSD_HEAD_DIM
CONV_HALO = 8


SSD_ROWS = 2 * SSD_GHEADS
PACK_V, PACK_W, PACK_E = 0, 3 * SSD_ROWS, 5 * SSD_ROWS


def _ssd_selectors():
    k = np.arange(LANES)[:, None]
    row = k % SSD_ROWS

    def sel(first, terms, head_of_col):
        live = (k >= first) & (k < first + terms * SSD_ROWS)
        return jnp.asarray((live & (row == head_of_col[None, :])).astype(np.float32), BF16)

    col = np.arange(2 * SSD_GWIDTH)
    head_dir = np.where(col < SSD_GWIDTH, col // SSD_HEAD_DIM, SSD_GHEADS + (col - SSD_GWIDTH) // SSD_HEAD_DIM)
    sel_v = sel(PACK_V, 3, np.arange(SSD_ROWS * SSD_CHUNK) // SSD_CHUNK)
    return sel_v, sel(PACK_W, 2, head_dir), sel(PACK_E, 2, head_dir)


def _ssd_kernel(xs_ref, b_ref, c_ref, z_ref, dt_ref, xsc_ref, bc_ref, cc_ref, dtc_ref,
                cwx_ref, cwb_ref, cwc_ref, cbx_ref, cbb_ref, cbc_ref, dtb_ref, alog_ref, dskip_ref, nrm_ref,
                selv_ref, selw_ref, sele_ref,
                y_ref,
                padx, padb, padc, xs_s, bt_s, c_s, pack_s, vrow_s, dtrow_s, hprev_s, sb_s, decb_s, hf_s, hb_s,
                *, seq, tctx):
    Q = SSD_CHUNK
    GH = SSD_GHEADS
    GW = SSD_GWIDTH
    rows = lax.broadcasted_iota(jnp.int32, (Q, Q), 0)
    cols = lax.broadcasted_iota(jnp.int32, (Q, Q), 1)
    cum_rhs = jnp.concatenate([jnp.where(rows <= cols, 1.0, 0.0), jnp.ones((Q, Q), F32)], axis=1).astype(BF16)
    is_fwd = lax.broadcasted_iota(jnp.int32, (SSD_ROWS, 1), 0) < GH
    first_head = lax.broadcasted_iota(jnp.int32, (1, LANES), 1) < SSD_HEAD_DIM
    a_coef = -jnp.exp(alog_ref[...])
    dt_bias = dtb_ref[...]

    def bf_terms(v, n):
        out, rem = [], v
        for _ in range(n):
            t = rem.astype(BF16).astype(F32)
            out.append(t)
            rem = rem - t
        return out

    def prep(x_raw, b_raw, c_raw, dt_raw, n, store_prev):
        for pad, raw in ((padx, x_raw), (padb, b_raw), (padc, c_raw)):
            width = pad.shape[1]
            pad[0:CONV_HALO, :] = jnp.zeros((CONV_HALO, width), F32)
            pad[CONV_HALO + n:2 * CONV_HALO + n, :] = jnp.zeros((CONV_HALO, width), F32)

        def stage(c, carry):
            r0 = pl.multiple_of(c * Q, Q)
            for pad, raw in ((padx, x_raw), (padb, b_raw), (padc, c_raw)):
                pad[pl.ds(r0 + CONV_HALO, Q), :] = raw[0, pl.ds(r0, Q), :].astype(F32)
            return carry

        lax.fori_loop(0, n // Q, stage, 0)

        def conv(pad, w_ref, bias_ref, r0):
            first = CONV_HALO - SSD_CONV // 2
            tiles = []
            for lo in range(0, pad.shape[1], LANES):
                win = pad[pl.ds(r0, Q + 2 * CONV_HALO), lo:lo + LANES]
                acc = bias_ref[:, lo:lo + LANES] + win[first:first + Q, :] * w_ref[0:1, lo:lo + LANES]
                for k in range(1, SSD_CONV):
                    acc = acc + win[first + k:first + k + Q, :] * w_ref[k:k + 1, lo:lo + LANES]
                tiles.append(_silu(acc))
            return tiles[0] if len(tiles) == 1 else jnp.concatenate(tiles, axis=1)

        def body(c, carry):
            r0 = pl.multiple_of(c * Q, Q)
            dt = _softplus(dt_raw[0, :, pl.ds(r0, Q)] + dt_bias)
            a = dt * a_coef
            cs = jnp.dot(jnp.concatenate([t.astype(BF16) for t in bf_terms(a, 3)], axis=0), cum_rhs,
                         preferred_element_type=F32)
            cs = cs[0:SSD_ROWS] + cs[SSD_ROWS:2 * SSD_ROWS] + cs[2 * SSD_ROWS:3 * SSD_ROWS]
            acs, tot = cs[:, :Q], cs[:, Q:]
            ex = acs - a
            v = jnp.where(is_fwd, acs, ex)
            w = dt * jnp.exp(jnp.where(is_fwd, tot - acs, ex))
            e = jnp.exp(jnp.where(is_fwd, acs, tot - ex))
            vrow_s[c] = v
            dtrow_s[c] = dt
            packed = jnp.concatenate(bf_terms(v, 3) + bf_terms(w, 2) + bf_terms(e, 2)
                                     + [jnp.zeros((SSD_ROWS, Q), F32)], axis=0)
            pk = packed.T.astype(BF16)
            pack_s[pl.ds(r0, Q), :] = pk
            wexp = jnp.dot(pk, selw_ref[...], preferred_element_type=F32)
            edge = jnp.concatenate([pk[0:16, :], pk[Q - 16:Q, :]], axis=0)
            dec = jnp.dot(edge, sele_ref[...], preferred_element_type=F32)
            b_t = conv(padb, cwb_ref, cbb_ref, r0).T.astype(BF16)
            bt_s[c] = b_t
            c_s[pl.ds(r0, Q), :] = conv(padc, cwc_ref, cbc_ref, r0).astype(BF16)
            xs = conv(padx, cwx_ref, cbx_ref, r0)
            xs_s[pl.ds(r0, Q), :] = xs
            xdec = jnp.concatenate([xs * wexp[:, :GW], xs * wexp[:, GW:]], axis=1).astype(BF16)
            s = jnp.dot(b_t, xdec, preferred_element_type=F32)
            if store_prev:
                hprev_s[c] = hf_s[...].astype(BF16)
            hf_s[...] = dec[31:32, :GW] * hf_s[...] + s[:, :GW]
            sb_s[c] = s[:, GW:]
            decb_s[c] = jnp.broadcast_to(dec[0:1, GW:], (8, GW))
            return carry

        lax.fori_loop(0, n // Q, body, 0, unroll=2)

    def backward_step(c):
        hb_s[...] = decb_s[c][0:1, :] * hb_s[...] + sb_s[c]

    hf_s[...] = jnp.zeros(hf_s.shape, F32)
    hb_s[...] = jnp.zeros(hb_s.shape, F32)

    prep(xsc_ref, bc_ref, cc_ref, dtc_ref, tctx, False)
    nctx = tctx // Q

    def ctx_back(i, carry):
        backward_step(nctx - 1 - i)
        return carry

    lax.fori_loop(0, nctx, ctx_back, 0)

    prep(xs_ref, b_ref, c_ref, dt_ref, seq, True)
    nch = seq // Q
    dskip = dskip_ref[...]
    gain = nrm_ref[...]

    def ybody(i, carry):
        c = nch - 1 - i
        r0 = pl.multiple_of(c * Q, Q)
        pk = pack_s[pl.ds(r0, Q), :]
        vcol = jnp.dot(pk, selv_ref[...], preferred_element_type=F32)
        eexp = jnp.dot(pk, sele_ref[...], preferred_element_type=F32)
        v_t, dt_t = vrow_s[c], dtrow_s[c]
        xs = xs_s[pl.ds(r0, Q), :]
        xsb = xs.astype(BF16)
        cm = c_s[pl.ds(r0, Q), :]
        cb = jnp.dot(cm, bt_s[c], preferred_element_type=F32)
        y_off = (jnp.dot(cm, hprev_s[c], preferred_element_type=F32) * eexp[:, :GW]
                 + jnp.dot(cm, hb_s[...].astype(BF16), preferred_element_type=F32) * eexp[:, GW:])
        gate = _silu(z_ref[0, pl.ds(r0, Q), :].astype(F32))
        tiles = []
        for j in range(GH // 2):
            xpair = xsb[:, j * LANES:(j + 1) * LANES]
            ys = []
            for hh in range(2):
                hf, hb = 2 * j + hh, GH + 2 * j + hh
                arg_f = vcol[:, hf * Q:(hf + 1) * Q] - v_t[hf:hf + 1, :]
                arg_b = v_t[hb:hb + 1, :] - vcol[:, hb * Q:(hb + 1) * Q]
                e = jnp.exp(jnp.where(rows >= cols, arg_f, arg_b))
                dt_f, dt_b = dt_t[hf:hf + 1, :], dt_t[hb:hb + 1, :]
                fac = jnp.where(rows > cols, dt_f, jnp.where(rows < cols, dt_b, dt_f + dt_b))
                m = (cb * e * fac).astype(BF16)
                ys.append(jnp.dot(m, xpair, preferred_element_type=F32))
            tiles.append(jnp.where(first_head, ys[0], ys[1]))
        y = jnp.concatenate(tiles, axis=1) + y_off + dskip * xs
        u = y * gate
        u = u * lax.rsqrt(jnp.mean(u * u, axis=-1, keepdims=True) + NORM_EPS) * gain
        y_ref[0, pl.ds(r0, Q), :] = u.astype(y_ref.dtype)
        backward_step(c)
        return carry

    lax.fori_loop(0, nch, ybody, 0, unroll=2)


def _ssd(slab, dt, slab_ctx, dt_ctx, cw, cb, dtb, alog, dskip, nrm):
    bsz, seq, _ = slab.shape
    tctx = slab_ctx.shape[1]
    gw, ns = SSD_GWIDTH, SSD_STATE
    nch = seq // SSD_CHUNK
    cwx, cwb, cwc = cw
    cbx, cbb, cbc = cb

    def tok(width, col0, n):
        return pl.BlockSpec((1, n, width), lambda b, g: (b, 0, col0 // width + g))

    def par(rows, width):
        return pl.BlockSpec((rows, width), lambda b, g: (0, g))

    def head_rows(n):
        return pl.BlockSpec((1, SSD_ROWS, n), lambda b, g: (b, g, 0))

    def const(a):
        return pl.BlockSpec(a.shape, lambda b, g: (0, 0))

    sel_v, sel_w, sel_e = _ssd_selectors()
    pad = seq + 2 * CONV_HALO
    return pl.pallas_call(
        functools.partial(_ssd_kernel, seq=seq, tctx=tctx),
        grid=(bsz, SSD_GROUPS),
        in_specs=[tok(gw, COL_X, seq), tok(ns, COL_B, seq), tok(ns, COL_C, seq), tok(gw, COL_Z, seq),
                  head_rows(seq),
                  tok(gw, CTX_X, tctx), tok(ns, CTX_B, tctx), tok(ns, CTX_C, tctx), head_rows(tctx),
                  par(SSD_CONV, gw), par(SSD_CONV, ns), par(SSD_CONV, ns),
                  par(1, gw), par(1, ns), par(1, ns),
                  pl.BlockSpec((SSD_ROWS, LANES), lambda b, g: (g, 0)),
                  pl.BlockSpec((SSD_ROWS, LANES), lambda b, g: (g, 0)),
                  par(1, gw), par(1, gw), const(sel_v), const(sel_w), const(sel_e)],
        out_specs=pl.BlockSpec((1, seq, gw), lambda b, g: (b, 0, g)),
        out_shape=jax.ShapeDtypeStruct((bsz, seq, SSD_WIDTH), BF16),
        scratch_shapes=[pltpu.VMEM((pad, gw), F32), pltpu.VMEM((pad, ns), F32), pltpu.VMEM((pad, ns), F32),
                        pltpu.VMEM((seq, gw), F32), pltpu.VMEM((nch, ns, SSD_CHUNK), BF16),
                        pltpu.VMEM((seq, ns), BF16), pltpu.VMEM((seq, LANES), BF16),
                        pltpu.VMEM((nch, SSD_ROWS, SSD_CHUNK), F32), pltpu.VMEM((nch, SSD_ROWS, SSD_CHUNK), F32),
                        pltpu.VMEM((nch, ns, gw), BF16), pltpu.VMEM((nch, ns, gw), F32),
                        pltpu.VMEM((nch, 8, gw), F32),
                        pltpu.VMEM((ns, gw), F32), pltpu.VMEM((ns, gw), F32)],
        compiler_params=_params(("parallel", "parallel")),
        name="ssd",
    )(slab, slab, slab, slab, dt, slab_ctx, slab_ctx, slab_ctx, dt_ctx,
      cwx, cwb, cwc, cbx, cbb, cbc, dtb, alog, dskip, nrm, sel_v, sel_w, sel_e)


def _rms(v, gain):
    return v * lax.rsqrt(jnp.mean(v * v, axis=-1, keepdims=True) + NORM_EPS) * gain


def _merge_kernel(ona_ref, y_ref, gna_ref, gssd_ref, x_ref, ga1_ref, sh2_ref, sc2_ref,
                  wna_ref, wssd_ref, wout_ref, npost_ref, npre_ref, wrh_ref, wrl_ref,
                  x1_ref, h2_ref, aff_ref):
    a = jnp.dot(ona_ref[0], wna_ref[...], preferred_element_type=F32)
    s = jnp.dot(y_ref[0], wssd_ref[...], preferred_element_type=F32)
    u = jax.nn.sigmoid(gna_ref[0].astype(F32)) * a + jax.nn.sigmoid(gssd_ref[0].astype(F32)) * s
    mix = jnp.dot(u.astype(BF16), wout_ref[...], preferred_element_type=F32)
    x1 = x_ref[0] + ga1_ref[0] * _rms(mix, npost_ref[...])
    x1_ref[0] = x1
    h2 = _rms(x1, npre_ref[...]) * (1.0 + sc2_ref[0]) + sh2_ref[0]
    hi = h2.astype(BF16)
    h2_ref[0] = hi
    lo = (h2 - hi.astype(F32)).astype(BF16)
    nt = (((1,), (1,)), ((), ()))
    wrh, wrl = wrh_ref[...], wrl_ref[...]
    logits = (lax.dot_general(wrh, hi, nt, preferred_element_type=F32)
              + lax.dot_general(wrh, lo, nt, preferred_element_type=F32)
              + lax.dot_general(wrl, hi, nt, preferred_element_type=F32))
    e = jnp.exp(logits - jnp.max(logits, axis=0, keepdims=True))
    aff_ref[0] = e / jnp.sum(e, axis=0, keepdims=True)


def _merge(o_na, y_ssd, slab, x, mod3, w_na, w_ssd, w_o, n_post, n_pre, wr_hi, wr_lo):
    bsz, seq, d = x.shape
    tm = 512
    ne = wr_hi.shape[0]
    tok = lambda width, blk: pl.BlockSpec((1, tm, width), lambda b, i: (b, i, blk))
    modv = lambda k: pl.BlockSpec((1, 1, d), lambda b, i: (b, 0, k))
    full = lambda r, c_: pl.BlockSpec((r, c_), lambda b, i: (0, 0))
    return pl.pallas_call(
        _merge_kernel,
        grid=(bsz, seq // tm),
        in_specs=[tok(d, 0), tok(SSD_WIDTH, 0), tok(d, COL_G // d), tok(d, COL_G // d + 1), tok(d, 0),
                  modv(2), modv(3), modv(4),
                  full(d, d), full(SSD_WIDTH, d), full(d, d), full(1, d), full(1, d), full(ne, d), full(ne, d)],
        out_specs=[tok(d, 0), tok(d, 0), pl.BlockSpec((1, ne, tm), lambda b, i: (b, 0, i))],
        out_shape=[jax.ShapeDtypeStruct((bsz, seq, d), F32), jax.ShapeDtypeStruct((bsz, seq, d), BF16),
                   jax.ShapeDtypeStruct((bsz, ne, seq), F32)],
        compiler_params=_params(("parallel", "parallel")),
        name="merge",
    )(o_na, y_ssd, slab, slab, x, mod3, mod3, mod3, w_na, w_ssd, w_o, n_post, n_pre, wr_hi, wr_lo)


def _prefix_count(mask_bf16, strict_upper):
    r, s = mask_bf16.shape
    offset = jnp.zeros((r, 1), F32)
    parts = []
    for j in range(s // LANES):
        seg = mask_bf16[:, j * LANES:(j + 1) * LANES]
        parts.append(jnp.dot(seg, strict_upper, preferred_element_type=F32) + offset)
        offset = offset + jnp.sum(seg.astype(F32), axis=1, keepdims=True)
    return jnp.concatenate(parts, axis=1)


def _route_kernel(aff_ref, h2_ref, pos_ref, xg_ref, gate_ref, *, cap):
    aff = aff_ref[0]
    bits = pltpu.bitcast(aff, jnp.int32)
    ne, t = aff.shape
    thr = jnp.zeros((ne, 1), jnp.int32)
    for bit in range(30, -1, -1):
        cand = thr | (1 << bit)
        cnt = jnp.sum(jnp.where(bits >= cand, 1.0, 0.0), axis=1, keepdims=True)
        thr = jnp.where(cnt >= cap, cand, thr)
    above = bits > thr
    tied = bits == thr
    need = cap - jnp.sum(jnp.where(above, 1.0, 0.0), axis=1, keepdims=True)
    ri = lax.broadcasted_iota(jnp.int32, (LANES, LANES), 0)
    ci = lax.broadcasted_iota(jnp.int32, (LANES, LANES), 1)
    strict_upper = jnp.where(ri < ci, 1.0, 0.0).astype(BF16)
    tie_rank = _prefix_count(jnp.where(tied, 1.0, 0.0).astype(BF16), strict_upper)
    sel = above | (tied & (tie_rank < need))
    slot = _prefix_count(jnp.where(sel, 1.0, 0.0).astype(BF16), strict_upper)
    pos = jnp.where(sel, slot.astype(jnp.int32), -1)
    pos_ref[0] = pos
    h2 = h2_ref[0]
    slot_id = lax.broadcasted_iota(jnp.int32, (cap, t), 0)
    for e in range(ne):
        hit = slot_id == pos[e:e + 1, :]
        onehot = jnp.where(hit, 1.0, 0.0).astype(BF16)
        xg_ref[e] = jnp.dot(onehot, h2, preferred_element_type=F32).astype(xg_ref.dtype)
        gate_ref[e] = jnp.sum(jnp.where(hit, aff[e:e + 1, :], 0.0), axis=1, keepdims=True)


def _route(aff_t, h2):
    bsz, ne, t = aff_t.shape
    d = h2.shape[2]
    cap = EC_CAPACITY_FACTOR * t // ne
    return pl.pallas_call(
        functools.partial(_route_kernel, cap=cap),
        grid=(bsz,),
        in_specs=[pl.BlockSpec((1, ne, t), lambda b: (b, 0, 0)),
                  pl.BlockSpec((1, t, d), lambda b: (b, 0, 0))],
        out_specs=[pl.BlockSpec((1, ne, t), lambda b: (b, 0, 0)),
                   pl.BlockSpec((ne, cap, d), lambda b: (0, b, 0)),
                   pl.BlockSpec((ne, cap, 1), lambda b: (0, b, 0))],
        out_shape=[jax.ShapeDtypeStruct((bsz, ne, t), jnp.int32),
                   jax.ShapeDtypeStruct((ne, bsz * cap, d), BF16),
                   jax.ShapeDtypeStruct((ne, bsz * cap, 1), F32)],
        compiler_params=_params(("parallel",)),
        name="route",
    )(aff_t, h2)


EXPERT_FF_TILE = 512
EXPERT_ROW_TILE = 512


def _expert_kernel(xg_ref, gate_ref, wg_ref, wu_ref, wd_ref, yo_ref, acc_ref):
    f = pl.program_id(1)
    wg, wu, wd = wg_ref[0].astype(BF16), wu_ref[0].astype(BF16), wd_ref[0].astype(BF16)
    n = xg_ref.shape[1]
    rt = min(EXPERT_ROW_TILE, n)
    for r in range(n // rt):
        rows = pl.ds(r * rt, rt)
        xg = xg_ref[0, rows, :]
        hid = _silu(jnp.dot(xg, wg, preferred_element_type=F32)) * jnp.dot(xg, wu, preferred_element_type=F32)
        part = jnp.dot(hid.astype(BF16), wd, preferred_element_type=F32)

        @pl.when(f == 0)
        def _():
            acc_ref[rows, :] = part

        @pl.when(f > 0)
        def _():
            acc_ref[rows, :] += part

    @pl.when(f == pl.num_programs(1) - 1)
    def _():
        cap = yo_ref.shape[1]
        for b in range(yo_ref.shape[0]):
            rows = pl.ds(b * cap, cap)
            yo_ref[b] = (acc_ref[rows, :] * gate_ref[0, rows, :]).astype(yo_ref.dtype)


def _experts(xg, gate, w_g, w_u, w_d, bsz):
    ne, n, d = xg.shape
    cap = n // bsz
    ff = w_g.shape[2]
    tf = EXPERT_FF_TILE
    return pl.pallas_call(
        _expert_kernel,
        grid=(ne, ff // tf),
        in_specs=[pl.BlockSpec((1, n, d), lambda e, f: (e, 0, 0)),
                  pl.BlockSpec((1, n, 1), lambda e, f: (e, 0, 0)),
                  pl.BlockSpec((1, d, tf), lambda e, f: (e, 0, f)),
                  pl.BlockSpec((1, d, tf), lambda e, f: (e, 0, f)),
                  pl.BlockSpec((1, tf, d), lambda e, f: (e, f, 0))],
        out_specs=pl.BlockSpec((bsz, cap, d), lambda e, f: (0, e, 0)),
        out_shape=jax.ShapeDtypeStruct((bsz, ne * cap, d), BF16),
        scratch_shapes=[pltpu.VMEM((n, d), F32)],
        compiler_params=_params(("parallel", "arbitrary")),
        name="experts",
    )(xg, gate, w_g, w_u, w_d)


def _combine_kernel(pos_ref, yo_ref, x1_ref, ga2_ref, npost_ref, o_ref, *, cap):
    pos = pos_ref[0]
    slot_id = lax.broadcasted_iota(jnp.int32, (1, cap), 1)
    onehot = jnp.concatenate(
        [jnp.where(pos[:, e:e + 1] == slot_id, 1.0, 0.0).astype(BF16) for e in range(pos.shape[1])], axis=1)
    moe = jnp.dot(onehot, yo_ref[0], preferred_element_type=F32)
    o_ref[0] = x1_ref[0] + ga2_ref[0] * _rms(moe, npost_ref[...])


def _combine(pos_t, yo, x1, mod3, n_post):
    bsz, seq, d = x1.shape
    ne = pos_t.shape[2]
    cap = yo.shape[1] // ne
    tm = 512
    return pl.pallas_call(
        functools.partial(_combine_kernel, cap=cap),
        grid=(bsz, seq // tm),
        in_specs=[pl.BlockSpec((1, tm, ne), lambda b, i: (b, i, 0)),
                  pl.BlockSpec((1, ne * cap, d), lambda b, i: (b, 0, 0)),
                  pl.BlockSpec((1, tm, d), lambda b, i: (b, i, 0)),
                  pl.BlockSpec((1, 1, d), lambda b, i: (b, 0, 5)),
                  pl.BlockSpec((1, d), lambda b, i: (0, 0))],
        out_specs=pl.BlockSpec((1, tm, d), lambda b, i: (b, i, 0)),
        out_shape=jax.ShapeDtypeStruct((bsz, seq, d), F32),
        compiler_params=_params(("parallel", "parallel")),
        name="combine",
    )(pos_t, yo, x1, mod3, n_post)


def kernel(x, c, ctx, c_ctx, w_ada, b_ada, norm_pre_mix, norm_post_mix, norm_pre_ffn, norm_post_ffn,
           w_in, na_rpb, ssd_conv_w, ssd_conv_b, ssd_a_log_fwd, ssd_a_log_bwd, ssd_dt_bias_fwd,
           ssd_dt_bias_bwd, ssd_d_skip, ssd_norm, w_branch_na, w_branch_ssd, w_out, w_router,
           w_exp_gate, w_exp_up, w_exp_down):
    mod, slab, o_na, y_ssd = _front(x, c, ctx, c_ctx, w_ada[0], b_ada[0], norm_pre_mix[0], w_in[0], na_rpb[0],
                                    ssd_conv_w[0], ssd_conv_b[0], ssd_a_log_fwd[0], ssd_a_log_bwd[0],
                                    ssd_dt_bias_fwd[0], ssd_dt_bias_bwd[0], ssd_d_skip[0], ssd_norm[0])
    return _back(x, mod, slab, o_na, y_ssd, norm_post_mix[0], norm_pre_ffn[0], norm_post_ffn[0],
                 w_branch_na[0], w_branch_ssd[0], w_out[0], w_router[0], w_exp_gate[0], w_exp_up[0], w_exp_down[0])


def _back(x, mod, slab, o_na, y_ssd, norm_post_mix, norm_pre_ffn, norm_post_ffn,
          w_branch_na, w_branch_ssd, w_out, w_router, w_exp_gate, w_exp_up, w_exp_down):
    bsz, seq, d = x.shape
    mod3 = mod.reshape(16, 1, N_MOD * d)
    wr_t = w_router.T
    wr_hi = wr_t.astype(BF16)
    wr_lo = (wr_t - wr_hi.astype(F32)).astype(BF16)
    x1, h2, aff_t = _merge(o_na, y_ssd, slab, x, mod3, w_branch_na.astype(BF16), w_branch_ssd.astype(BF16),
                           w_out.astype(BF16), norm_post_mix.reshape(1, d), norm_pre_ffn.reshape(1, d),
                           wr_hi, wr_lo)
    pos, xg, gate = _route(aff_t, h2)
    yo = _experts(xg, gate, w_exp_gate, w_exp_up, w_exp_down, bsz)
    return _combine(jnp.swapaxes(pos, 1, 2), yo, x1, mod3, norm_post_ffn.reshape(1, d))


def _group_rows(fwd, bwd):
    rows = []
    for g in range(SSD_GROUPS):
        sl = slice(g * SSD_GHEADS, (g + 1) * SSD_GHEADS)
        rows += [fwd[sl], bwd[sl]]
    return jnp.concatenate(rows, axis=0)


def _front(x, c, ctx, c_ctx, w_ada, b_ada, norm_pre_mix, w_in, na_rpb, ssd_conv_w, ssd_conv_b,
           ssd_a_log_fwd, ssd_a_log_bwd, ssd_dt_bias_fwd, ssd_dt_bias_bwd, ssd_d_skip, ssd_norm):
    bsz, seq, d = x.shape
    tctx = ctx.shape[1]
    assert bsz <= 8
    cc =jnp.zeros((16, d), F32).at[:bsz].set(c).at[8].set(c_ctx)
    mod = _ada(cc, w_ada, b_ada)
    mod3 = mod.reshape(16, 1, N_MOD * d)

    dt0 = COL_G
    w_lat = jnp.concatenate([w_in[:, :dt0], w_in[:, dt0 + 2 * SSD_HEADS:]], axis=1).astype(BF16)
    w_dt = _group_rows(w_in[:, dt0:dt0 + SSD_HEADS].T, w_in[:, dt0 + SSD_HEADS:dt0 + 2 * SSD_HEADS].T).astype(BF16)
    rope = _rope_tables(seq)
    g_pre = norm_pre_mix.reshape(1, d)

    slab, dt = _inproj(x, mod3, 0, g_pre, w_lat, w_dt, rope, 2, LAT_COLS // 1024, lambda j: j)
    nctx = bsz * tctx
    ctx_rows = -(-nctx // 1024) * 1024
    ctx_flat = ctx.reshape(1, nctx, d)
    if ctx_rows != nctx:
        ctx_flat = jnp.pad(ctx_flat, ((0, 0), (0, ctx_rows - nctx), (0, 0)))
    ctx_tile = lambda j: jnp.where(j < 2, j + COL_K // 1024, j - 2 + COL_X // 1024)
    slab_c, dt_c = _inproj(ctx_flat, mod3, 8, g_pre, w_lat, w_dt, rope, 0, CTX_COLS // 1024, ctx_tile)
    slab_c = slab_c[0, :nctx].reshape(bsz, tctx, CTX_COLS)
    dt_c = jnp.swapaxes(dt_c[0, :, :nctx].reshape(-1, bsz, tctx), 0, 1)

    o_na = _na(slab, slab_c, _na_bias(na_rpb, seq // GRID_W))

    nx, nb = SSD_WIDTH, SSD_GROUPS * SSD_STATE
    cw = (ssd_conv_w[:, :nx], ssd_conv_w[:, nx:nx + nb], ssd_conv_w[:, nx + nb:])
    cb = tuple(v.reshape(1, -1) for v in (ssd_conv_b[:nx], ssd_conv_b[nx:nx + nb], ssd_conv_b[nx + nb:]))
    lanes = lambda v: jnp.broadcast_to(v[:, None], (v.shape[0], LANES))
    dtb = _group_rows(lanes(ssd_dt_bias_fwd), lanes(ssd_dt_bias_bwd))
    alog = _group_rows(lanes(ssd_a_log_fwd), lanes(ssd_a_log_bwd))
    dskip = jnp.repeat(ssd_d_skip, SSD_HEAD_DIM).reshape(1, -1)
    y_ssd = _ssd(slab, dt, slab_c, dt_c, cw, cb, dtb, alog, dskip, ssd_norm.reshape(1, -1))
    return mod, slab, o_na, y_ssd
```

```python
import functools
import math

import jax
import jax.numpy as jnp
import numpy as np
from jax import lax
from jax.experimental import pallas as pl
from jax.experimental.pallas import tpu as pltpu

F32 = jnp.float32
BF16 = jnp.bfloat16

D_MODEL = 1024
GRID_W = 64
NORM_EPS = 1e-6
N_MOD = 6
NA_HEADS = 16
NA_HEAD_DIM = 64
NA_KH = 8
NA_KW = 16
ROPE_BASE = 10000.0
SSD_WIDTH = 2048
SSD_HEAD_DIM = 64
SSD_HEADS = 32
SSD_GROUPS = 4
SSD_STATE = 128
SSD_CONV = 5
SSD_CHUNK = 128
N_EXPERTS = 16
EXPERT_FF = 2048
EC_CAPACITY_FACTOR = 2

VMEM_LIMIT_BYTES = 56 * 1024 * 1024
LANES = 128

COL_Q, COL_K, COL_V, COL_Z, COL_X, COL_B, COL_C, COL_G = 0, 1024, 2048, 3072, 5120, 7168, 7680, 8192
LAT_COLS = 10240
CTX_K, CTX_V, CTX_X, CTX_B, CTX_C = 0, 1024, 2048, 4096, 4608
CTX_COLS = 5120


def _params(semantics):
    return pltpu.CompilerParams(dimension_semantics=semantics, vmem_limit_bytes=VMEM_LIMIT_BYTES)


def _silu(v):
    return v * jax.nn.sigmoid(v)


def _softplus(v):
    return jnp.maximum(v, 0.0) + jnp.log1p(jnp.exp(-jnp.abs(v)))


def _ada_kernel(c_ref, w_ref, b_ref, o_ref):
    o_ref[...] = jnp.dot(_silu(c_ref[...]), w_ref[...], preferred_element_type=F32) + b_ref[...]


def _ada(cc, w_ada, b_ada):
    rows, d = cc.shape
    n = w_ada.shape[1]
    tn = 1536
    return pl.pallas_call(
        _ada_kernel,
        grid=(n // tn,),
        in_specs=[pl.BlockSpec((rows, d), lambda j: (0, 0)),
                  pl.BlockSpec((d, tn), lambda j: (0, j)),
                  pl.BlockSpec((1, tn), lambda j: (0, j))],
        out_specs=pl.BlockSpec((rows, tn), lambda j: (0, j)),
        out_shape=jax.ShapeDtypeStruct((rows, n), F32),
        compiler_params=_params(("arbitrary",)),
        name="ada",
    )(cc, w_ada, b_ada.reshape(1, n))


def _rope_tables(seq):
    lane = np.arange(LANES)
    axis = (lane % NA_HEAD_DIM) // 32
    within = lane % 32
    half = 16
    inv_freq = ROPE_BASE ** (-(within % half).astype(np.float64) / half)
    pos = np.arange(seq)
    coord = np.where(axis[None, :] == 0, (pos // GRID_W)[:, None], (pos % GRID_W)[:, None])
    ang = coord.astype(np.float32) * inv_freq.astype(np.float32)[None, :]
    cos, sin = jnp.cos(jnp.asarray(ang, F32)), jnp.sin(jnp.asarray(ang, F32))
    first = jnp.asarray(within < half)[None, :]
    return cos, jnp.where(first, -sin, 0.0), jnp.where(first, 0.0, sin)


def _inproj_kernel(x_ref, sh_ref, sc_ref, g_ref, w_ref, wgate_ref, wdt_ref, cos_ref, s1_ref, s2_ref,
                   o_ref, dt_ref, h_scr, *, rope_tiles, main_tiles):
    j = pl.program_id(2)

    @pl.when(j == 0)
    def _():
        x = x_ref[0]
        xn = x * lax.rsqrt(jnp.mean(x * x, axis=-1, keepdims=True) + NORM_EPS) * g_ref[...]
        h = (xn * (1.0 + sc_ref[0]) + sh_ref[0]).astype(BF16)
        h_scr[...] = h
        dt_ref[0] = lax.dot_general(wdt_ref[...], h, (((1,), (1,)), ((), ())), preferred_element_type=F32)

    h = h_scr[...]
    chunks = range(0, o_ref.shape[2], INPROJ_CHUNK)

    def plain(weights):
        for c in chunks:
            acc = jnp.dot(h, weights[:, c:c + INPROJ_CHUNK], preferred_element_type=F32)
            o_ref[0, :, c:c + INPROJ_CHUNK] = acc.astype(o_ref.dtype)

    pl.when((j >= rope_tiles) & (j < main_tiles))(lambda: plain(w_ref))
    pl.when(j >= main_tiles)(lambda: plain(wgate_ref))

    if rope_tiles:
        @pl.when(j < rope_tiles)
        def _():
            scale = jnp.where(j == 0, NA_HEAD_DIM ** -0.5, 1.0).astype(F32)
            cos, s1, s2 = cos_ref[...] * scale, s1_ref[...] * scale, s2_ref[...] * scale
            for c in chunks:
                acc = jnp.dot(h, w_ref[:, c:c + INPROJ_CHUNK], preferred_element_type=F32)
                for l in range(0, INPROJ_CHUNK, LANES):
                    t = acc[:, l:l + LANES]
                    r = t * cos + pltpu.roll(t, LANES - 16, 1) * s1 + pltpu.roll(t, 16, 1) * s2
                    o_ref[0, :, c + l:c + l + LANES] = r.astype(o_ref.dtype)


INPROJ_CHUNK = 256


def _inproj(x, mod3, mod_row0, g_pre, w, w_gate, w_dt, rope, rope_tiles, main_tiles, n_tiles, w_tile):
    bsz, seq, d = x.shape
    ndt = w_dt.shape[0]
    tm, tn = 1024, 1024
    n = n_tiles * tn
    cos, s1, s2 = rope
    kern = functools.partial(_inproj_kernel, rope_tiles=rope_tiles, main_tiles=main_tiles)
    return pl.pallas_call(
        kern,
        grid=(bsz, seq // tm, n_tiles),
        in_specs=[pl.BlockSpec((1, tm, d), lambda b, i, j: (b, i, 0)),
                  pl.BlockSpec((1, 1, d), lambda b, i, j: (b + mod_row0, 0, 0)),
                  pl.BlockSpec((1, 1, d), lambda b, i, j: (b + mod_row0, 0, 1)),
                  pl.BlockSpec((1, d), lambda b, i, j: (0, 0)),
                  pl.BlockSpec((d, tn), lambda b, i, j: (0, w_tile(jnp.minimum(j, main_tiles - 1)))),
                  pl.BlockSpec((d, tn), lambda b, i, j: (0, jnp.maximum(j - main_tiles, 0))),
                  pl.BlockSpec((ndt, d), lambda b, i, j: (0, 0)),
                  pl.BlockSpec((tm, LANES), lambda b, i, j: (i, 0)),
                  pl.BlockSpec((tm, LANES), lambda b, i, j: (i, 0)),
                  pl.BlockSpec((tm, LANES), lambda b, i, j: (i, 0))],
        out_specs=[pl.BlockSpec((1, tm, tn), lambda b, i, j: (b, i, j)),
                   pl.BlockSpec((1, ndt, tm), lambda b, i, j: (b, 0, i))],
        out_shape=[jax.ShapeDtypeStruct((bsz, seq, n), BF16),
                   jax.ShapeDtypeStruct((bsz, ndt, seq), F32)],
        scratch_shapes=[pltpu.VMEM((tm, d), BF16)],
        compiler_params=_params(("parallel", "parallel", "arbitrary")),
        name="inproj",
    )(x, mod3, mod3, g_pre, w, w_gate, w_dt, cos, s1, s2)


NA_QROWS = 4
NA_KROWS = 12
NA_QBLK = NA_QROWS * GRID_W
NA_KBLK = NA_KROWS * GRID_W
NA_MASKED = -1e30


def _na_key_start(g, rows):
    return min(max(NA_QROWS * g - NA_KH // 2, 0), rows - NA_KROWS)


def _na_bias(rpb, rows):
    heads = rpb.shape[0]
    col = np.arange(GRID_W)
    col_start = np.clip(col - NA_KW // 2, 0, GRID_W - NA_KW)
    col_ok = (col[None, :] >= col_start[:, None]) & (col[None, :] < col_start[:, None] + NA_KW)
    edge = GRID_W - NA_KW
    padded = jnp.pad(rpb, ((0, 0), (0, 0), (edge, edge)), mode="edge")
    n = 2 * GRID_W - 1
    tiled = jnp.tile(padded, (1, 1, GRID_W + 1))[..., :GRID_W * (n + 1)].reshape(heads, -1, GRID_W, n + 1)
    toe = tiled[:, :, ::-1, :GRID_W]
    toe = jnp.pad(toe, ((0, 0), (NA_QROWS, NA_QROWS), (0, 0), (0, 0)), mode="edge")
    flat = jnp.transpose(toe, (0, 2, 1, 3)).reshape(heads, GRID_W, -1)
    nblk = rows // NA_QROWS
    out = []
    for g in (0, 1, nblk - 1):
        r = NA_QROWS * g + np.arange(NA_QROWS)
        kr = _na_key_start(g, rows) + np.arange(NA_KROWS)
        r0 = np.clip(r - NA_KH // 2, 0, rows - NA_KH)
        row_ok = (kr[None, :] >= r0[:, None]) & (kr[None, :] < r0[:, None] + NA_KH)
        first = int(kr[0] - r[0]) + NA_KH - 1 + NA_QROWS
        b = jnp.stack([flat[:, :, (first - dr) * GRID_W:(first - dr) * GRID_W + NA_KBLK]
                       for dr in range(NA_QROWS)], axis=1)
        ok = (row_ok[:, None, :, None] & col_ok[None, :, None, :]).reshape(NA_QROWS, GRID_W, NA_KBLK)
        out.append(jnp.where(ok[None], b, NA_MASKED).reshape(heads, NA_QBLK, NA_KBLK).astype(F32))
    return out


def _na_kernel(q_ref, k_ref, v_ref, kc_ref, vc_ref, bias_first, bias_mid, bias_last, o_ref, *, rows):
    nblk = rows // NA_QROWS
    first_head = lax.broadcasted_iota(jnp.int32, (1, LANES), 1) < NA_HEAD_DIM
    kc, vc = kc_ref[0], vc_ref[0]
    nt = (((1,), (1,)), ((), ()))
    for g in range(nblk):
        k0 = _na_key_start(g, rows) * GRID_W
        bias_ref = bias_first if g == 0 else (bias_last if g == nblk - 1 else bias_mid)
        q = q_ref[0, g * NA_QBLK:(g + 1) * NA_QBLK, :]
        kw = k_ref[0, k0:k0 + NA_KBLK, :]
        vw = v_ref[0, k0:k0 + NA_KBLK, :]
        outs = []
        for hh in range(2):
            qm = jnp.where(first_head if hh == 0 else ~first_head, q, jnp.zeros_like(q))
            s = lax.dot_general(qm, kw, nt, preferred_element_type=F32) + bias_ref[hh]
            sc = lax.dot_general(qm, kc, nt, preferred_element_type=F32)
            m = jnp.maximum(jnp.max(s, axis=-1, keepdims=True), jnp.max(sc, axis=-1, keepdims=True))
            p, pc = jnp.exp(s - m), jnp.exp(sc - m)
            denom = jnp.sum(p, axis=-1, keepdims=True) + jnp.sum(pc, axis=-1, keepdims=True)
            o = (jnp.dot(p.astype(BF16), vw, preferred_element_type=F32)
                 + jnp.dot(pc.astype(BF16), vc, preferred_element_type=F32))
            outs.append(o / denom)
        o_ref[0, g * NA_QBLK:(g + 1) * NA_QBLK, :] = jnp.where(first_head, outs[0], outs[1]).astype(o_ref.dtype)


def _na(slab, slab_ctx, biases):
    bsz, seq, _ = slab.shape
    tctx = slab_ctx.shape[1]
    pairs = NA_HEADS // 2
    blk = lambda col0: (lambda hp, b: (b, 0, col0 // LANES + hp))
    bias_spec = pl.BlockSpec((2, NA_QBLK, NA_KBLK), lambda hp, b: (hp, 0, 0))
    return pl.pallas_call(
        functools.partial(_na_kernel, rows=seq // GRID_W),
        grid=(pairs, bsz),
        in_specs=[pl.BlockSpec((1, seq, LANES), blk(COL_Q)),
                  pl.BlockSpec((1, seq, LANES), blk(COL_K)),
                  pl.BlockSpec((1, seq, LANES), blk(COL_V)),
                  pl.BlockSpec((1, tctx, LANES), blk(CTX_K)),
                  pl.BlockSpec((1, tctx, LANES), blk(CTX_V)),
                  bias_spec, bias_spec, bias_spec],
        out_specs=pl.BlockSpec((1, seq, LANES), lambda hp, b: (b, 0, hp)),
        out_shape=jax.ShapeDtypeStruct((bsz, seq, NA_HEADS * NA_HEAD_DIM), BF16),
        compiler_params=_params(("parallel", "parallel")),
        name="na",
    )(slab, slab, slab, slab_ctx, slab_ctx, *biases)


SSD_GHEADS = SSD_HEADS // SSD_GROUPS
SSD_GWIDTH = SSD_GHEADS * SSD_HEAD_DIM
CONV_HALO = 8


SSD_ROWS = 2 * SSD_GHEADS
PACK_V, PACK_W, PACK_E = 0, 3 * SSD_ROWS, 5 * SSD_ROWS


def _ssd_selectors():
    k = np.arange(LANES)[:, None]
    row = k % SSD_ROWS

    def sel(first, terms, head_of_col):
        live = (k >= first) & (k < first + terms * SSD_ROWS)
        return jnp.asarray((live & (row == head_of_col[None, :])).astype(np.float32), BF16)

    col = np.arange(2 * SSD_GWIDTH)
    head_dir = np.where(col < SSD_GWIDTH, col // SSD_HEAD_DIM, SSD_GHEADS + (col - SSD_GWIDTH) // SSD_HEAD_DIM)
    sel_v = sel(PACK_V, 3, np.arange(SSD_ROWS * SSD_CHUNK) // SSD_CHUNK)
    return sel_v, sel(PACK_W, 2, head_dir), sel(PACK_E, 2, head_dir)


def _ssd_kernel(xs_ref, b_ref, c_ref, z_ref, dt_ref, xsc_ref, bc_ref, cc_ref, dtc_ref,
                cwx_ref, cwb_ref, cwc_ref, cbx_ref, cbb_ref, cbc_ref, dtb_ref, alog_ref, dskip_ref, nrm_ref,
                selv_ref, selw_ref, sele_ref,
                y_ref,
                padx, padb, padc, xs_s, bt_s, c_s, pack_s, vrow_s, dtrow_s, hprev_s, sb_s, decb_s, hf_s, hb_s,
                *, seq, tctx):
    Q = SSD_CHUNK
    GH = SSD_GHEADS
    GW = SSD_GWIDTH
    rows = lax.broadcasted_iota(jnp.int32, (Q, Q), 0)
    cols = lax.broadcasted_iota(jnp.int32, (Q, Q), 1)
    cum_rhs = jnp.concatenate([jnp.where(rows <= cols, 1.0, 0.0), jnp.ones((Q, Q), F32)], axis=1).astype(BF16)
    is_fwd = lax.broadcasted_iota(jnp.int32, (SSD_ROWS, 1), 0) < GH
    first_head = lax.broadcasted_iota(jnp.int32, (1, LANES), 1) < SSD_HEAD_DIM
    a_coef = -jnp.exp(alog_ref[...])
    dt_bias = dtb_ref[...]

    def bf_terms(v, n):
        out, rem = [], v
        for _ in range(n):
            t = rem.astype(BF16).astype(F32)
            out.append(t)
            rem = rem - t
        return out

    def prep(x_raw, b_raw, c_raw, dt_raw, n, store_prev):
        for pad, raw in ((padx, x_raw), (padb, b_raw), (padc, c_raw)):
            width = pad.shape[1]
            pad[0:CONV_HALO, :] = jnp.zeros((CONV_HALO, width), F32)
            pad[CONV_HALO + n:2 * CONV_HALO + n, :] = jnp.zeros((CONV_HALO, width), F32)

        def stage(c, carry):
            r0 = pl.multiple_of(c * Q, Q)
            for pad, raw in ((padx, x_raw), (padb, b_raw), (padc, c_raw)):
                pad[pl.ds(r0 + CONV_HALO, Q), :] = raw[0, pl.ds(r0, Q), :].astype(F32)
            return carry

        lax.fori_loop(0, n // Q, stage, 0)

        def conv(pad, w_ref, bias_ref, r0):
            first = CONV_HALO - SSD_CONV // 2
            tiles = []
            for lo in range(0, pad.shape[1], LANES):
                win = pad[pl.ds(r0, Q + 2 * CONV_HALO), lo:lo + LANES]
                acc = bias_ref[:, lo:lo + LANES] + win[first:first + Q, :] * w_ref[0:1, lo:lo + LANES]
                for k in range(1, SSD_CONV):
                    acc = acc + win[first + k:first + k + Q, :] * w_ref[k:k + 1, lo:lo + LANES]
                tiles.append(_silu(acc))
            return tiles[0] if len(tiles) == 1 else jnp.concatenate(tiles, axis=1)

        def body(c, carry):
            r0 = pl.multiple_of(c * Q, Q)
            dt = _softplus(dt_raw[0, :, pl.ds(r0, Q)] + dt_bias)
            a = dt * a_coef
            cs = jnp.dot(jnp.concatenate([t.astype(BF16) for t in bf_terms(a, 3)], axis=0), cum_rhs,
                         preferred_element_type=F32)
            cs = cs[0:SSD_ROWS] + cs[SSD_ROWS:2 * SSD_ROWS] + cs[2 * SSD_ROWS:3 * SSD_ROWS]
            acs, tot = cs[:, :Q], cs[:, Q:]
            ex = acs - a
            v = jnp.where(is_fwd, acs, ex)
            w = dt * jnp.exp(jnp.where(is_fwd, tot - acs, ex))
            e = jnp.exp(jnp.where(is_fwd, acs, tot - ex))
            vrow_s[c] = v
            dtrow_s[c] = dt
            packed = jnp.concatenate(bf_terms(v, 3) + bf_terms(w, 2) + bf_terms(e, 2)
                                     + [jnp.zeros((SSD_ROWS, Q), F32)], axis=0)
            pk = packed.T.astype(BF16)
            pack_s[pl.ds(r0, Q), :] = pk
            wexp = jnp.dot(pk, selw_ref[...], preferred_element_type=F32)
            edge = jnp.concatenate([pk[0:16, :], pk[Q - 16:Q, :]], axis=0)
            dec = jnp.dot(edge, sele_ref[...], preferred_element_type=F32)
            b_t = conv(padb, cwb_ref, cbb_ref, r0).T.astype(BF16)
            bt_s[c] = b_t
            c_s[pl.ds(r0, Q), :] = conv(padc, cwc_ref, cbc_ref, r0).astype(BF16)
            xs = conv(padx, cwx_ref, cbx_ref, r0)
            xs_s[pl.ds(r0, Q), :] = xs
            xdec = jnp.concatenate([xs * wexp[:, :GW], xs * wexp[:, GW:]], axis=1).astype(BF16)
            s = jnp.dot(b_t, xdec, preferred_element_type=F32)
            if store_prev:
                hprev_s[c] = hf_s[...].astype(BF16)
            hf_s[...] = dec[31:32, :GW] * hf_s[...] + s[:, :GW]
            sb_s[c] = s[:, GW:]
            decb_s[c] = jnp.broadcast_to(dec[0:1, GW:], (8, GW))
            return carry

        lax.fori_loop(0, n // Q, body, 0, unroll=2)

    def backward_step(c):
        hb_s[...] = decb_s[c][0:1, :] * hb_s[...] + sb_s[c]

    hf_s[...] = jnp.zeros(hf_s.shape, F32)
    hb_s[...] = jnp.zeros(hb_s.shape, F32)

    prep(xsc_ref, bc_ref, cc_ref, dtc_ref, tctx, False)
    nctx = tctx // Q

    def ctx_back(i, carry):
        backward_step(nctx - 1 - i)
        return carry

    lax.fori_loop(0, nctx, ctx_back, 0)

    prep(xs_ref, b_ref, c_ref, dt_ref, seq, True)
    nch = seq // Q
    dskip = dskip_ref[...]
    gain = nrm_ref[...]

    def ybody(i, carry):
        c = nch - 1 - i
        r0 = pl.multiple_of(c * Q, Q)
        pk = pack_s[pl.ds(r0, Q), :]
        vcol = jnp.dot(pk, selv_ref[...], preferred_element_type=F32)
        eexp = jnp.dot(pk, sele_ref[...], preferred_element_type=F32)
        v_t, dt_t = vrow_s[c], dtrow_s[c]
        xs = xs_s[pl.ds(r0, Q), :]
        xsb = xs.astype(BF16)
        cm = c_s[pl.ds(r0, Q), :]
        cb = jnp.dot(cm, bt_s[c], preferred_element_type=F32)
        y_off = (jnp.dot(cm, hprev_s[c], preferred_element_type=F32) * eexp[:, :GW]
                 + jnp.dot(cm, hb_s[...].astype(BF16), preferred_element_type=F32) * eexp[:, GW:])
        gate = _silu(z_ref[0, pl.ds(r0, Q), :].astype(F32))
        tiles = []
        for j in range(GH // 2):
            xpair = xsb[:, j * LANES:(j + 1) * LANES]
            ys = []
            for hh in range(2):
                hf, hb = 2 * j + hh, GH + 2 * j + hh
                arg_f = vcol[:, hf * Q:(hf + 1) * Q] - v_t[hf:hf + 1, :]
                arg_b = v_t[hb:hb + 1, :] - vcol[:, hb * Q:(hb + 1) * Q]
                e = jnp.exp(jnp.where(rows >= cols, arg_f, arg_b))
                dt_f, dt_b = dt_t[hf:hf + 1, :], dt_t[hb:hb + 1, :]
                fac = jnp.where(rows > cols, dt_f, jnp.where(rows < cols, dt_b, dt_f + dt_b))
                m = (cb * e * fac).astype(BF16)
                ys.append(jnp.dot(m, xpair, preferred_element_type=F32))
            tiles.append(jnp.where(first_head, ys[0], ys[1]))
        y = jnp.concatenate(tiles, axis=1) + y_off + dskip * xs
        u = y * gate
        u = u * lax.rsqrt(jnp.mean(u * u, axis=-1, keepdims=True) + NORM_EPS) * gain
        y_ref[0, pl.ds(r0, Q), :] = u.astype(y_ref.dtype)
        backward_step(c)
        return carry

    lax.fori_loop(0, nch, ybody, 0, unroll=2)


def _ssd(slab, dt, slab_ctx, dt_ctx, cw, cb, dtb, alog, dskip, nrm):
    bsz, seq, _ = slab.shape
    tctx = slab_ctx.shape[1]
    gw, ns = SSD_GWIDTH, SSD_STATE
    nch = seq // SSD_CHUNK
    cwx, cwb, cwc = cw
    cbx, cbb, cbc = cb

    def tok(width, col0, n):
        return pl.BlockSpec((1, n, width), lambda b, g: (b, 0, col0 // width + g))

    def par(rows, width):
        return pl.BlockSpec((rows, width), lambda b, g: (0, g))

    def head_rows(n):
        return pl.BlockSpec((1, SSD_ROWS, n), lambda b, g: (b, g, 0))

    def const(a):
        return pl.BlockSpec(a.shape, lambda b, g: (0, 0))

    sel_v, sel_w, sel_e = _ssd_selectors()
    pad = seq + 2 * CONV_HALO
    return pl.pallas_call(
        functools.partial(_ssd_kernel, seq=seq, tctx=tctx),
        grid=(bsz, SSD_GROUPS),
        in_specs=[tok(gw, COL_X, seq), tok(ns, COL_B, seq), tok(ns, COL_C, seq), tok(gw, COL_Z, seq),
                  head_rows(seq),
                  tok(gw, CTX_X, tctx), tok(ns, CTX_B, tctx), tok(ns, CTX_C, tctx), head_rows(tctx),
                  par(SSD_CONV, gw), par(SSD_CONV, ns), par(SSD_CONV, ns),
                  par(1, gw), par(1, ns), par(1, ns),
                  pl.BlockSpec((SSD_ROWS, LANES), lambda b, g: (g, 0)),
                  pl.BlockSpec((SSD_ROWS, LANES), lambda b, g: (g, 0)),
                  par(1, gw), par(1, gw), const(sel_v), const(sel_w), const(sel_e)],
        out_specs=pl.BlockSpec((1, seq, gw), lambda b, g: (b, 0, g)),
        out_shape=jax.ShapeDtypeStruct((bsz, seq, SSD_WIDTH), BF16),
        scratch_shapes=[pltpu.VMEM((pad, gw), F32), pltpu.VMEM((pad, ns), F32), pltpu.VMEM((pad, ns), F32),
                        pltpu.VMEM((seq, gw), F32), pltpu.VMEM((nch, ns, SSD_CHUNK), BF16),
                        pltpu.VMEM((seq, ns), BF16), pltpu.VMEM((seq, LANES), BF16),
                        pltpu.VMEM((nch, SSD_ROWS, SSD_CHUNK), F32), pltpu.VMEM((nch, SSD_ROWS, SSD_CHUNK), F32),
                        pltpu.VMEM((nch, ns, gw), BF16), pltpu.VMEM((nch, ns, gw), F32),
                        pltpu.VMEM((nch, 8, gw), F32),
                        pltpu.VMEM((ns, gw), F32), pltpu.VMEM((ns, gw), F32)],
        compiler_params=_params(("parallel", "parallel")),
        name="ssd",
    )(slab, slab, slab, slab, dt, slab_ctx, slab_ctx, slab_ctx, dt_ctx,
      cwx, cwb, cwc, cbx, cbb, cbc, dtb, alog, dskip, nrm, sel_v, sel_w, sel_e)


def _rms(v, gain):
    return v * lax.rsqrt(jnp.mean(v * v, axis=-1, keepdims=True) + NORM_EPS) * gain


def _merge_kernel(ona_ref, y_ref, gna_ref, gssd_ref, x_ref, ga1_ref, sh2_ref, sc2_ref,
                  wna_ref, wssd_ref, wout_ref, npost_ref, npre_ref, wrh_ref, wrl_ref,
                  x1_ref, h2_ref, aff_ref):
    a = jnp.dot(ona_ref[0], wna_ref[...], preferred_element_type=F32)
    s = jnp.dot(y_ref[0], wssd_ref[...], preferred_element_type=F32)
    u = jax.nn.sigmoid(gna_ref[0].astype(F32)) * a + jax.nn.sigmoid(gssd_ref[0].astype(F32)) * s
    mix = jnp.dot(u.astype(BF16), wout_ref[...], preferred_element_type=F32)
    x1 = x_ref[0] + ga1_ref[0] * _rms(mix, npost_ref[...])
    x1_ref[0] = x1
    h2 = _rms(x1, npre_ref[...]) * (1.0 + sc2_ref[0]) + sh2_ref[0]
    hi = h2.astype(BF16)
    h2_ref[0] = hi
    lo = (h2 - hi.astype(F32)).astype(BF16)
    nt = (((1,), (1,)), ((), ()))
    wrh, wrl = wrh_ref[...], wrl_ref[...]
    logits = (lax.dot_general(wrh, hi, nt, preferred_element_type=F32)
              + lax.dot_general(wrh, lo, nt, preferred_element_type=F32)
              + lax.dot_general(wrl, hi, nt, preferred_element_type=F32))
    e = jnp.exp(logits - jnp.max(logits, axis=0, keepdims=True))
    aff_ref[0] = e / jnp.sum(e, axis=0, keepdims=True)


def _merge(o_na, y_ssd, slab, x, mod3, w_na, w_ssd, w_o, n_post, n_pre, wr_hi, wr_lo):
    bsz, seq, d = x.shape
    tm = 512
    ne = wr_hi.shape[0]
    tok = lambda width, blk: pl.BlockSpec((1, tm, width), lambda b, i: (b, i, blk))
    modv = lambda k: pl.BlockSpec((1, 1, d), lambda b, i: (b, 0, k))
    full = lambda r, c_: pl.BlockSpec((r, c_), lambda b, i: (0, 0))
    return pl.pallas_call(
        _merge_kernel,
        grid=(bsz, seq // tm),
        in_specs=[tok(d, 0), tok(SSD_WIDTH, 0), tok(d, COL_G // d), tok(d, COL_G // d + 1), tok(d, 0),
                  modv(2), modv(3), modv(4),
                  full(d, d), full(SSD_WIDTH, d), full(d, d), full(1, d), full(1, d), full(ne, d), full(ne, d)],
        out_specs=[tok(d, 0), tok(d, 0), pl.BlockSpec((1, ne, tm), lambda b, i: (b, 0, i))],
        out_shape=[jax.ShapeDtypeStruct((bsz, seq, d), F32), jax.ShapeDtypeStruct((bsz, seq, d), BF16),
                   jax.ShapeDtypeStruct((bsz, ne, seq), F32)],
        compiler_params=_params(("parallel", "parallel")),
        name="merge",
    )(o_na, y_ssd, slab, slab, x, mod3, mod3, mod3, w_na, w_ssd, w_o, n_post, n_pre, wr_hi, wr_lo)


def _prefix_count(mask_bf16, strict_upper):
    r, s = mask_bf16.shape
    offset = jnp.zeros((r, 1), F32)
    parts = []
    for j in range(s // LANES):
        seg = mask_bf16[:, j * LANES:(j + 1) * LANES]
        parts.append(jnp.dot(seg, strict_upper, preferred_element_type=F32) + offset)
        offset = offset + jnp.sum(seg.astype(F32), axis=1, keepdims=True)
    return jnp.concatenate(parts, axis=1)


def _route_kernel(aff_ref, h2_ref, pos_ref, xg_ref, gate_ref, *, cap):
    aff = aff_ref[0]
    bits = pltpu.bitcast(aff, jnp.int32)
    ne, t = aff.shape
    thr = jnp.zeros((ne, 1), jnp.int32)
    for bit in range(30, -1, -1):
        cand = thr | (1 << bit)
        cnt = jnp.sum(jnp.where(bits >= cand, 1.0, 0.0), axis=1, keepdims=True)
        thr = jnp.where(cnt >= cap, cand, thr)
    above = bits > thr
    tied = bits == thr
    need = cap - jnp.sum(jnp.where(above, 1.0, 0.0), axis=1, keepdims=True)
    ri = lax.broadcasted_iota(jnp.int32, (LANES, LANES), 0)
    ci = lax.broadcasted_iota(jnp.int32, (LANES, LANES), 1)
    strict_upper = jnp.where(ri < ci, 1.0, 0.0).astype(BF16)
    tie_rank = _prefix_count(jnp.where(tied, 1.0, 0.0).astype(BF16), strict_upper)
    sel = above | (tied & (tie_rank < need))
    slot = _prefix_count(jnp.where(sel, 1.0, 0.0).astype(BF16), strict_upper)
    slot = jnp.where(sel, slot, -1.0)
    pos = slot.astype(jnp.int32)
    padded = jnp.concatenate([slot, jnp.full((LANES - ne, t), -1.0, F32)], axis=0)
    for j in range(t // LANES):
        pos_ref[0, j * LANES:(j + 1) * LANES, :] = padded[:, j * LANES:(j + 1) * LANES].T.astype(jnp.int32)
    h2 = h2_ref[0]
    slot_id = lax.broadcasted_iota(jnp.int32, (cap, t), 0)
    for e in range(ne):
        hit = slot_id == pos[e:e + 1, :]
        onehot = jnp.where(hit, 1.0, 0.0).astype(BF16)
        xg_ref[e] = jnp.dot(onehot, h2, preferred_element_type=F32).astype(xg_ref.dtype)
        gate_ref[e] = jnp.sum(jnp.where(hit, aff[e:e + 1, :], 0.0), axis=1, keepdims=True)


def _route(aff_t, h2):
    bsz, ne, t = aff_t.shape
    d = h2.shape[2]
    cap = EC_CAPACITY_FACTOR * t // ne
    return pl.pallas_call(
        functools.partial(_route_kernel, cap=cap),
        grid=(bsz,),
        in_specs=[pl.BlockSpec((1, ne, t), lambda b: (b, 0, 0)),
                  pl.BlockSpec((1, t, d), lambda b: (b, 0, 0))],
        out_specs=[pl.BlockSpec((1, t, LANES), lambda b: (b, 0, 0)),
                   pl.BlockSpec((ne, cap, d), lambda b: (0, b, 0)),
                   pl.BlockSpec((ne, cap, 1), lambda b: (0, b, 0))],
        out_shape=[jax.ShapeDtypeStruct((bsz, t, LANES), jnp.int32),
                   jax.ShapeDtypeStruct((ne, bsz * cap, d), BF16),
                   jax.ShapeDtypeStruct((ne, bsz * cap, 1), F32)],
        compiler_params=_params(("parallel",)),
        name="route",
    )(aff_t, h2)


EXPERT_FF_TILE = 512
EXPERT_ROW_TILE = 512


def _expert_kernel(xg_ref, gate_ref, wg_ref, wu_ref, wd_ref, yo_ref, acc_ref):
    f = pl.program_id(1)

    @pl.when(f == 0)
    def _():
        acc_ref[...] = jnp.zeros(acc_ref.shape, F32)

    wg, wu, wd = wg_ref[0].astype(BF16), wu_ref[0].astype(BF16), wd_ref[0].astype(BF16)
    n = xg_ref.shape[1]
    rt = min(EXPERT_ROW_TILE, n)
    for r in range(n // rt):
        rows = pl.ds(r * rt, rt)
        xg = xg_ref[0, rows, :]
        hid = _silu(jnp.dot(xg, wg, preferred_element_type=F32)) * jnp.dot(xg, wu, preferred_element_type=F32)
        acc_ref[rows, :] += jnp.dot(hid.astype(BF16), wd, preferred_element_type=F32)

    @pl.when(f == pl.num_programs(1) - 1)
    def _():
        cap = yo_ref.shape[1]
        for b in range(yo_ref.shape[0]):
            rows = pl.ds(b * cap, cap)
            yo_ref[b] = (acc_ref[rows, :] * gate_ref[0, rows, :]).astype(yo_ref.dtype)


def _experts(xg, gate, w_g, w_u, w_d, bsz):
    ne, n, d = xg.shape
    cap = n // bsz
    ff = w_g.shape[2]
    tf = EXPERT_FF_TILE
    return pl.pallas_call(
        _expert_kernel,
        grid=(ne, ff // tf),
        in_specs=[pl.BlockSpec((1, n, d), lambda e, f: (e, 0, 0)),
                  pl.BlockSpec((1, n, 1), lambda e, f: (e, 0, 0)),
                  pl.BlockSpec((1, d, tf), lambda e, f: (e, 0, f)),
                  pl.BlockSpec((1, d, tf), lambda e, f: (e, 0, f)),
                  pl.BlockSpec((1, tf, d), lambda e, f: (e, f, 0))],
        out_specs=pl.BlockSpec((bsz, cap, d), lambda e, f: (0, e, 0)),
        out_shape=jax.ShapeDtypeStruct((bsz, ne * cap, d), BF16),
        scratch_shapes=[pltpu.VMEM((n, d), F32)],
        compiler_params=_params(("parallel", "arbitrary")),
        name="experts",
    )(xg, gate, w_g, w_u, w_d)


def _combine_kernel(pos_ref, yo_ref, x1_ref, ga2_ref, npost_ref, o_ref, *, cap, ne):
    pos = pos_ref[0]
    slot_id = lax.broadcasted_iota(jnp.int32, (1, cap), 1)
    onehot = jnp.concatenate(
        [jnp.where(pos[:, e:e + 1] == slot_id, 1.0, 0.0).astype(BF16) for e in range(ne)], axis=1)
    moe = jnp.dot(onehot, yo_ref[0], preferred_element_type=F32)
    o_ref[0] = x1_ref[0] + ga2_ref[0] * _rms(moe, npost_ref[...])


def _combine(pos_t, yo, x1, mod3, n_post, ne):
    bsz, seq, d = x1.shape
    cap = yo.shape[1] // ne
    tm = 512
    return pl.pallas_call(
        functools.partial(_combine_kernel, cap=cap, ne=ne),
        grid=(bsz, seq // tm),
        in_specs=[pl.BlockSpec((1, tm, LANES), lambda b, i: (b, i, 0)),
                  pl.BlockSpec((1, ne * cap, d), lambda b, i: (b, 0, 0)),
                  pl.BlockSpec((1, tm, d), lambda b, i: (b, i, 0)),
                  pl.BlockSpec((1, 1, d), lambda b, i: (b, 0, 5)),
                  pl.BlockSpec((1, d), lambda b, i: (0, 0))],
        out_specs=pl.BlockSpec((1, tm, d), lambda b, i: (b, i, 0)),
        out_shape=jax.ShapeDtypeStruct((bsz, seq, d), F32),
        compiler_params=_params(("parallel", "parallel")),
        name="combine",
    )(pos_t, yo, x1, mod3, n_post)


def kernel(x, c, ctx, c_ctx, w_ada, b_ada, norm_pre_mix, norm_post_mix, norm_pre_ffn, norm_post_ffn,
           w_in, na_rpb, ssd_conv_w, ssd_conv_b, ssd_a_log_fwd, ssd_a_log_bwd, ssd_dt_bias_fwd,
           ssd_dt_bias_bwd, ssd_d_skip, ssd_norm, w_branch_na, w_branch_ssd, w_out, w_router,
           w_exp_gate, w_exp_up, w_exp_down):
    mod, slab, o_na, y_ssd = _front(x, c, ctx, c_ctx, w_ada[0], b_ada[0], norm_pre_mix[0], w_in[0], na_rpb[0],
                                    ssd_conv_w[0], ssd_conv_b[0], ssd_a_log_fwd[0], ssd_a_log_bwd[0],
                                    ssd_dt_bias_fwd[0], ssd_dt_bias_bwd[0], ssd_d_skip[0], ssd_norm[0])
    return _back(x, mod, slab, o_na, y_ssd, norm_post_mix[0], norm_pre_ffn[0], norm_post_ffn[0],
                 w_branch_na[0], w_branch_ssd[0], w_out[0], w_router[0], w_exp_gate[0], w_exp_up[0], w_exp_down[0])


def _back(x, mod, slab, o_na, y_ssd, norm_post_mix, norm_pre_ffn, norm_post_ffn,
          w_branch_na, w_branch_ssd, w_out, w_router, w_exp_gate, w_exp_up, w_exp_down):
    bsz, seq, d = x.shape
    mod3 = mod.reshape(16, 1, N_MOD * d)
    wr_t = w_router.T
    wr_hi = wr_t.astype(BF16)
    wr_lo = (wr_t - wr_hi.astype(F32)).astype(BF16)
    x1, h2, aff_t = _merge(o_na, y_ssd, slab, x, mod3, w_branch_na.astype(BF16), w_branch_ssd.astype(BF16),
                           w_out.astype(BF16), norm_post_mix.reshape(1, d), norm_pre_ffn.reshape(1, d),
                           wr_hi, wr_lo)
    pos_t, xg, gate = _route(aff_t, h2)
    yo = _experts(xg, gate, w_exp_gate, w_exp_up, w_exp_down, bsz)
    return _combine(pos_t, yo, x1, mod3, norm_post_ffn.reshape(1, d), w_router.shape[1])


def _group_rows(fwd, bwd):
    rows = []
    for g in range(SSD_GROUPS):
        sl = slice(g * SSD_GHEADS, (g + 1) * SSD_GHEADS)
        rows += [fwd[sl], bwd[sl]]
    return jnp.concatenate(rows, axis=0)


def _front(x, c, ctx, c_ctx, w_ada, b_ada, norm_pre_mix, w_in, na_rpb, ssd_conv_w, ssd_conv_b,
           ssd_a_log_fwd, ssd_a_log_bwd, ssd_dt_bias_fwd, ssd_dt_bias_bwd, ssd_d_skip, ssd_norm):
    bsz, seq, d = x.shape
    tctx = ctx.shape[1]
    assert bsz <= 8
    cc =jnp.zeros((16, d), F32).at[:bsz].set(c).at[8].set(c_ctx)
    mod = _ada(cc, w_ada, b_ada)
    mod3 = mod.reshape(16, 1, N_MOD * d)

    dt0 = COL_G
    w_main = w_in.astype(BF16)
    w_gate = w_in[:, dt0 + 2 * SSD_HEADS:].astype(BF16)
    main_tiles = dt0 // 1024
    w_dt = _group_rows(w_in[:, dt0:dt0 + SSD_HEADS].T, w_in[:, dt0 + SSD_HEADS:dt0 + 2 * SSD_HEADS].T).astype(BF16)
    rope = _rope_tables(seq)
    g_pre = norm_pre_mix.reshape(1, d)

    slab, dt = _inproj(x, mod3, 0, g_pre, w_main, w_gate, w_dt, rope, 2, main_tiles, LAT_COLS // 1024, lambda j: j)
    nctx = bsz * tctx
    ctx_rows = -(-nctx // 1024) * 1024
    ctx_flat = ctx.reshape(1, nctx, d)
    if ctx_rows != nctx:
        ctx_flat = jnp.pad(ctx_flat, ((0, 0), (0, ctx_rows - nctx), (0, 0)))
    ctx_tile = lambda j: jnp.where(j < 2, j + COL_K // 1024, j - 2 + COL_X // 1024)
    ctx_tiles = CTX_COLS // 1024
    slab_c, dt_c = _inproj(ctx_flat, mod3, 8, g_pre, w_main, w_gate, w_dt, rope, 0, ctx_tiles, ctx_tiles, ctx_tile)
    slab_c = slab_c[0, :nctx].reshape(bsz, tctx, CTX_COLS)
    dt_c = jnp.swapaxes(dt_c[0, :, :nctx].reshape(-1, bsz, tctx), 0, 1)

    o_na = _na(slab, slab_c, _na_bias(na_rpb, seq // GRID_W))

    nx, nb = SSD_WIDTH, SSD_GROUPS * SSD_STATE
    cw = (ssd_conv_w[:, :nx], ssd_conv_w[:, nx:nx + nb], ssd_conv_w[:, nx + nb:])
    cb = tuple(v.reshape(1, -1) for v in (ssd_conv_b[:nx], ssd_conv_b[nx:nx + nb], ssd_conv_b[nx + nb:]))
    lanes = lambda v: jnp.broadcast_to(v[:, None], (v.shape[0], LANES))
    dtb = _group_rows(lanes(ssd_dt_bias_fwd), lanes(ssd_dt_bias_bwd))
    alog = _group_rows(lanes(ssd_a_log_fwd), lanes(ssd_a_log_bwd))
    dskip = jnp.repeat(ssd_d_skip, SSD_HEAD_DIM).reshape(1, -1)
    y_ssd = _ssd(slab, dt, slab_c, dt_c, cw, cb, dtb, alog, dskip, ssd_norm.reshape(1, -1))
    return mod, slab, o_na, y_ssd
```

```python
import functools
import math

import jax
import jax.numpy as jnp
import numpy as np
from jax import lax
from jax.experimental import pallas as pl
from jax.experimental.pallas import tpu as pltpu

F32 = jnp.float32
BF16 = jnp.bfloat16

D_MODEL = 1024
GRID_W = 64
NORM_EPS = 1e-6
N_MOD = 6
NA_HEADS = 16
NA_HEAD_DIM = 64
NA_KH = 8
NA_KW = 16
ROPE_BASE = 10000.0
SSD_WIDTH = 2048
SSD_HEAD_DIM = 64
SSD_HEADS = 32
SSD_GROUPS = 4
SSD_STATE = 128
SSD_CONV = 5
SSD_CHUNK = 128
N_EXPERTS = 16
EXPERT_FF = 2048
EC_CAPACITY_FACTOR = 2

VMEM_LIMIT_BYTES = 56 * 1024 * 1024
LANES = 128

COL_Q, COL_K, COL_V, COL_Z, COL_X, COL_B, COL_C, COL_G = 0, 1024, 2048, 3072, 5120, 7168, 7680, 8192
LAT_COLS = 10240
CTX_K, CTX_V, CTX_X, CTX_B, CTX_C = 0, 1024, 2048, 4096, 4608
CTX_COLS = 5120


def _params(semantics):
    return pltpu.CompilerParams(dimension_semantics=semantics, vmem_limit_bytes=VMEM_LIMIT_BYTES)


def _silu(v):
    return v * jax.nn.sigmoid(v)


def _softplus(v):
    return jnp.maximum(v, 0.0) + jnp.log1p(jnp.exp(-jnp.abs(v)))


def _ada_kernel(c_ref, w_ref, b_ref, o_ref):
    o_ref[...] = jnp.dot(_silu(c_ref[...]), w_ref[...], preferred_element_type=F32) + b_ref[...]


def _ada(cc, w_ada, b_ada):
    rows, d = cc.shape
    n = w_ada.shape[1]
    tn = 1536
    return pl.pallas_call(
        _ada_kernel,
        grid=(n // tn,),
        in_specs=[pl.BlockSpec((rows, d), lambda j: (0, 0)),
                  pl.BlockSpec((d, tn), lambda j: (0, j)),
                  pl.BlockSpec((1, tn), lambda j: (0, j))],
        out_specs=pl.BlockSpec((rows, tn), lambda j: (0, j)),
        out_shape=jax.ShapeDtypeStruct((rows, n), F32),
        compiler_params=_params(("arbitrary",)),
        name="ada",
    )(cc, w_ada, b_ada.reshape(1, n))


def _rope_tables(seq):
    lane = np.arange(LANES)
    axis = (lane % NA_HEAD_DIM) // 32
    within = lane % 32
    half = 16
    inv_freq = ROPE_BASE ** (-(within % half).astype(np.float64) / half)
    pos = np.arange(seq)
    coord = np.where(axis[None, :] == 0, (pos // GRID_W)[:, None], (pos % GRID_W)[:, None])
    ang = coord.astype(np.float32) * inv_freq.astype(np.float32)[None, :]
    cos, sin = jnp.cos(jnp.asarray(ang, F32)), jnp.sin(jnp.asarray(ang, F32))
    first = jnp.asarray(within < half)[None, :]
    return cos, jnp.where(first, -sin, 0.0), jnp.where(first, 0.0, sin)


def _inproj_kernel(x_ref, sh_ref, sc_ref, g_ref, w_ref, wgate_ref, wdt_ref, cos_ref, s1_ref, s2_ref,
                   o_ref, dt_ref, h_scr, *, rope_tiles, main_tiles):
    j = pl.program_id(2)

    @pl.when(j == 0)
    def _():
        x = x_ref[0]
        xn = x * lax.rsqrt(jnp.mean(x * x, axis=-1, keepdims=True) + NORM_EPS) * g_ref[...]
        h = (xn * (1.0 + sc_ref[0]) + sh_ref[0]).astype(BF16)
        h_scr[...] = h
        dt_ref[0] = lax.dot_general(wdt_ref[...], h, (((1,), (1,)), ((), ())), preferred_element_type=F32)

    h = h_scr[...]
    chunks = range(0, o_ref.shape[2], INPROJ_CHUNK)

    def plain(weights):
        for c in chunks:
            acc = jnp.dot(h, weights[:, c:c + INPROJ_CHUNK], preferred_element_type=F32)
            o_ref[0, :, c:c + INPROJ_CHUNK] = acc.astype(o_ref.dtype)

    pl.when((j >= rope_tiles) & (j < main_tiles))(lambda: plain(w_ref))
    pl.when(j >= main_tiles)(lambda: plain(wgate_ref))

    if rope_tiles:
        @pl.when(j < rope_tiles)
        def _():
            scale = jnp.where(j == 0, NA_HEAD_DIM ** -0.5, 1.0).astype(F32)
            cos, s1, s2 = cos_ref[...] * scale, s1_ref[...] * scale, s2_ref[...] * scale
            for c in chunks:
                acc = jnp.dot(h, w_ref[:, c:c + INPROJ_CHUNK], preferred_element_type=F32)
                for l in range(0, INPROJ_CHUNK, LANES):
                    t = acc[:, l:l + LANES]
                    r = t * cos + pltpu.roll(t, LANES - 16, 1) * s1 + pltpu.roll(t, 16, 1) * s2
                    o_ref[0, :, c + l:c + l + LANES] = r.astype(o_ref.dtype)


INPROJ_CHUNK = 256


def _inproj(x, mod3, mod_row0, g_pre, w, w_gate, w_dt, rope, rope_tiles, main_tiles, n_tiles, w_tile):
    bsz, seq, d = x.shape
    ndt = w_dt.shape[0]
    tm, tn = 1024, 1024
    n = n_tiles * tn
    cos, s1, s2 = rope
    kern = functools.partial(_inproj_kernel, rope_tiles=rope_tiles, main_tiles=main_tiles)
    return pl.pallas_call(
        kern,
        grid=(bsz, seq // tm, n_tiles),
        in_specs=[pl.BlockSpec((1, tm, d), lambda b, i, j: (b, i, 0)),
                  pl.BlockSpec((1, 1, d), lambda b, i, j: (b + mod_row0, 0, 0)),
                  pl.BlockSpec((1, 1, d), lambda b, i, j: (b + mod_row0, 0, 1)),
                  pl.BlockSpec((1, d), lambda b, i, j: (0, 0)),
                  pl.BlockSpec((d, tn), lambda b, i, j: (0, w_tile(jnp.minimum(j, main_tiles - 1)))),
                  pl.BlockSpec((d, tn), lambda b, i, j: (0, jnp.maximum(j - main_tiles, 0))),
                  pl.BlockSpec((ndt, d), lambda b, i, j: (0, 0)),
                  pl.BlockSpec((tm, LANES), lambda b, i, j: (i, 0)),
                  pl.BlockSpec((tm, LANES), lambda b, i, j: (i, 0)),
                  pl.BlockSpec((tm, LANES), lambda b, i, j: (i, 0))],
        out_specs=[pl.BlockSpec((1, tm, tn), lambda b, i, j: (b, i, j)),
                   pl.BlockSpec((1, ndt, tm), lambda b, i, j: (b, 0, i))],
        out_shape=[jax.ShapeDtypeStruct((bsz, seq, n), BF16),
                   jax.ShapeDtypeStruct((bsz, ndt, seq), F32)],
        scratch_shapes=[pltpu.VMEM((tm, d), BF16)],
        compiler_params=_params(("parallel", "parallel", "arbitrary")),
        name="inproj",
    )(x, mod3, mod3, g_pre, w, w_gate, w_dt, cos, s1, s2)


NA_QROWS = 4
NA_KROWS = 12
NA_QBLK = NA_QROWS * GRID_W
NA_KBLK = NA_KROWS * GRID_W
NA_MASKED = -1e30


def _na_key_start(g, rows):
    return min(max(NA_QROWS * g - NA_KH // 2, 0), rows - NA_KROWS)


NA_TABLE_ROWS = 2 * NA_KH - 2 + 2 * NA_QROWS


def _na_pair_table(rpb):
    col = np.arange(GRID_W)
    col_start = np.clip(col - NA_KW // 2, 0, GRID_W - NA_KW)
    col_ok = (col[None, :] >= col_start[:, None]) & (col[None, :] < col_start[:, None] + NA_KW)
    col_idx = np.clip(col[None, :] - col[:, None] + NA_KW - 1, 0, 2 * NA_KW - 2)
    onehot = ((col_idx[None] == np.arange(2 * NA_KW - 1)[:, None, None]) & col_ok[None]).astype(np.float32)
    toe = jnp.einsum("hrc,cqk->hrqk", rpb, jnp.asarray(onehot), precision=lax.Precision.HIGHEST)
    toe = toe + jnp.asarray(np.where(col_ok, 0.0, NA_MASKED), F32)
    toe = jnp.pad(toe, ((0, 0), (NA_QROWS, NA_QROWS), (0, 0), (0, 0)))
    return jnp.concatenate([toe[:, :-1], toe[:, 1:]], axis=-1)


def _na_block_bias(table_ref, hh, g, rows):
    r = NA_QROWS * g + np.arange(NA_QROWS)
    kr = _na_key_start(g, rows) + np.arange(NA_KROWS)
    r0 = np.clip(r - NA_KH // 2, 0, rows - NA_KH)
    row_ok = (kr[None, :] >= r0[:, None]) & (kr[None, :] < r0[:, None] + NA_KH)
    first = int(kr[0] - r[0]) + NA_KH - 1 + NA_QROWS
    left = lax.broadcasted_iota(jnp.int32, (1, LANES), 1) < GRID_W
    masked = jnp.full((GRID_W, LANES), NA_MASKED, F32)
    out = []
    for dr in range(NA_QROWS):
        tiles = []
        for j in range(NA_KROWS // 2):
            ok0, ok1 = bool(row_ok[dr, 2 * j]), bool(row_ok[dr, 2 * j + 1])
            if not (ok0 or ok1):
                tiles.append(masked)
                continue
            t = table_ref[hh, first - dr + 2 * j]
            tiles.append(t if ok0 and ok1 else jnp.where(left == ok0, t, NA_MASKED))
        out.append(jnp.concatenate(tiles, axis=1))
    return jnp.concatenate(out, axis=0)


def _na_kernel(q_ref, k_ref, v_ref, kc_ref, vc_ref, table_ref, o_ref, s_scr, *, rows):
    nblk = rows // NA_QROWS
    first_head = lax.broadcasted_iota(jnp.int32, (1, LANES), 1) < NA_HEAD_DIM
    kc, vc = kc_ref[0], vc_ref[0]
    nt = (((1,), (1,)), ((), ()))
    ones = jnp.ones((1, LANES), BF16)
    units = [(g, hh) for g in range(nblk) for hh in range(2)]

    def scores(unit, slot):
        g, hh = unit
        k0 = _na_key_start(g, rows) * GRID_W
        q = q_ref[0, g * NA_QBLK:(g + 1) * NA_QBLK, :]
        qm = jnp.where(first_head if hh == 0 else ~first_head, q, jnp.zeros_like(q))
        s_scr[slot, :, :NA_KBLK] = (lax.dot_general(qm, k_ref[0, k0:k0 + NA_KBLK, :], nt, preferred_element_type=F32)
                                    + _na_block_bias(table_ref, hh, g, rows))
        s_scr[slot, :, NA_KBLK:] = lax.dot_general(qm, kc, nt, preferred_element_type=F32)

    scores(units[0], 0)
    prev = None
    for t, (g, hh) in enumerate(units):
        if t + 1 < len(units):
            scores(units[t + 1], (t + 1) % 2)
        mine = first_head if hh == 0 else ~first_head
        k0 = _na_key_start(g, rows) * GRID_W
        s = s_scr[t % 2]
        p = jnp.exp(s - jnp.max(s, axis=-1, keepdims=True)).astype(BF16)
        o = (jnp.dot(p[:, :NA_KBLK], jnp.where(mine, v_ref[0, k0:k0 + NA_KBLK, :], ones), preferred_element_type=F32)
             + jnp.dot(p[:, NA_KBLK:], jnp.where(mine, vc, ones), preferred_element_type=F32))
        o = o / pltpu.roll(o, NA_HEAD_DIM, 1)
        if hh == 0:
            prev = o
        else:
            o_ref[0, g * NA_QBLK:(g + 1) * NA_QBLK, :] = jnp.where(first_head, prev, o).astype(o_ref.dtype)


def _na(slab, slab_ctx, table):
    bsz, seq, _ = slab.shape
    tctx = slab_ctx.shape[1]
    pairs = NA_HEADS // 2
    blk = lambda col0: (lambda hp, b: (b, 0, col0 // LANES + hp))
    return pl.pallas_call(
        functools.partial(_na_kernel, rows=seq // GRID_W),
        grid=(pairs, bsz),
        in_specs=[pl.BlockSpec((1, seq, LANES), blk(COL_Q)),
                  pl.BlockSpec((1, seq, LANES), blk(COL_K)),
                  pl.BlockSpec((1, seq, LANES), blk(COL_V)),
                  pl.BlockSpec((1, tctx, LANES), blk(CTX_K)),
                  pl.BlockSpec((1, tctx, LANES), blk(CTX_V)),
                  pl.BlockSpec((2, NA_TABLE_ROWS, GRID_W, LANES), lambda hp, b: (hp, 0, 0, 0))],
        out_specs=pl.BlockSpec((1, seq, LANES), lambda hp, b: (b, 0, hp)),
        out_shape=jax.ShapeDtypeStruct((bsz, seq, NA_HEADS * NA_HEAD_DIM), BF16),
        scratch_shapes=[pltpu.VMEM((2, NA_QBLK, NA_KBLK + tctx), F32)],
        compiler_params=_params(("parallel", "parallel")),
        name="na",
    )(slab, slab, slab, slab_ctx, slab_ctx, table)


SSD_GHEADS = SSD_HEADS // SSD_GROUPS
SSD_GWIDTH = SSD_GHEADS * SSD_HEAD_DIM
CONV_HALO = 8


SSD_ROWS = 2 * SSD_GHEADS
PACK_V, PACK_W, PACK_E = 0, 3 * SSD_ROWS, 5 * SSD_ROWS


def _ssd_selectors():
    k = np.arange(LANES)[:, None]
    row = k % SSD_ROWS

    def sel(first, terms, head_of_col):
        live = (k >= first) & (k < first + terms * SSD_ROWS)
        return jnp.asarray((live & (row == head_of_col[None, :])).astype(np.float32), BF16)

    col = np.arange(2 * SSD_GWIDTH)
    head_dir = np.where(col < SSD_GWIDTH, col // SSD_HEAD_DIM, SSD_GHEADS + (col - SSD_GWIDTH) // SSD_HEAD_DIM)
    sel_v = sel(PACK_V, 3, np.arange(SSD_ROWS * SSD_CHUNK) // SSD_CHUNK)
    return sel_v, sel(PACK_W, 2, head_dir), sel(PACK_E, 2, head_dir)


def _ssd_kernel(xs_ref, b_ref, c_ref, z_ref, dt_ref, xsc_ref, bc_ref, cc_ref, dtc_ref,
                cwx_ref, cwb_ref, cwc_ref, cbx_ref, cbb_ref, cbc_ref, dtb_ref, alog_ref, dskip_ref, nrm_ref,
                selv_ref, selw_ref, sele_ref,
                y_ref,
                padx, padb, padc, xs_s, bt_s, c_s, pack_s, vrow_s, dtrow_s, hprev_s, sb_s, decb_s, hf_s, hb_s,
                *, seq, tctx):
    Q = SSD_CHUNK
    GH = SSD_GHEADS
    GW = SSD_GWIDTH
    rows = lax.broadcasted_iota(jnp.int32, (Q, Q), 0)
    cols = lax.broadcasted_iota(jnp.int32, (Q, Q), 1)
    cum_rhs = jnp.concatenate([jnp.where(rows <= cols, 1.0, 0.0), jnp.ones((Q, Q), F32)], axis=1).astype(BF16)
    is_fwd = lax.broadcasted_iota(jnp.int32, (SSD_ROWS, 1), 0) < GH
    first_head = lax.broadcasted_iota(jnp.int32, (1, LANES), 1) < SSD_HEAD_DIM
    a_coef = -jnp.exp(alog_ref[...])
    dt_bias = dtb_ref[...]

    def bf_terms(v, n):
        out, rem = [], v
        for _ in range(n):
            t = rem.astype(BF16).astype(F32)
            out.append(t)
            rem = rem - t
        return out

    def prep(x_raw, b_raw, c_raw, dt_raw, n, store_prev):
        for pad, raw in ((padx, x_raw), (padb, b_raw), (padc, c_raw)):
            width = pad.shape[1]
            pad[0:CONV_HALO, :] = jnp.zeros((CONV_HALO, width), F32)
            pad[CONV_HALO + n:2 * CONV_HALO + n, :] = jnp.zeros((CONV_HALO, width), F32)

        def stage(c, carry):
            r0 = pl.multiple_of(c * Q, Q)
            for pad, raw in ((padx, x_raw), (padb, b_raw), (padc, c_raw)):
                pad[pl.ds(r0 + CONV_HALO, Q), :] = raw[0, pl.ds(r0, Q), :].astype(F32)
            return carry

        lax.fori_loop(0, n // Q, stage, 0)

        def conv(pad, w_ref, bias_ref, r0):
            first = CONV_HALO - SSD_CONV // 2
            tiles = []
            for lo in range(0, pad.shape[1], LANES):
                win = pad[pl.ds(r0, Q + 2 * CONV_HALO), lo:lo + LANES]
                acc = bias_ref[:, lo:lo + LANES] + win[first:first + Q, :] * w_ref[0:1, lo:lo + LANES]
                for k in range(1, SSD_CONV):
                    acc = acc + win[first + k:first + k + Q, :] * w_ref[k:k + 1, lo:lo + LANES]
                tiles.append(_silu(acc))
            return tiles[0] if len(tiles) == 1 else jnp.concatenate(tiles, axis=1)

        def body(c, carry):
            r0 = pl.multiple_of(c * Q, Q)
            dt = _softplus(dt_raw[0, :, pl.ds(r0, Q)] + dt_bias)
            a = dt * a_coef
            cs = jnp.dot(jnp.concatenate([t.astype(BF16) for t in bf_terms(a, 3)], axis=0), cum_rhs,
                         preferred_element_type=F32)
            cs = cs[0:SSD_ROWS] + cs[SSD_ROWS:2 * SSD_ROWS] + cs[2 * SSD_ROWS:3 * SSD_ROWS]
            acs, tot = cs[:, :Q], cs[:, Q:]
            ex = acs - a
            v = jnp.where(is_fwd, acs, ex)
            w = dt * jnp.exp(jnp.where(is_fwd, tot - acs, ex))
            e = jnp.exp(jnp.where(is_fwd, acs, tot - ex))
            vrow_s[c] = v
            dtrow_s[c] = dt
            packed = jnp.concatenate(bf_terms(v, 3) + bf_terms(w, 2) + bf_terms(e, 2)
                                     + [jnp.zeros((SSD_ROWS, Q), F32)], axis=0)
            pk = packed.T.astype(BF16)
            pack_s[pl.ds(r0, Q), :] = pk
            wexp = jnp.dot(pk, selw_ref[...], preferred_element_type=F32)
            edge = jnp.concatenate([pk[0:16, :], pk[Q - 16:Q, :]], axis=0)
            dec = jnp.dot(edge, sele_ref[...], preferred_element_type=F32)
            b_t = conv(padb, cwb_ref, cbb_ref, r0).T.astype(BF16)
            bt_s[c] = b_t
            c_s[pl.ds(r0, Q), :] = conv(padc, cwc_ref, cbc_ref, r0).astype(BF16)
            xs = conv(padx, cwx_ref, cbx_ref, r0)
            xs_s[pl.ds(r0, Q), :] = xs
            xdec = jnp.concatenate([xs * wexp[:, :GW], xs * wexp[:, GW:]], axis=1).astype(BF16)
            s = jnp.dot(b_t, xdec, preferred_element_type=F32)
            if store_prev:
                hprev_s[c] = hf_s[...].astype(BF16)
            hf_s[...] = dec[31:32, :GW] * hf_s[...] + s[:, :GW]
            sb_s[c] = s[:, GW:]
            decb_s[c] = jnp.broadcast_to(dec[0:1, GW:], (8, GW))
            return carry

        lax.fori_loop(0, n // Q, body, 0, unroll=2)

    def backward_step(c):
        hb_s[...] = decb_s[c][0:1, :] * hb_s[...] + sb_s[c]

    hf_s[...] = jnp.zeros(hf_s.shape, F32)
    hb_s[...] = jnp.zeros(hb_s.shape, F32)

    prep(xsc_ref, bc_ref, cc_ref, dtc_ref, tctx, False)
    nctx = tctx // Q

    def ctx_back(i, carry):
        backward_step(nctx - 1 - i)
        return carry

    lax.fori_loop(0, nctx, ctx_back, 0)

    prep(xs_ref, b_ref, c_ref, dt_ref, seq, True)
    nch = seq // Q
    dskip = dskip_ref[...]
    gain = nrm_ref[...]

    def ybody(i, carry):
        c = nch - 1 - i
        r0 = pl.multiple_of(c * Q, Q)
        pk = pack_s[pl.ds(r0, Q), :]
        vcol = jnp.dot(pk, selv_ref[...], preferred_element_type=F32)
        eexp = jnp.dot(pk, sele_ref[...], preferred_element_type=F32)
        v_t, dt_t = vrow_s[c], dtrow_s[c]
        xs = xs_s[pl.ds(r0, Q), :]
        xsb = xs.astype(BF16)
        cm = c_s[pl.ds(r0, Q), :]
        cb = jnp.dot(cm, bt_s[c], preferred_element_type=F32)
        y_off = (jnp.dot(cm, hprev_s[c], preferred_element_type=F32) * eexp[:, :GW]
                 + jnp.dot(cm, hb_s[...].astype(BF16), preferred_element_type=F32) * eexp[:, GW:])
        gate = _silu(z_ref[0, pl.ds(r0, Q), :].astype(F32))
        tiles = []
        for j in range(GH // 2):
            xpair = xsb[:, j * LANES:(j + 1) * LANES]
            ys = []
            for hh in range(2):
                hf, hb = 2 * j + hh, GH + 2 * j + hh
                arg_f = vcol[:, hf * Q:(hf + 1) * Q] - v_t[hf:hf + 1, :]
                arg_b = v_t[hb:hb + 1, :] - vcol[:, hb * Q:(hb + 1) * Q]
                e = jnp.exp(jnp.where(rows >= cols, arg_f, arg_b))
                dt_f, dt_b = dt_t[hf:hf + 1, :], dt_t[hb:hb + 1, :]
                fac = jnp.where(rows > cols, dt_f, jnp.where(rows < cols, dt_b, dt_f + dt_b))
                m = (cb * e * fac).astype(BF16)
                ys.append(jnp.dot(m, xpair, preferred_element_type=F32))
            tiles.append(jnp.where(first_head, ys[0], ys[1]))
        y = jnp.concatenate(tiles, axis=1) + y_off + dskip * xs
        u = y * gate
        u = u * lax.rsqrt(jnp.mean(u * u, axis=-1, keepdims=True) + NORM_EPS) * gain
        y_ref[0, pl.ds(r0, Q), :] = u.astype(y_ref.dtype)
        backward_step(c)
        return carry

    lax.fori_loop(0, nch, ybody, 0, unroll=2)


def _ssd(slab, dt, slab_ctx, dt_ctx, cw, cb, dtb, alog, dskip, nrm):
    bsz, seq, _ = slab.shape
    tctx = slab_ctx.shape[1]
    gw, ns = SSD_GWIDTH, SSD_STATE
    nch = seq // SSD_CHUNK
    cwx, cwb, cwc = cw
    cbx, cbb, cbc = cb

    def tok(width, col0, n):
        return pl.BlockSpec((1, n, width), lambda b, g: (b, 0, col0 // width + g))

    def par(rows, width):
        return pl.BlockSpec((rows, width), lambda b, g: (0, g))

    def head_rows(n):
        return pl.BlockSpec((1, SSD_ROWS, n), lambda b, g: (b, g, 0))

    def const(a):
        return pl.BlockSpec(a.shape, lambda b, g: (0, 0))

    sel_v, sel_w, sel_e = _ssd_selectors()
    pad = seq + 2 * CONV_HALO
    return pl.pallas_call(
        functools.partial(_ssd_kernel, seq=seq, tctx=tctx),
        grid=(bsz, SSD_GROUPS),
        in_specs=[tok(gw, COL_X, seq), tok(ns, COL_B, seq), tok(ns, COL_C, seq), tok(gw, COL_Z, seq),
                  head_rows(seq),
                  tok(gw, CTX_X, tctx), tok(ns, CTX_B, tctx), tok(ns, CTX_C, tctx), head_rows(tctx),
                  par(SSD_CONV, gw), par(SSD_CONV, ns), par(SSD_CONV, ns),
                  par(1, gw), par(1, ns), par(1, ns),
                  pl.BlockSpec((SSD_ROWS, LANES), lambda b, g: (g, 0)),
                  pl.BlockSpec((SSD_ROWS, LANES), lambda b, g: (g, 0)),
                  par(1, gw), par(1, gw), const(sel_v), const(sel_w), const(sel_e)],
        out_specs=pl.BlockSpec((1, seq, gw), lambda b, g: (b, 0, g)),
        out_shape=jax.ShapeDtypeStruct((bsz, seq, SSD_WIDTH), BF16),
        scratch_shapes=[pltpu.VMEM((pad, gw), F32), pltpu.VMEM((pad, ns), F32), pltpu.VMEM((pad, ns), F32),
                        pltpu.VMEM((seq, gw), F32), pltpu.VMEM((nch, ns, SSD_CHUNK), BF16),
                        pltpu.VMEM((seq, ns), BF16), pltpu.VMEM((seq, LANES), BF16),
                        pltpu.VMEM((nch, SSD_ROWS, SSD_CHUNK), F32), pltpu.VMEM((nch, SSD_ROWS, SSD_CHUNK), F32),
                        pltpu.VMEM((nch, ns, gw), BF16), pltpu.VMEM((nch, ns, gw), F32),
                        pltpu.VMEM((nch, 8, gw), F32),
                        pltpu.VMEM((ns, gw), F32), pltpu.VMEM((ns, gw), F32)],
        compiler_params=_params(("parallel", "parallel")),
        name="ssd",
    )(slab, slab, slab, slab, dt, slab_ctx, slab_ctx, slab_ctx, dt_ctx,
      cwx, cwb, cwc, cbx, cbb, cbc, dtb, alog, dskip, nrm, sel_v, sel_w, sel_e)


def _rms(v, gain):
    return v * lax.rsqrt(jnp.mean(v * v, axis=-1, keepdims=True) + NORM_EPS) * gain


def _merge_kernel(ona_ref, y_ref, gna_ref, gssd_ref, x_ref, ga1_ref, sh2_ref, sc2_ref,
                  wna_ref, wssd_ref, wout_ref, npost_ref, npre_ref, wrh_ref, wrl_ref,
                  x1_ref, h2_ref, aff_ref):
    a = jnp.dot(ona_ref[0], wna_ref[...], preferred_element_type=F32)
    s = jnp.dot(y_ref[0], wssd_ref[...], preferred_element_type=F32)
    u = jax.nn.sigmoid(gna_ref[0].astype(F32)) * a + jax.nn.sigmoid(gssd_ref[0].astype(F32)) * s
    mix = jnp.dot(u.astype(BF16), wout_ref[...], preferred_element_type=F32)
    x1 = x_ref[0] + ga1_ref[0] * _rms(mix, npost_ref[...])
    x1_ref[0] = x1
    h2 = _rms(x1, npre_ref[...]) * (1.0 + sc2_ref[0]) + sh2_ref[0]
    hi = h2.astype(BF16)
    h2_ref[0] = hi
    lo = (h2 - hi.astype(F32)).astype(BF16)
    nt = (((1,), (1,)), ((), ()))
    wrh, wrl = wrh_ref[...], wrl_ref[...]
    logits = (lax.dot_general(wrh, hi, nt, preferred_element_type=F32)
              + lax.dot_general(wrh, lo, nt, preferred_element_type=F32)
              + lax.dot_general(wrl, hi, nt, preferred_element_type=F32))
    e = jnp.exp(logits - jnp.max(logits, axis=0, keepdims=True))
    aff_ref[0] = e / jnp.sum(e, axis=0, keepdims=True)


def _merge(o_na, y_ssd, slab, x, mod3, w_na, w_ssd, w_o, n_post, n_pre, wr_hi, wr_lo):
    bsz, seq, d = x.shape
    tm = 512
    ne = wr_hi.shape[0]
    tok = lambda width, blk: pl.BlockSpec((1, tm, width), lambda b, i: (b, i, blk))
    modv = lambda k: pl.BlockSpec((1, 1, d), lambda b, i: (b, 0, k))
    full = lambda r, c_: pl.BlockSpec((r, c_), lambda b, i: (0, 0))
    return pl.pallas_call(
        _merge_kernel,
        grid=(bsz, seq // tm),
        in_specs=[tok(d, 0), tok(SSD_WIDTH, 0), tok(d, COL_G // d), tok(d, COL_G // d + 1), tok(d, 0),
                  modv(2), modv(3), modv(4),
                  full(d, d), full(SSD_WIDTH, d), full(d, d), full(1, d), full(1, d), full(ne, d), full(ne, d)],
        out_specs=[tok(d, 0), tok(d, 0), pl.BlockSpec((1, ne, tm), lambda b, i: (b, 0, i))],
        out_shape=[jax.ShapeDtypeStruct((bsz, seq, d), F32), jax.ShapeDtypeStruct((bsz, seq, d), BF16),
                   jax.ShapeDtypeStruct((bsz, ne, seq), F32)],
        compiler_params=_params(("parallel", "parallel")),
        name="merge",
    )(o_na, y_ssd, slab, slab, x, mod3, mod3, mod3, w_na, w_ssd, w_o, n_post, n_pre, wr_hi, wr_lo)


def _prefix_count(mask_bf16, strict_upper):
    r, s = mask_bf16.shape
    offset = jnp.zeros((r, 1), F32)
    parts = []
    for j in range(s // LANES):
        seg = mask_bf16[:, j * LANES:(j + 1) * LANES]
        parts.append(jnp.dot(seg, strict_upper, preferred_element_type=F32) + offset)
        offset = offset + jnp.sum(seg.astype(F32), axis=1, keepdims=True)
    return jnp.concatenate(parts, axis=1)


def _route_kernel(aff_ref, h2_ref, pos_ref, xg_ref, gate_ref, *, cap):
    aff = aff_ref[0]
    bits = pltpu.bitcast(aff, jnp.int32)
    ne, t = aff.shape
    thr = jnp.zeros((ne, 1), jnp.int32)
    for bit in range(30, -1, -1):
        cand = thr | (1 << bit)
        cnt = jnp.sum(jnp.where(bits >= cand, 1.0, 0.0), axis=1, keepdims=True)
        thr = jnp.where(cnt >= cap, cand, thr)
    above = bits > thr
    tied = bits == thr
    need = cap - jnp.sum(jnp.where(above, 1.0, 0.0), axis=1, keepdims=True)
    ri = lax.broadcasted_iota(jnp.int32, (LANES, LANES), 0)
    ci = lax.broadcasted_iota(jnp.int32, (LANES, LANES), 1)
    strict_upper = jnp.where(ri < ci, 1.0, 0.0).astype(BF16)
    tie_rank = _prefix_count(jnp.where(tied, 1.0, 0.0).astype(BF16), strict_upper)
    sel = above | (tied & (tie_rank < need))
    slot = _prefix_count(jnp.where(sel, 1.0, 0.0).astype(BF16), strict_upper)
    slot = jnp.where(sel, slot, -1.0)
    pos = slot.astype(jnp.int32)
    padded = jnp.concatenate([slot, jnp.full((LANES - ne, t), -1.0, F32)], axis=0)
    for j in range(t // LANES):
        pos_ref[0, j * LANES:(j + 1) * LANES, :] = padded[:, j * LANES:(j + 1) * LANES].T.astype(jnp.int32)
    h2 = h2_ref[0]
    slot_id = lax.broadcasted_iota(jnp.int32, (cap, t), 0)
    for e in range(ne):
        hit = slot_id == pos[e:e + 1, :]
        onehot = jnp.where(hit, 1.0, 0.0).astype(BF16)
        xg_ref[e] = jnp.dot(onehot, h2, preferred_element_type=F32).astype(xg_ref.dtype)
        gate_ref[e] = jnp.sum(jnp.where(hit, aff[e:e + 1, :], 0.0), axis=1, keepdims=True)


def _route(aff_t, h2):
    bsz, ne, t = aff_t.shape
    d = h2.shape[2]
    cap = EC_CAPACITY_FACTOR * t // ne
    return pl.pallas_call(
        functools.partial(_route_kernel, cap=cap),
        grid=(bsz,),
        in_specs=[pl.BlockSpec((1, ne, t), lambda b: (b, 0, 0)),
                  pl.BlockSpec((1, t, d), lambda b: (b, 0, 0))],
        out_specs=[pl.BlockSpec((1, t, LANES), lambda b: (b, 0, 0)),
                   pl.BlockSpec((ne, cap, d), lambda b: (0, b, 0)),
                   pl.BlockSpec((ne, cap, 1), lambda b: (0, b, 0))],
        out_shape=[jax.ShapeDtypeStruct((bsz, t, LANES), jnp.int32),
                   jax.ShapeDtypeStruct((ne, bsz * cap, d), BF16),
                   jax.ShapeDtypeStruct((ne, bsz * cap, 1), F32)],
        compiler_params=_params(("parallel",)),
        name="route",
    )(aff_t, h2)


EXPERT_FF_TILE = 512
EXPERT_ROW_TILE = 512


def _expert_kernel(xg_ref, gate_ref, wg_ref, wu_ref, wd_ref, yo_ref, acc_ref):
    f = pl.program_id(1)

    @pl.when(f == 0)
    def _():
        acc_ref[...] = jnp.zeros(acc_ref.shape, F32)

    wg, wu, wd = wg_ref[0].astype(BF16), wu_ref[0].astype(BF16), wd_ref[0].astype(BF16)
    n = xg_ref.shape[1]
    rt = min(EXPERT_ROW_TILE, n)
    for r in range(n // rt):
        rows = pl.ds(r * rt, rt)
        xg = xg_ref[0, rows, :]
        hid = _silu(jnp.dot(xg, wg, preferred_element_type=F32)) * jnp.dot(xg, wu, preferred_element_type=F32)
        acc_ref[rows, :] += jnp.dot(hid.astype(BF16), wd, preferred_element_type=F32)

    @pl.when(f == pl.num_programs(1) - 1)
    def _():
        cap = yo_ref.shape[1]
        for b in range(yo_ref.shape[0]):
            rows = pl.ds(b * cap, cap)
            yo_ref[b] = (acc_ref[rows, :] * gate_ref[0, rows, :]).astype(yo_ref.dtype)


def _experts(xg, gate, w_g, w_u, w_d, bsz):
    ne, n, d = xg.shape
    cap = n // bsz
    ff = w_g.shape[2]
    tf = EXPERT_FF_TILE
    return pl.pallas_call(
        _expert_kernel,
        grid=(ne, ff // tf),
        in_specs=[pl.BlockSpec((1, n, d), lambda e, f: (e, 0, 0)),
                  pl.BlockSpec((1, n, 1), lambda e, f: (e, 0, 0)),
                  pl.BlockSpec((1, d, tf), lambda e, f: (e, 0, f)),
                  pl.BlockSpec((1, d, tf), lambda e, f: (e, 0, f)),
                  pl.BlockSpec((1, tf, d), lambda e, f: (e, f, 0))],
        out_specs=pl.BlockSpec((bsz, cap, d), lambda e, f: (0, e, 0)),
        out_shape=jax.ShapeDtypeStruct((bsz, ne * cap, d), BF16),
        scratch_shapes=[pltpu.VMEM((n, d), F32)],
        compiler_params=_params(("parallel", "arbitrary")),
        name="experts",
    )(xg, gate, w_g, w_u, w_d)


def _combine_kernel(pos_ref, yo_ref, x1_ref, ga2_ref, npost_ref, o_ref, *, cap, ne):
    pos = pos_ref[0]
    slot_id = lax.broadcasted_iota(jnp.int32, (1, cap), 1)
    onehot = jnp.concatenate(
        [jnp.where(pos[:, e:e + 1] == slot_id, 1.0, 0.0).astype(BF16) for e in range(ne)], axis=1)
    moe = jnp.dot(onehot, yo_ref[0], preferred_element_type=F32)
    o_ref[0] = x1_ref[0] + ga2_ref[0] * _rms(moe, npost_ref[...])


def _combine(pos_t, yo, x1, mod3, n_post, ne):
    bsz, seq, d = x1.shape
    cap = yo.shape[1] // ne
    tm = 512
    return pl.pallas_call(
        functools.partial(_combine_kernel, cap=cap, ne=ne),
        grid=(bsz, seq // tm),
        in_specs=[pl.BlockSpec((1, tm, LANES), lambda b, i: (b, i, 0)),
                  pl.BlockSpec((1, ne * cap, d), lambda b, i: (b, 0, 0)),
                  pl.BlockSpec((1, tm, d), lambda b, i: (b, i, 0)),
                  pl.BlockSpec((1, 1, d), lambda b, i: (b, 0, 5)),
                  pl.BlockSpec((1, d), lambda b, i: (0, 0))],
        out_specs=pl.BlockSpec((1, tm, d), lambda b, i: (b, i, 0)),
        out_shape=jax.ShapeDtypeStruct((bsz, seq, d), F32),
        compiler_params=_params(("parallel", "parallel")),
        name="combine",
    )(pos_t, yo, x1, mod3, n_post)


def kernel(x, c, ctx, c_ctx, w_ada, b_ada, norm_pre_mix, norm_post_mix, norm_pre_ffn, norm_post_ffn,
           w_in, na_rpb, ssd_conv_w, ssd_conv_b, ssd_a_log_fwd, ssd_a_log_bwd, ssd_dt_bias_fwd,
           ssd_dt_bias_bwd, ssd_d_skip, ssd_norm, w_branch_na, w_branch_ssd, w_out, w_router,
           w_exp_gate, w_exp_up, w_exp_down):
    mod, slab, o_na, y_ssd = _front(x, c, ctx, c_ctx, w_ada[0], b_ada[0], norm_pre_mix[0], w_in[0], na_rpb[0],
                                    ssd_conv_w[0], ssd_conv_b[0], ssd_a_log_fwd[0], ssd_a_log_bwd[0],
                                    ssd_dt_bias_fwd[0], ssd_dt_bias_bwd[0], ssd_d_skip[0], ssd_norm[0])
    return _back(x, mod, slab, o_na, y_ssd, norm_post_mix[0], norm_pre_ffn[0], norm_post_ffn[0],
                 w_branch_na[0], w_branch_ssd[0], w_out[0], w_router[0], w_exp_gate[0], w_exp_up[0], w_exp_down[0])


def _back(x, mod, slab, o_na, y_ssd, norm_post_mix, norm_pre_ffn, norm_post_ffn,
          w_branch_na, w_branch_ssd, w_out, w_router, w_exp_gate, w_exp_up, w_exp_down):
    bsz, seq, d = x.shape
    mod3 = mod.reshape(16, 1, N_MOD * d)
    wr_t = w_router.T
    wr_hi = wr_t.astype(BF16)
    wr_lo = (wr_t - wr_hi.astype(F32)).astype(BF16)
    x1, h2, aff_t = _merge(o_na, y_ssd, slab, x, mod3, w_branch_na.astype(BF16), w_branch_ssd.astype(BF16),
                           w_out.astype(BF16), norm_post_mix.reshape(1, d), norm_pre_ffn.reshape(1, d),
                           wr_hi, wr_lo)
    pos_t, xg, gate = _route(aff_t, h2)
    yo = _experts(xg, gate, w_exp_gate, w_exp_up, w_exp_down, bsz)
    return _combine(pos_t, yo, x1, mod3, norm_post_ffn.reshape(1, d), w_router.shape[1])


def _group_rows(fwd, bwd):
    rows = []
    for g in range(SSD_GROUPS):
        sl = slice(g * SSD_GHEADS, (g + 1) * SSD_GHEADS)
        rows += [fwd[sl], bwd[sl]]
    return jnp.concatenate(rows, axis=0)


def _front(x, c, ctx, c_ctx, w_ada, b_ada, norm_pre_mix, w_in, na_rpb, ssd_conv_w, ssd_conv_b,
           ssd_a_log_fwd, ssd_a_log_bwd, ssd_dt_bias_fwd, ssd_dt_bias_bwd, ssd_d_skip, ssd_norm):
    bsz, seq, d = x.shape
    tctx = ctx.shape[1]
    assert bsz <= 8
    cc =jnp.zeros((16, d), F32).at[:bsz].set(c).at[8].set(c_ctx)
    mod = _ada(cc, w_ada, b_ada)
    mod3 = mod.reshape(16, 1, N_MOD * d)

    dt0 = COL_G
    w_main = w_in.astype(BF16)
    w_gate = w_main[:, dt0 + 2 * SSD_HEADS:]
    main_tiles = dt0 // 1024
    w_dt = _group_rows(w_in[:, dt0:dt0 + SSD_HEADS].T, w_in[:, dt0 + SSD_HEADS:dt0 + 2 * SSD_HEADS].T).astype(BF16)
    rope = _rope_tables(seq)
    g_pre = norm_pre_mix.reshape(1, d)

    slab, dt = _inproj(x, mod3, 0, g_pre, w_main, w_gate, w_dt, rope, 2, main_tiles, LAT_COLS // 1024, lambda j: j)
    nctx = bsz * tctx
    ctx_rows = -(-nctx // 1024) * 1024
    ctx_flat = ctx.reshape(1, nctx, d)
    if ctx_rows != nctx:
        ctx_flat = jnp.pad(ctx_flat, ((0, 0), (0, ctx_rows - nctx), (0, 0)))
    ctx_tile = lambda j: jnp.where(j < 2, j + COL_K // 1024, j - 2 + COL_X // 1024)
    ctx_tiles = CTX_COLS // 1024
    slab_c, dt_c = _inproj(ctx_flat, mod3, 8, g_pre, w_main, w_gate, w_dt, rope, 0, ctx_tiles, ctx_tiles, ctx_tile)
    slab_c = slab_c[0, :nctx].reshape(bsz, tctx, CTX_COLS)
    dt_c = jnp.swapaxes(dt_c[0, :, :nctx].reshape(-1, bsz, tctx), 0, 1)

    o_na = _na(slab, slab_c, _na_pair_table(na_rpb))

    nx, nb = SSD_WIDTH, SSD_GROUPS * SSD_STATE
    cw = (ssd_conv_w[:, :nx], ssd_conv_w[:, nx:nx + nb], ssd_conv_w[:, nx + nb:])
    cb = tuple(v.reshape(1, -1) for v in (ssd_conv_b[:nx], ssd_conv_b[nx:nx + nb], ssd_conv_b[nx + nb:]))
    lanes = lambda v: jnp.broadcast_to(v[:, None], (v.shape[0], LANES))
    dtb = _group_rows(lanes(ssd_dt_bias_fwd), lanes(ssd_dt_bias_bwd))
    alog = _group_rows(lanes(ssd_a_log_fwd), lanes(ssd_a_log_bwd))
    dskip = jnp.repeat(ssd_d_skip, SSD_HEAD_DIM).reshape(1, -1)
    y_ssd = _ssd(slab, dt, slab_c, dt_c, cw, cb, dtb, alog, dskip, ssd_norm.reshape(1, -1))
    return mod, slab, o_na, y_ssd
```

```python
import functools
import math

import jax
import jax.numpy as jnp
import numpy as np
from jax import lax
from jax.experimental import pallas as pl
from jax.experimental.pallas import tpu as pltpu

F32 = jnp.float32
BF16 = jnp.bfloat16

D_MODEL = 1024
GRID_W = 64
NORM_EPS = 1e-6
N_MOD = 6
NA_HEADS = 16
NA_HEAD_DIM = 64
NA_KH = 8
NA_KW = 16
ROPE_BASE = 10000.0
SSD_WIDTH = 2048
SSD_HEAD_DIM = 64
SSD_HEADS = 32
SSD_GROUPS = 4
SSD_STATE = 128
SSD_CONV = 5
SSD_CHUNK = 128
N_EXPERTS = 16
EXPERT_FF = 2048
EC_CAPACITY_FACTOR = 2

VMEM_LIMIT_BYTES = 56 * 1024 * 1024
LANES = 128

COL_Q, COL_K, COL_V, COL_Z, COL_X, COL_B, COL_C, COL_G = 0, 1024, 2048, 3072, 5120, 7168, 7680, 8192
LAT_COLS = 10240
CTX_K, CTX_V, CTX_X, CTX_B, CTX_C = 0, 1024, 2048, 4096, 4608
CTX_COLS = 5120


def _params(semantics):
    return pltpu.CompilerParams(dimension_semantics=semantics, vmem_limit_bytes=VMEM_LIMIT_BYTES)


def _silu(v):
    return v * jax.nn.sigmoid(v)


def _softplus(v):
    return jnp.maximum(v, 0.0) + jnp.log1p(jnp.exp(-jnp.abs(v)))


def _ada_kernel(c_ref, w_ref, b_ref, o_ref):
    o_ref[...] = jnp.dot(_silu(c_ref[...]), w_ref[...], preferred_element_type=F32) + b_ref[...]


def _ada(cc, w_ada, b_ada):
    rows, d = cc.shape
    n = w_ada.shape[1]
    tn = 1536
    return pl.pallas_call(
        _ada_kernel,
        grid=(n // tn,),
        in_specs=[pl.BlockSpec((rows, d), lambda j: (0, 0)),
                  pl.BlockSpec((d, tn), lambda j: (0, j)),
                  pl.BlockSpec((1, tn), lambda j: (0, j))],
        out_specs=pl.BlockSpec((rows, tn), lambda j: (0, j)),
        out_shape=jax.ShapeDtypeStruct((rows, n), F32),
        compiler_params=_params(("arbitrary",)),
        name="ada",
    )(cc, w_ada, b_ada.reshape(1, n))


def _rope_tables(seq):
    lane = np.arange(LANES)
    axis = (lane % NA_HEAD_DIM) // 32
    within = lane % 32
    half = 16
    inv_freq = ROPE_BASE ** (-(within % half).astype(np.float64) / half)
    pos = np.arange(seq)
    coord = np.where(axis[None, :] == 0, (pos // GRID_W)[:, None], (pos % GRID_W)[:, None])
    ang = coord.astype(np.float32) * inv_freq.astype(np.float32)[None, :]
    cos, sin = jnp.cos(jnp.asarray(ang, F32)), jnp.sin(jnp.asarray(ang, F32))
    first = jnp.asarray(within < half)[None, :]
    return cos, jnp.where(first, -sin, 0.0), jnp.where(first, 0.0, sin)


def _inproj_kernel(x_ref, sh_ref, sc_ref, g_ref, w_ref, wgate_ref, wdt_ref, cos_ref, s1_ref, s2_ref,
                   o_ref, dt_ref, h_scr, *, rope_tiles, main_tiles):
    j = pl.program_id(2)

    @pl.when(j == 0)
    def _():
        x = x_ref[0]
        xn = x * lax.rsqrt(jnp.mean(x * x, axis=-1, keepdims=True) + NORM_EPS) * g_ref[...]
        h = (xn * (1.0 + sc_ref[0]) + sh_ref[0]).astype(BF16)
        h_scr[...] = h
        dt_ref[0] = lax.dot_general(wdt_ref[...], h, (((1,), (1,)), ((), ())), preferred_element_type=F32)

    h = h_scr[...]
    chunks = range(0, o_ref.shape[2], INPROJ_CHUNK)

    def plain(weights):
        for c in chunks:
            acc = jnp.dot(h, weights[:, c:c + INPROJ_CHUNK], preferred_element_type=F32)
            o_ref[0, :, c:c + INPROJ_CHUNK] = acc.astype(o_ref.dtype)

    pl.when((j >= rope_tiles) & (j < main_tiles))(lambda: plain(w_ref))
    pl.when(j >= main_tiles)(lambda: plain(wgate_ref))

    if rope_tiles:
        @pl.when(j < rope_tiles)
        def _():
            scale = jnp.where(j == 0, NA_HEAD_DIM ** -0.5, 1.0).astype(F32)
            cos, s1, s2 = cos_ref[...] * scale, s1_ref[...] * scale, s2_ref[...] * scale
            for c in chunks:
                acc = jnp.dot(h, w_ref[:, c:c + INPROJ_CHUNK], preferred_element_type=F32)
                for l in range(0, INPROJ_CHUNK, LANES):
                    t = acc[:, l:l + LANES]
                    r = t * cos + pltpu.roll(t, LANES - 16, 1) * s1 + pltpu.roll(t, 16, 1) * s2
                    o_ref[0, :, c + l:c + l + LANES] = r.astype(o_ref.dtype)


INPROJ_CHUNK = 256
INPROJ_ROWS = 2048


def _inproj(x, mod3, mod_row0, g_pre, w, w_gate, w_dt, rope, rope_tiles, main_tiles, n_tiles, w_tile):
    bsz, seq, d = x.shape
    ndt = w_dt.shape[0]
    tm, tn = INPROJ_ROWS, 1024
    n = n_tiles * tn
    cos, s1, s2 = rope
    kern = functools.partial(_inproj_kernel, rope_tiles=rope_tiles, main_tiles=main_tiles)
    return pl.pallas_call(
        kern,
        grid=(bsz, seq // tm, n_tiles),
        in_specs=[pl.BlockSpec((1, tm, d), lambda b, i, j: (b, i, 0)),
                  pl.BlockSpec((1, 1, d), lambda b, i, j: (b + mod_row0, 0, 0)),
                  pl.BlockSpec((1, 1, d), lambda b, i, j: (b + mod_row0, 0, 1)),
                  pl.BlockSpec((1, d), lambda b, i, j: (0, 0)),
                  pl.BlockSpec((d, tn), lambda b, i, j: (0, w_tile(jnp.minimum(j, main_tiles - 1)))),
                  pl.BlockSpec((d, tn), lambda b, i, j: (0, jnp.maximum(j - main_tiles, 0))),
                  pl.BlockSpec((ndt, d), lambda b, i, j: (0, 0)),
                  pl.BlockSpec((tm, LANES), lambda b, i, j: (i, 0)),
                  pl.BlockSpec((tm, LANES), lambda b, i, j: (i, 0)),
                  pl.BlockSpec((tm, LANES), lambda b, i, j: (i, 0))],
        out_specs=[pl.BlockSpec((1, tm, tn), lambda b, i, j: (b, i, j)),
                   pl.BlockSpec((1, ndt, tm), lambda b, i, j: (b, 0, i))],
        out_shape=[jax.ShapeDtypeStruct((bsz, seq, n), BF16),
                   jax.ShapeDtypeStruct((bsz, ndt, seq), F32)],
        scratch_shapes=[pltpu.VMEM((tm, d), BF16)],
        compiler_params=_params(("parallel", "parallel", "arbitrary")),
        name="inproj",
    )(x, mod3, mod3, g_pre, w, w_gate, w_dt, cos, s1, s2)


NA_QROWS = 4
NA_KROWS = 12
NA_QBLK = NA_QROWS * GRID_W
NA_KBLK = NA_KROWS * GRID_W
NA_MASKED = -1e30


def _na_key_start(g, rows):
    return min(max(NA_QROWS * g - NA_KH // 2, 0), rows - NA_KROWS)


NA_TABLE_ROWS = 2 * NA_KH - 2 + 2 * NA_QROWS


def _na_pair_table(rpb):
    col = np.arange(GRID_W)
    col_start = np.clip(col - NA_KW // 2, 0, GRID_W - NA_KW)
    col_ok = (col[None, :] >= col_start[:, None]) & (col[None, :] < col_start[:, None] + NA_KW)
    col_idx = np.clip(col[None, :] - col[:, None] + NA_KW - 1, 0, 2 * NA_KW - 2)
    onehot = ((col_idx[None] == np.arange(2 * NA_KW - 1)[:, None, None]) & col_ok[None]).astype(np.float32)
    toe = jnp.einsum("hrc,cqk->hrqk", rpb, jnp.asarray(onehot), precision=lax.Precision.HIGHEST)
    toe = toe + jnp.asarray(np.where(col_ok, 0.0, NA_MASKED), F32)
    toe = jnp.pad(toe, ((0, 0), (NA_QROWS, NA_QROWS), (0, 0), (0, 0)))
    return jnp.concatenate([toe[:, :-1], toe[:, 1:]], axis=-1)


def _na_block_bias(table_ref, hh, g, rows):
    r = NA_QROWS * g + np.arange(NA_QROWS)
    kr = _na_key_start(g, rows) + np.arange(NA_KROWS)
    r0 = np.clip(r - NA_KH // 2, 0, rows - NA_KH)
    row_ok = (kr[None, :] >= r0[:, None]) & (kr[None, :] < r0[:, None] + NA_KH)
    first = int(kr[0] - r[0]) + NA_KH - 1 + NA_QROWS
    left = lax.broadcasted_iota(jnp.int32, (1, LANES), 1) < GRID_W
    masked = jnp.full((GRID_W, LANES), NA_MASKED, F32)
    out = []
    for dr in range(NA_QROWS):
        tiles = []
        for j in range(NA_KROWS // 2):
            ok0, ok1 = bool(row_ok[dr, 2 * j]), bool(row_ok[dr, 2 * j + 1])
            if not (ok0 or ok1):
                tiles.append(masked)
                continue
            t = table_ref[hh, first - dr + 2 * j]
            tiles.append(t if ok0 and ok1 else jnp.where(left == ok0, t, NA_MASKED))
        out.append(jnp.concatenate(tiles, axis=1))
    return jnp.concatenate(out, axis=0)


def _na_kernel(q_ref, k_ref, v_ref, kc_ref, vc_ref, table_ref, o_ref, s_scr, *, rows):
    nblk = rows // NA_QROWS
    first_head = lax.broadcasted_iota(jnp.int32, (1, LANES), 1) < NA_HEAD_DIM
    kc, vc = kc_ref[0], vc_ref[0]
    nt = (((1,), (1,)), ((), ()))
    ones = jnp.ones((1, LANES), BF16)
    units = [(g, hh) for g in range(nblk) for hh in range(2)]

    def scores(unit, slot):
        g, hh = unit
        k0 = _na_key_start(g, rows) * GRID_W
        q = q_ref[0, g * NA_QBLK:(g + 1) * NA_QBLK, :]
        qm = jnp.where(first_head if hh == 0 else ~first_head, q, jnp.zeros_like(q))
        s_scr[slot, :, :NA_KBLK] = (lax.dot_general(qm, k_ref[0, k0:k0 + NA_KBLK, :], nt, preferred_element_type=F32)
                                    + _na_block_bias(table_ref, hh, g, rows))
        s_scr[slot, :, NA_KBLK:] = lax.dot_general(qm, kc, nt, preferred_element_type=F32)

    scores(units[0], 0)
    prev = None
    for t, (g, hh) in enumerate(units):
        if t + 1 < len(units):
            scores(units[t + 1], (t + 1) % 2)
        mine = first_head if hh == 0 else ~first_head
        k0 = _na_key_start(g, rows) * GRID_W
        s = s_scr[t % 2]
        p = jnp.exp(s - jnp.max(s, axis=-1, keepdims=True)).astype(BF16)
        o = (jnp.dot(p[:, :NA_KBLK], jnp.where(mine, v_ref[0, k0:k0 + NA_KBLK, :], ones), preferred_element_type=F32)
             + jnp.dot(p[:, NA_KBLK:], jnp.where(mine, vc, ones), preferred_element_type=F32))
        o = o / pltpu.roll(o, NA_HEAD_DIM, 1)
        if hh == 0:
            prev = o
        else:
            o_ref[0, g * NA_QBLK:(g + 1) * NA_QBLK, :] = jnp.where(first_head, prev, o).astype(o_ref.dtype)


def _na(slab, slab_ctx, table):
    bsz, seq, _ = slab.shape
    tctx = slab_ctx.shape[1]
    pairs = NA_HEADS // 2
    blk = lambda col0: (lambda hp, b: (b, 0, col0 // LANES + hp))
    return pl.pallas_call(
        functools.partial(_na_kernel, rows=seq // GRID_W),
        grid=(pairs, bsz),
        in_specs=[pl.BlockSpec((1, seq, LANES), blk(COL_Q)),
                  pl.BlockSpec((1, seq, LANES), blk(COL_K)),
                  pl.BlockSpec((1, seq, LANES), blk(COL_V)),
                  pl.BlockSpec((1, tctx, LANES), blk(CTX_K)),
                  pl.BlockSpec((1, tctx, LANES), blk(CTX_V)),
                  pl.BlockSpec((2, NA_TABLE_ROWS, GRID_W, LANES), lambda hp, b: (hp, 0, 0, 0))],
        out_specs=pl.BlockSpec((1, seq, LANES), lambda hp, b: (b, 0, hp)),
        out_shape=jax.ShapeDtypeStruct((bsz, seq, NA_HEADS * NA_HEAD_DIM), BF16),
        scratch_shapes=[pltpu.VMEM((2, NA_QBLK, NA_KBLK + tctx), F32)],
        compiler_params=_params(("parallel", "parallel")),
        name="na",
    )(slab, slab, slab, slab_ctx, slab_ctx, table)


SSD_GHEADS = SSD_HEADS // SSD_GROUPS
SSD_GWIDTH = SSD_GHEADS * SSD_HEAD_DIM
CONV_HALO = 8


SSD_ROWS = 2 * SSD_GHEADS
PACK_V, PACK_W, PACK_E = 0, 3 * SSD_ROWS, 5 * SSD_ROWS


def _ssd_selectors():
    k = np.arange(LANES)[:, None]
    row = k % SSD_ROWS

    def sel(first, terms, head_of_col):
        live = (k >= first) & (k < first + terms * SSD_ROWS)
        return jnp.asarray((live & (row == head_of_col[None, :])).astype(np.float32), BF16)

    col = np.arange(2 * SSD_GWIDTH)
    head_dir = np.where(col < SSD_GWIDTH, col // SSD_HEAD_DIM, SSD_GHEADS + (col - SSD_GWIDTH) // SSD_HEAD_DIM)
    sel_v = sel(PACK_V, 3, np.arange(SSD_ROWS * SSD_CHUNK) // SSD_CHUNK)
    return sel_v, sel(PACK_W, 2, head_dir), sel(PACK_E, 2, head_dir)


def _ssd_kernel(xs_ref, b_ref, c_ref, z_ref, dt_ref, xsc_ref, bc_ref, cc_ref, dtc_ref,
                cwx_ref, cwb_ref, cwc_ref, cbx_ref, cbb_ref, cbc_ref, dtb_ref, alog_ref, dskip_ref, nrm_ref,
                selv_ref, selw_ref, sele_ref,
                y_ref,
                padx, padb, padc, xs_s, bt_s, c_s, pack_s, vrow_s, dtrow_s, hprev_s, sb_s, decb_s, hf_s, hb_s,
                wexp_s, dec_s, vcol_s, cb_s, yoff_s,
                *, seq, tctx):
    Q = SSD_CHUNK
    GH = SSD_GHEADS
    GW = SSD_GWIDTH
    rows = lax.broadcasted_iota(jnp.int32, (Q, Q), 0)
    cols = lax.broadcasted_iota(jnp.int32, (Q, Q), 1)
    cum_rhs = jnp.concatenate([jnp.where(rows <= cols, 1.0, 0.0), jnp.ones((Q, Q), F32)], axis=1).astype(BF16)
    is_fwd = lax.broadcasted_iota(jnp.int32, (SSD_ROWS, 1), 0) < GH
    first_head = lax.broadcasted_iota(jnp.int32, (1, LANES), 1) < SSD_HEAD_DIM
    a_coef = -jnp.exp(alog_ref[...])
    dt_bias = dtb_ref[...]

    def bf_terms(v, n):
        out, rem = [], v
        for _ in range(n):
            t = rem.astype(BF16).astype(F32)
            out.append(t)
            rem = rem - t
        return out

    def prep(x_raw, b_raw, c_raw, dt_raw, n, store_prev):
        for pad, raw in ((padx, x_raw), (padb, b_raw), (padc, c_raw)):
            width = pad.shape[1]
            pad[0:CONV_HALO, :] = jnp.zeros((CONV_HALO, width), F32)
            pad[CONV_HALO + n:2 * CONV_HALO + n, :] = jnp.zeros((CONV_HALO, width), F32)

        def stage(c, carry):
            r0 = pl.multiple_of(c * Q, Q)
            for pad, raw in ((padx, x_raw), (padb, b_raw), (padc, c_raw)):
                pad[pl.ds(r0 + CONV_HALO, Q), :] = raw[0, pl.ds(r0, Q), :].astype(F32)
            return carry

        lax.fori_loop(0, n // Q, stage, 0)

        def conv(pad, w_ref, bias_ref, r0):
            first = CONV_HALO - SSD_CONV // 2
            tiles = []
            for lo in range(0, pad.shape[1], LANES):
                win = pad[pl.ds(r0, Q + 2 * CONV_HALO), lo:lo + LANES]
                acc = bias_ref[:, lo:lo + LANES] + win[first:first + Q, :] * w_ref[0:1, lo:lo + LANES]
                for k in range(1, SSD_CONV):
                    acc = acc + win[first + k:first + k + Q, :] * w_ref[k:k + 1, lo:lo + LANES]
                tiles.append(_silu(acc))
            return tiles[0] if len(tiles) == 1 else jnp.concatenate(tiles, axis=1)

        def scalars(c, slot):
            r0 = pl.multiple_of(c * Q, Q)
            dt = _softplus(dt_raw[0, :, pl.ds(r0, Q)] + dt_bias)
            a = dt * a_coef
            cs = jnp.dot(jnp.concatenate([t.astype(BF16) for t in bf_terms(a, 3)], axis=0), cum_rhs,
                         preferred_element_type=F32)
            cs = cs[0:SSD_ROWS] + cs[SSD_ROWS:2 * SSD_ROWS] + cs[2 * SSD_ROWS:3 * SSD_ROWS]
            acs, tot = cs[:, :Q], cs[:, Q:]
            ex = acs - a
            v = jnp.where(is_fwd, acs, ex)
            w = dt * jnp.exp(jnp.where(is_fwd, tot - acs, ex))
            e = jnp.exp(jnp.where(is_fwd, acs, tot - ex))
            vrow_s[c] = v
            dtrow_s[c] = dt
            packed = jnp.concatenate(bf_terms(v, 3) + bf_terms(w, 2) + bf_terms(e, 2)
                                     + [jnp.zeros((SSD_ROWS, Q), F32)], axis=0)
            pk = packed.T.astype(BF16)
            pack_s[pl.ds(r0, Q), :] = pk
            wexp_s[slot] = jnp.dot(pk, selw_ref[...], preferred_element_type=F32)
            edge = jnp.concatenate([pk[0:16, :], pk[Q - 16:Q, :]], axis=0)
            dec_s[slot] = jnp.dot(edge, sele_ref[...], preferred_element_type=F32)

        def states(c, slot):
            r0 = pl.multiple_of(c * Q, Q)
            b_t = conv(padb, cwb_ref, cbb_ref, r0).T.astype(BF16)
            bt_s[c] = b_t
            c_s[pl.ds(r0, Q), :] = conv(padc, cwc_ref, cbc_ref, r0).astype(BF16)
            xs = conv(padx, cwx_ref, cbx_ref, r0)
            xs_s[pl.ds(r0, Q), :] = xs
            wexp = wexp_s[slot]
            xdec = jnp.concatenate([xs * wexp[:, :GW], xs * wexp[:, GW:]], axis=1).astype(BF16)
            s = jnp.dot(b_t, xdec, preferred_element_type=F32)
            if store_prev:
                hprev_s[c] = hf_s[...].astype(BF16)
            hf_s[...] = dec_s[slot][31:32, :GW] * hf_s[...] + s[:, :GW]
            sb_s[c] = s[:, GW:]
            decb_s[c] = jnp.broadcast_to(dec_s[slot][0:1, GW:], (8, GW))

        nchunks = n // Q
        scalars(0, 0)

        def body(k, carry):
            c = 2 * k
            scalars(c + 1, 1)
            states(c, 0)
            scalars(jnp.minimum(c + 2, nchunks - 1), 0)
            states(c + 1, 1)
            return carry

        lax.fori_loop(0, nchunks // 2, body, 0)

    def backward_step(c):
        hb_s[...] = decb_s[c][0:1, :] * hb_s[...] + sb_s[c]

    hf_s[...] = jnp.zeros(hf_s.shape, F32)
    hb_s[...] = jnp.zeros(hb_s.shape, F32)

    prep(xsc_ref, bc_ref, cc_ref, dtc_ref, tctx, False)
    nctx = tctx // Q

    def ctx_back(i, carry):
        backward_step(nctx - 1 - i)
        return carry

    lax.fori_loop(0, nctx, ctx_back, 0)

    prep(xs_ref, b_ref, c_ref, dt_ref, seq, True)
    nch = seq // Q
    dskip = dskip_ref[...]
    gain = nrm_ref[...]

    def broadcasts(c, slot):
        r0 = pl.multiple_of(c * Q, Q)
        pk = pack_s[pl.ds(r0, Q), :]
        vcol_s[slot] = jnp.dot(pk, selv_ref[...], preferred_element_type=F32)
        eexp = jnp.dot(pk, sele_ref[...], preferred_element_type=F32)
        cm = c_s[pl.ds(r0, Q), :]
        cb_s[slot] = jnp.dot(cm, bt_s[c], preferred_element_type=F32)
        yoff_s[slot] = (jnp.dot(cm, hprev_s[c], preferred_element_type=F32) * eexp[:, :GW]
                        + jnp.dot(cm, hb_s[...].astype(BF16), preferred_element_type=F32) * eexp[:, GW:])

    def outputs(c, slot):
        r0 = pl.multiple_of(c * Q, Q)
        backward_step(c)
        broadcasts(jnp.maximum(c - 1, 0), 1 - slot)
        vcol, cb = vcol_s[slot], cb_s[slot]
        v_t, dt_t = vrow_s[c], dtrow_s[c]
        xs = xs_s[pl.ds(r0, Q), :]
        xsb = xs.astype(BF16)
        gate = _silu(z_ref[0, pl.ds(r0, Q), :].astype(F32))
        tiles = []
        for j in range(GH // 2):
            xpair = xsb[:, j * LANES:(j + 1) * LANES]
            ys = []
            for hh in range(2):
                hf, hb = 2 * j + hh, GH + 2 * j + hh
                arg_f = vcol[:, hf * Q:(hf + 1) * Q] - v_t[hf:hf + 1, :]
                arg_b = v_t[hb:hb + 1, :] - vcol[:, hb * Q:(hb + 1) * Q]
                e = jnp.exp(jnp.where(rows >= cols, arg_f, arg_b))
                dt_f, dt_b = dt_t[hf:hf + 1, :], dt_t[hb:hb + 1, :]
                fac = jnp.where(rows > cols, dt_f, jnp.where(rows < cols, dt_b, dt_f + dt_b))
                m = (cb * e * fac).astype(BF16)
                ys.append(jnp.dot(m, xpair, preferred_element_type=F32))
            tiles.append(jnp.where(first_head, ys[0], ys[1]))
        y = jnp.concatenate(tiles, axis=1) + yoff_s[slot] + dskip * xs
        u = y * gate
        u = u * lax.rsqrt(jnp.mean(u * u, axis=-1, keepdims=True) + NORM_EPS) * gain
        y_ref[0, pl.ds(r0, Q), :] = u.astype(y_ref.dtype)

    broadcasts(nch - 1, 1)

    def ybody(k, carry):
        c = nch - 1 - 2 * k
        outputs(c, 1)
        outputs(c - 1, 0)
        return carry

    lax.fori_loop(0, nch // 2, ybody, 0)


def _ssd(slab, dt, slab_ctx, dt_ctx, cw, cb, dtb, alog, dskip, nrm):
    bsz, seq, _ = slab.shape
    tctx = slab_ctx.shape[1]
    gw, ns = SSD_GWIDTH, SSD_STATE
    nch = seq // SSD_CHUNK
    cwx, cwb, cwc = cw
    cbx, cbb, cbc = cb

    def tok(width, col0, n):
        return pl.BlockSpec((1, n, width), lambda b, g: (b, 0, col0 // width + g))

    def par(rows, width):
        return pl.BlockSpec((rows, width), lambda b, g: (0, g))

    def head_rows(n):
        return pl.BlockSpec((1, SSD_ROWS, n), lambda b, g: (b, g, 0))

    def const(a):
        return pl.BlockSpec(a.shape, lambda b, g: (0, 0))

    sel_v, sel_w, sel_e = _ssd_selectors()
    pad = seq + 2 * CONV_HALO
    return pl.pallas_call(
        functools.partial(_ssd_kernel, seq=seq, tctx=tctx),
        grid=(bsz, SSD_GROUPS),
        in_specs=[tok(gw, COL_X, seq), tok(ns, COL_B, seq), tok(ns, COL_C, seq), tok(gw, COL_Z, seq),
                  head_rows(seq),
                  tok(gw, CTX_X, tctx), tok(ns, CTX_B, tctx), tok(ns, CTX_C, tctx), head_rows(tctx),
                  par(SSD_CONV, gw), par(SSD_CONV, ns), par(SSD_CONV, ns),
                  par(1, gw), par(1, ns), par(1, ns),
                  pl.BlockSpec((SSD_ROWS, LANES), lambda b, g: (g, 0)),
                  pl.BlockSpec((SSD_ROWS, LANES), lambda b, g: (g, 0)),
                  par(1, gw), par(1, gw), const(sel_v), const(sel_w), const(sel_e)],
        out_specs=pl.BlockSpec((1, seq, gw), lambda b, g: (b, 0, g)),
        out_shape=jax.ShapeDtypeStruct((bsz, seq, SSD_WIDTH), BF16),
        scratch_shapes=[pltpu.VMEM((pad, gw), F32), pltpu.VMEM((pad, ns), F32), pltpu.VMEM((pad, ns), F32),
                        pltpu.VMEM((seq, gw), F32), pltpu.VMEM((nch, ns, SSD_CHUNK), BF16),
                        pltpu.VMEM((seq, ns), BF16), pltpu.VMEM((seq, LANES), BF16),
                        pltpu.VMEM((nch, SSD_ROWS, SSD_CHUNK), F32), pltpu.VMEM((nch, SSD_ROWS, SSD_CHUNK), F32),
                        pltpu.VMEM((nch, ns, gw), BF16), pltpu.VMEM((nch, ns, gw), F32),
                        pltpu.VMEM((nch, 8, gw), F32),
                        pltpu.VMEM((ns, gw), F32), pltpu.VMEM((ns, gw), F32),
                        pltpu.VMEM((2, SSD_CHUNK, 2 * gw), F32), pltpu.VMEM((2, 32, 2 * gw), F32),
                        pltpu.VMEM((2, SSD_CHUNK, SSD_ROWS * SSD_CHUNK), F32),
                        pltpu.VMEM((2, SSD_CHUNK, SSD_CHUNK), F32), pltpu.VMEM((2, SSD_CHUNK, gw), F32)],
        compiler_params=_params(("parallel", "parallel")),
        name="ssd",
    )(slab, slab, slab, slab, dt, slab_ctx, slab_ctx, slab_ctx, dt_ctx,
      cwx, cwb, cwc, cbx, cbb, cbc, dtb, alog, dskip, nrm, sel_v, sel_w, sel_e)


def _rms(v, gain):
    return v * lax.rsqrt(jnp.mean(v * v, axis=-1, keepdims=True) + NORM_EPS) * gain


def _merge_kernel(ona_ref, y_ref, gna_ref, gssd_ref, x_ref, ga1_ref, sh2_ref, sc2_ref,
                  wna_ref, wssd_ref, wout_ref, npost_ref, npre_ref, wrh_ref, wrl_ref,
                  x1_ref, h2_ref, aff_ref):
    a = jnp.dot(ona_ref[0], wna_ref[...], preferred_element_type=F32)
    s = jnp.dot(y_ref[0], wssd_ref[...], preferred_element_type=F32)
    u = jax.nn.sigmoid(gna_ref[0].astype(F32)) * a + jax.nn.sigmoid(gssd_ref[0].astype(F32)) * s
    mix = jnp.dot(u.astype(BF16), wout_ref[...], preferred_element_type=F32)
    x1 = x_ref[0] + ga1_ref[0] * _rms(mix, npost_ref[...])
    x1_ref[0] = x1
    h2 = _rms(x1, npre_ref[...]) * (1.0 + sc2_ref[0]) + sh2_ref[0]
    hi = h2.astype(BF16)
    h2_ref[0] = hi
    lo = (h2 - hi.astype(F32)).astype(BF16)
    nt = (((1,), (1,)), ((), ()))
    wrh, wrl = wrh_ref[...], wrl_ref[...]
    logits = (lax.dot_general(wrh, hi, nt, preferred_element_type=F32)
              + lax.dot_general(wrh, lo, nt, preferred_element_type=F32)
              + lax.dot_general(wrl, hi, nt, preferred_element_type=F32))
    e = jnp.exp(logits - jnp.max(logits, axis=0, keepdims=True))
    aff_ref[0] = e / jnp.sum(e, axis=0, keepdims=True)


def _merge(o_na, y_ssd, slab, x, mod3, w_na, w_ssd, w_o, n_post, n_pre, wr_hi, wr_lo):
    bsz, seq, d = x.shape
    tm = 512
    ne = wr_hi.shape[0]
    tok = lambda width, blk: pl.BlockSpec((1, tm, width), lambda b, i: (b, i, blk))
    modv = lambda k: pl.BlockSpec((1, 1, d), lambda b, i: (b, 0, k))
    full = lambda r, c_: pl.BlockSpec((r, c_), lambda b, i: (0, 0))
    return pl.pallas_call(
        _merge_kernel,
        grid=(bsz, seq // tm),
        in_specs=[tok(d, 0), tok(SSD_WIDTH, 0), tok(d, COL_G // d), tok(d, COL_G // d + 1), tok(d, 0),
                  modv(2), modv(3), modv(4),
                  full(d, d), full(SSD_WIDTH, d), full(d, d), full(1, d), full(1, d), full(ne, d), full(ne, d)],
        out_specs=[tok(d, 0), tok(d, 0), pl.BlockSpec((1, ne, tm), lambda b, i: (b, 0, i))],
        out_shape=[jax.ShapeDtypeStruct((bsz, seq, d), F32), jax.ShapeDtypeStruct((bsz, seq, d), BF16),
                   jax.ShapeDtypeStruct((bsz, ne, seq), F32)],
        compiler_params=_params(("parallel", "parallel")),
        name="merge",
    )(o_na, y_ssd, slab, slab, x, mod3, mod3, mod3, w_na, w_ssd, w_o, n_post, n_pre, wr_hi, wr_lo)


def _prefix_count(mask_bf16, strict_upper):
    r, s = mask_bf16.shape
    offset = jnp.zeros((r, 1), F32)
    parts = []
    for j in range(s // LANES):
        seg = mask_bf16[:, j * LANES:(j + 1) * LANES]
        parts.append(jnp.dot(seg, strict_upper, preferred_element_type=F32) + offset)
        offset = offset + jnp.sum(seg.astype(F32), axis=1, keepdims=True)
    return jnp.concatenate(parts, axis=1)


def _route_kernel(aff_ref, h2_ref, pos_ref, xg_ref, gate_ref, *, cap):
    aff = aff_ref[0]
    bits = pltpu.bitcast(aff, jnp.int32)
    ne, t = aff.shape
    thr = jnp.zeros((ne, 1), jnp.int32)
    for bit in range(30, -1, -1):
        cand = thr | (1 << bit)
        cnt = jnp.sum(jnp.where(bits >= cand, 1.0, 0.0), axis=1, keepdims=True)
        thr = jnp.where(cnt >= cap, cand, thr)
    above = bits > thr
    tied = bits == thr
    need = cap - jnp.sum(jnp.where(above, 1.0, 0.0), axis=1, keepdims=True)
    ri = lax.broadcasted_iota(jnp.int32, (LANES, LANES), 0)
    ci = lax.broadcasted_iota(jnp.int32, (LANES, LANES), 1)
    strict_upper = jnp.where(ri < ci, 1.0, 0.0).astype(BF16)
    tie_rank = _prefix_count(jnp.where(tied, 1.0, 0.0).astype(BF16), strict_upper)
    sel = above | (tied & (tie_rank < need))
    slot = _prefix_count(jnp.where(sel, 1.0, 0.0).astype(BF16), strict_upper)
    slot = jnp.where(sel, slot, -1.0)
    pos = slot.astype(jnp.int32)
    padded = jnp.concatenate([slot, jnp.full((LANES - ne, t), -1.0, F32)], axis=0)
    for j in range(t // LANES):
        pos_ref[0, j * LANES:(j + 1) * LANES, :] = padded[:, j * LANES:(j + 1) * LANES].T.astype(jnp.int32)
    h2 = h2_ref[0]
    slot_id = lax.broadcasted_iota(jnp.int32, (cap, t), 0)
    for e in range(ne):
        hit = slot_id == pos[e:e + 1, :]
        onehot = jnp.where(hit, 1.0, 0.0).astype(BF16)
        xg_ref[e] = jnp.dot(onehot, h2, preferred_element_type=F32).astype(xg_ref.dtype)
        gate_ref[e] = jnp.sum(jnp.where(hit, aff[e:e + 1, :], 0.0), axis=1, keepdims=True)


def _route(aff_t, h2):
    bsz, ne, t = aff_t.shape
    d = h2.shape[2]
    cap = EC_CAPACITY_FACTOR * t // ne
    return pl.pallas_call(
        functools.partial(_route_kernel, cap=cap),
        grid=(bsz,),
        in_specs=[pl.BlockSpec((1, ne, t), lambda b: (b, 0, 0)),
                  pl.BlockSpec((1, t, d), lambda b: (b, 0, 0))],
        out_specs=[pl.BlockSpec((1, t, LANES), lambda b: (b, 0, 0)),
                   pl.BlockSpec((ne, cap, d), lambda b: (0, b, 0)),
                   pl.BlockSpec((ne, cap, 1), lambda b: (0, b, 0))],
        out_shape=[jax.ShapeDtypeStruct((bsz, t, LANES), jnp.int32),
                   jax.ShapeDtypeStruct((ne, bsz * cap, d), BF16),
                   jax.ShapeDtypeStruct((ne, bsz * cap, 1), F32)],
        compiler_params=_params(("parallel",)),
        name="route",
    )(aff_t, h2)


EXPERT_FF_TILE = 512
EXPERT_ROW_TILE = 512


def _expert_kernel(xg_ref, gate_ref, wg_ref, wu_ref, wd_ref, yo_ref, acc_ref):
    f = pl.program_id(1)

    @pl.when(f == 0)
    def _():
        acc_ref[...] = jnp.zeros(acc_ref.shape, F32)

    wg, wu, wd = wg_ref[0].astype(BF16), wu_ref[0].astype(BF16), wd_ref[0].astype(BF16)
    n = xg_ref.shape[1]
    rt = min(EXPERT_ROW_TILE, n)
    for r in range(n // rt):
        rows = pl.ds(r * rt, rt)
        xg = xg_ref[0, rows, :]
        hid = _silu(jnp.dot(xg, wg, preferred_element_type=F32)) * jnp.dot(xg, wu, preferred_element_type=F32)
        acc_ref[rows, :] += jnp.dot(hid.astype(BF16), wd, preferred_element_type=F32)

    @pl.when(f == pl.num_programs(1) - 1)
    def _():
        cap = yo_ref.shape[1]
        for b in range(yo_ref.shape[0]):
            rows = pl.ds(b * cap, cap)
            yo_ref[b] = (acc_ref[rows, :] * gate_ref[0, rows, :]).astype(yo_ref.dtype)


def _experts(xg, gate, w_g, w_u, w_d, bsz):
    ne, n, d = xg.shape
    cap = n // bsz
    ff = w_g.shape[2]
    tf = EXPERT_FF_TILE
    return pl.pallas_call(
        _expert_kernel,
        grid=(ne, ff // tf),
        in_specs=[pl.BlockSpec((1, n, d), lambda e, f: (e, 0, 0)),
                  pl.BlockSpec((1, n, 1), lambda e, f: (e, 0, 0)),
                  pl.BlockSpec((1, d, tf), lambda e, f: (e, 0, f)),
                  pl.BlockSpec((1, d, tf), lambda e, f: (e, 0, f)),
                  pl.BlockSpec((1, tf, d), lambda e, f: (e, f, 0))],
        out_specs=pl.BlockSpec((bsz, cap, d), lambda e, f: (0, e, 0)),
        out_shape=jax.ShapeDtypeStruct((bsz, ne * cap, d), BF16),
        scratch_shapes=[pltpu.VMEM((n, d), F32)],
        compiler_params=_params(("parallel", "arbitrary")),
        name="experts",
    )(xg, gate, w_g, w_u, w_d)


def _combine_kernel(pos_ref, yo_ref, x1_ref, ga2_ref, npost_ref, o_ref, *, cap, ne):
    pos = pos_ref[0]
    slot_id = lax.broadcasted_iota(jnp.int32, (1, cap), 1)
    onehot = jnp.concatenate(
        [jnp.where(pos[:, e:e + 1] == slot_id, 1.0, 0.0).astype(BF16) for e in range(ne)], axis=1)
    moe = jnp.dot(onehot, yo_ref[0], preferred_element_type=F32)
    o_ref[0] = x1_ref[0] + ga2_ref[0] * _rms(moe, npost_ref[...])


def _combine(pos_t, yo, x1, mod3, n_post, ne):
    bsz, seq, d = x1.shape
    cap = yo.shape[1] // ne
    tm = 512
    return pl.pallas_call(
        functools.partial(_combine_kernel, cap=cap, ne=ne),
        grid=(bsz, seq // tm),
        in_specs=[pl.BlockSpec((1, tm, LANES), lambda b, i: (b, i, 0)),
                  pl.BlockSpec((1, ne * cap, d), lambda b, i: (b, 0, 0)),
                  pl.BlockSpec((1, tm, d), lambda b, i: (b, i, 0)),
                  pl.BlockSpec((1, 1, d), lambda b, i: (b, 0, 5)),
                  pl.BlockSpec((1, d), lambda b, i: (0, 0))],
        out_specs=pl.BlockSpec((1, tm, d), lambda b, i: (b, i, 0)),
        out_shape=jax.ShapeDtypeStruct((bsz, seq, d), F32),
        compiler_params=_params(("parallel", "parallel")),
        name="combine",
    )(pos_t, yo, x1, mod3, n_post)


def kernel(x, c, ctx, c_ctx, w_ada, b_ada, norm_pre_mix, norm_post_mix, norm_pre_ffn, norm_post_ffn,
           w_in, na_rpb, ssd_conv_w, ssd_conv_b, ssd_a_log_fwd, ssd_a_log_bwd, ssd_dt_bias_fwd,
           ssd_dt_bias_bwd, ssd_d_skip, ssd_norm, w_branch_na, w_branch_ssd, w_out, w_router,
           w_exp_gate, w_exp_up, w_exp_down):
    mod, slab, o_na, y_ssd = _front(x, c, ctx, c_ctx, w_ada[0], b_ada[0], norm_pre_mix[0], w_in[0], na_rpb[0],
                                    ssd_conv_w[0], ssd_conv_b[0], ssd_a_log_fwd[0], ssd_a_log_bwd[0],
                                    ssd_dt_bias_fwd[0], ssd_dt_bias_bwd[0], ssd_d_skip[0], ssd_norm[0])
    return _back(x, mod, slab, o_na, y_ssd, norm_post_mix[0], norm_pre_ffn[0], norm_post_ffn[0],
                 w_branch_na[0], w_branch_ssd[0], w_out[0], w_router[0], w_exp_gate[0], w_exp_up[0], w_exp_down[0])


def _back(x, mod, slab, o_na, y_ssd, norm_post_mix, norm_pre_ffn, norm_post_ffn,
          w_branch_na, w_branch_ssd, w_out, w_router, w_exp_gate, w_exp_up, w_exp_down):
    bsz, seq, d = x.shape
    mod3 = mod.reshape(16, 1, N_MOD * d)
    wr_t = w_router.T
    wr_hi = wr_t.astype(BF16)
    wr_lo = (wr_t - wr_hi.astype(F32)).astype(BF16)
    x1, h2, aff_t = _merge(o_na, y_ssd, slab, x, mod3, w_branch_na.astype(BF16), w_branch_ssd.astype(BF16),
                           w_out.astype(BF16), norm_post_mix.reshape(1, d), norm_pre_ffn.reshape(1, d),
                           wr_hi, wr_lo)
    pos_t, xg, gate = _route(aff_t, h2)
    yo = _experts(xg, gate, w_exp_gate, w_exp_up, w_exp_down, bsz)
    return _combine(pos_t, yo, x1, mod3, norm_post_ffn.reshape(1, d), w_router.shape[1])


def _group_rows(fwd, bwd):
    rows = []
    for g in range(SSD_GROUPS):
        sl = slice(g * SSD_GHEADS, (g + 1) * SSD_GHEADS)
        rows += [fwd[sl], bwd[sl]]
    return jnp.concatenate(rows, axis=0)


def _front(x, c, ctx, c_ctx, w_ada, b_ada, norm_pre_mix, w_in, na_rpb, ssd_conv_w, ssd_conv_b,
           ssd_a_log_fwd, ssd_a_log_bwd, ssd_dt_bias_fwd, ssd_dt_bias_bwd, ssd_d_skip, ssd_norm):
    bsz, seq, d = x.shape
    tctx = ctx.shape[1]
    assert bsz <= 8
    cc =jnp.zeros((16, d), F32).at[:bsz].set(c).at[8].set(c_ctx)
    mod = _ada(cc, w_ada, b_ada)
    mod3 = mod.reshape(16, 1, N_MOD * d)

    dt0 = COL_G
    w_main = w_in.astype(BF16)
    w_gate = w_main[:, dt0 + 2 * SSD_HEADS:]
    main_tiles = dt0 // 1024
    w_dt = _group_rows(w_in[:, dt0:dt0 + SSD_HEADS].T, w_in[:, dt0 + SSD_HEADS:dt0 + 2 * SSD_HEADS].T).astype(BF16)
    rope = _rope_tables(seq)
    g_pre = norm_pre_mix.reshape(1, d)

    slab, dt = _inproj(x, mod3, 0, g_pre, w_main, w_gate, w_dt, rope, 2, main_tiles, LAT_COLS // 1024, lambda j: j)
    nctx = bsz * tctx
    ctx_rows = -(-nctx // INPROJ_ROWS) * INPROJ_ROWS
    ctx_flat = ctx.reshape(1, nctx, d)
    if ctx_rows != nctx:
        ctx_flat = jnp.pad(ctx_flat, ((0, 0), (0, ctx_rows - nctx), (0, 0)))
    ctx_tile = lambda j: jnp.where(j < 2, j + COL_K // 1024, j - 2 + COL_X // 1024)
    ctx_tiles = CTX_COLS // 1024
    slab_c, dt_c = _inproj(ctx_flat, mod3, 8, g_pre, w_main, w_gate, w_dt, rope, 0, ctx_tiles, ctx_tiles, ctx_tile)
    slab_c = slab_c[0, :nctx].reshape(bsz, tctx, CTX_COLS)
    dt_c = jnp.swapaxes(dt_c[0, :, :nctx].reshape(-1, bsz, tctx), 0, 1)

    o_na = _na(slab, slab_c, _na_pair_table(na_rpb))

    nx, nb = SSD_WIDTH, SSD_GROUPS * SSD_STATE
    cw = (ssd_conv_w[:, :nx], ssd_conv_w[:, nx:nx + nb], ssd_conv_w[:, nx + nb:])
    cb = tuple(v.reshape(1, -1) for v in (ssd_conv_b[:nx], ssd_conv_b[nx:nx + nb], ssd_conv_b[nx + nb:]))
    lanes = lambda v: jnp.broadcast_to(v[:, None], (v.shape[0], LANES))
    dtb = _group_rows(lanes(ssd_dt_bias_fwd), lanes(ssd_dt_bias_bwd))
    alog = _group_rows(lanes(ssd_a_log_fwd), lanes(ssd_a_log_bwd))
    dskip = jnp.repeat(ssd_d_skip, SSD_HEAD_DIM).reshape(1, -1)
    y_ssd = _ssd(slab, dt, slab_c, dt_c, cw, cb, dtb, alog, dskip, ssd_norm.reshape(1, -1))
    return mod, slab, o_na, y_ssd
```

```python
import functools
import math

import jax
import jax.numpy as jnp
import numpy as np
from jax import lax
from jax.experimental import pallas as pl
from jax.experimental.pallas import tpu as pltpu

F32 = jnp.float32
BF16 = jnp.bfloat16

D_MODEL = 1024
GRID_W = 64
NORM_EPS = 1e-6
N_MOD = 6
NA_HEADS = 16
NA_HEAD_DIM = 64
NA_KH = 8
NA_KW = 16
ROPE_BASE = 10000.0
SSD_WIDTH = 2048
SSD_HEAD_DIM = 64
SSD_HEADS = 32
SSD_GROUPS = 4
SSD_STATE = 128
SSD_CONV = 5
SSD_CHUNK = 128
N_EXPERTS = 16
EXPERT_FF = 2048
EC_CAPACITY_FACTOR = 2

VMEM_LIMIT_BYTES = 56 * 1024 * 1024
LANES = 128

COL_Q, COL_K, COL_V, COL_Z, COL_X, COL_B, COL_C, COL_G = 0, 1024, 2048, 3072, 5120, 7168, 7680, 8192
LAT_COLS = 10240
CTX_K, CTX_V, CTX_X, CTX_B, CTX_C = 0, 1024, 2048, 4096, 4608
CTX_COLS = 5120


def _params(semantics):
    return pltpu.CompilerParams(dimension_semantics=semantics, vmem_limit_bytes=VMEM_LIMIT_BYTES)


def _sigmoid(v):
    return 0.5 + 0.5 * jnp.tanh(0.5 * v)


def _silu(v):
    h = 0.5 * v
    return h + h * jnp.tanh(h)


def _softplus(v):
    return jnp.maximum(v, 0.0) + jnp.log1p(jnp.exp(-jnp.abs(v)))


def _ada_kernel(c_ref, w_ref, b_ref, o_ref):
    o_ref[...] = jnp.dot(_silu(c_ref[...]), w_ref[...], preferred_element_type=F32) + b_ref[...]


def _ada(cc, w_ada, b_ada):
    rows, d = cc.shape
    n = w_ada.shape[1]
    tn = 1536
    return pl.pallas_call(
        _ada_kernel,
        grid=(n // tn,),
        in_specs=[pl.BlockSpec((rows, d), lambda j: (0, 0)),
                  pl.BlockSpec((d, tn), lambda j: (0, j)),
                  pl.BlockSpec((1, tn), lambda j: (0, j))],
        out_specs=pl.BlockSpec((rows, tn), lambda j: (0, j)),
        out_shape=jax.ShapeDtypeStruct((rows, n), F32),
        compiler_params=_params(("arbitrary",)),
        name="ada",
    )(cc, w_ada, b_ada.reshape(1, n))


def _rope_tables(seq):
    lane = np.arange(LANES)
    axis = (lane % NA_HEAD_DIM) // 32
    within = lane % 32
    half = 16
    inv_freq = ROPE_BASE ** (-(within % half).astype(np.float64) / half)
    pos = np.arange(seq)
    coord = np.where(axis[None, :] == 0, (pos // GRID_W)[:, None], (pos % GRID_W)[:, None])
    ang = coord.astype(np.float32) * inv_freq.astype(np.float32)[None, :]
    cos, sin = jnp.cos(jnp.asarray(ang, F32)), jnp.sin(jnp.asarray(ang, F32))
    first = jnp.asarray(within < half)[None, :]
    return cos, jnp.where(first, -sin, 0.0), jnp.where(first, 0.0, sin)


def _inproj_kernel(x_ref, sh_ref, sc_ref, g_ref, w_ref, wgate_ref, wdt_ref, cos_ref, s1_ref, s2_ref,
                   o_ref, dt_ref, h_scr, *, rope_tiles, main_tiles):
    j = pl.program_id(2)

    @pl.when(j == 0)
    def _():
        x = x_ref[0]
        xn = x * lax.rsqrt(jnp.mean(x * x, axis=-1, keepdims=True) + NORM_EPS) * g_ref[...]
        h = (xn * (1.0 + sc_ref[0]) + sh_ref[0]).astype(BF16)
        h_scr[...] = h
        dt_ref[0] = lax.dot_general(wdt_ref[...], h, (((1,), (1,)), ((), ())), preferred_element_type=F32)

    h = h_scr[...]
    chunks = range(0, o_ref.shape[2], INPROJ_CHUNK)

    def plain(weights):
        for c in chunks:
            acc = jnp.dot(h, weights[:, c:c + INPROJ_CHUNK], preferred_element_type=F32)
            o_ref[0, :, c:c + INPROJ_CHUNK] = acc.astype(o_ref.dtype)

    pl.when((j >= rope_tiles) & (j < main_tiles))(lambda: plain(w_ref))
    pl.when(j >= main_tiles)(lambda: plain(wgate_ref))

    if rope_tiles:
        @pl.when(j < rope_tiles)
        def _():
            scale = jnp.where(j == 0, NA_HEAD_DIM ** -0.5, 1.0).astype(F32)
            cos, s1, s2 = cos_ref[...] * scale, s1_ref[...] * scale, s2_ref[...] * scale
            for c in chunks:
                acc = jnp.dot(h, w_ref[:, c:c + INPROJ_CHUNK], preferred_element_type=F32)
                for l in range(0, INPROJ_CHUNK, LANES):
                    t = acc[:, l:l + LANES]
                    r = t * cos + pltpu.roll(t, LANES - 16, 1) * s1 + pltpu.roll(t, 16, 1) * s2
                    o_ref[0, :, c + l:c + l + LANES] = r.astype(o_ref.dtype)


INPROJ_CHUNK = 256
INPROJ_ROWS = 2048


def _inproj(x, mod3, mod_row0, g_pre, w, w_gate, w_dt, rope, rope_tiles, main_tiles, n_tiles, w_tile):
    bsz, seq, d = x.shape
    ndt = w_dt.shape[0]
    tm, tn = INPROJ_ROWS, 1024
    n = n_tiles * tn
    cos, s1, s2 = rope
    kern = functools.partial(_inproj_kernel, rope_tiles=rope_tiles, main_tiles=main_tiles)
    return pl.pallas_call(
        kern,
        grid=(bsz, seq // tm, n_tiles),
        in_specs=[pl.BlockSpec((1, tm, d), lambda b, i, j: (b, i, 0)),
                  pl.BlockSpec((1, 1, d), lambda b, i, j: (b + mod_row0, 0, 0)),
                  pl.BlockSpec((1, 1, d), lambda b, i, j: (b + mod_row0, 0, 1)),
                  pl.BlockSpec((1, d), lambda b, i, j: (0, 0)),
                  pl.BlockSpec((d, tn), lambda b, i, j: (0, w_tile(jnp.minimum(j, main_tiles - 1)))),
                  pl.BlockSpec((d, tn), lambda b, i, j: (0, jnp.maximum(j - main_tiles, 0))),
                  pl.BlockSpec((ndt, d), lambda b, i, j: (0, 0)),
                  pl.BlockSpec((tm, LANES), lambda b, i, j: (i, 0)),
                  pl.BlockSpec((tm, LANES), lambda b, i, j: (i, 0)),
                  pl.BlockSpec((tm, LANES), lambda b, i, j: (i, 0))],
        out_specs=[pl.BlockSpec((1, tm, tn), lambda b, i, j: (b, i, j)),
                   pl.BlockSpec((1, ndt, tm), lambda b, i, j: (b, 0, i))],
        out_shape=[jax.ShapeDtypeStruct((bsz, seq, n), BF16),
                   jax.ShapeDtypeStruct((bsz, ndt, seq), F32)],
        scratch_shapes=[pltpu.VMEM((tm, d), BF16)],
        compiler_params=_params(("parallel", "parallel", "arbitrary")),
        name="inproj",
    )(x, mod3, mod3, g_pre, w, w_gate, w_dt, cos, s1, s2)


NA_QROWS = 4
NA_KROWS = 12
NA_QBLK = NA_QROWS * GRID_W
NA_KBLK = NA_KROWS * GRID_W
NA_MASKED = -1e30


def _na_key_start(g, rows):
    return min(max(NA_QROWS * g - NA_KH // 2, 0), rows - NA_KROWS)


NA_TABLE_ROWS = 2 * NA_KH - 2 + 2 * NA_QROWS


def _na_pair_table(rpb):
    col = np.arange(GRID_W)
    col_start = np.clip(col - NA_KW // 2, 0, GRID_W - NA_KW)
    col_ok = (col[None, :] >= col_start[:, None]) & (col[None, :] < col_start[:, None] + NA_KW)
    col_idx = np.clip(col[None, :] - col[:, None] + NA_KW - 1, 0, 2 * NA_KW - 2)
    onehot = ((col_idx[None] == np.arange(2 * NA_KW - 1)[:, None, None]) & col_ok[None]).astype(np.float32)
    toe = jnp.einsum("hrc,cqk->hrqk", rpb, jnp.asarray(onehot), precision=lax.Precision.HIGHEST)
    toe = toe + jnp.asarray(np.where(col_ok, 0.0, NA_MASKED), F32)
    toe = jnp.pad(toe, ((0, 0), (NA_QROWS, NA_QROWS), (0, 0), (0, 0)))
    return jnp.concatenate([toe[:, :-1], toe[:, 1:]], axis=-1)


def _na_block_bias(table_ref, hh, g, rows):
    r = NA_QROWS * g + np.arange(NA_QROWS)
    kr = _na_key_start(g, rows) + np.arange(NA_KROWS)
    r0 = np.clip(r - NA_KH // 2, 0, rows - NA_KH)
    row_ok = (kr[None, :] >= r0[:, None]) & (kr[None, :] < r0[:, None] + NA_KH)
    first = int(kr[0] - r[0]) + NA_KH - 1 + NA_QROWS
    left = lax.broadcasted_iota(jnp.int32, (1, LANES), 1) < GRID_W
    masked = jnp.full((GRID_W, LANES), NA_MASKED, F32)
    out = []
    for dr in range(NA_QROWS):
        tiles = []
        for j in range(NA_KROWS // 2):
            ok0, ok1 = bool(row_ok[dr, 2 * j]), bool(row_ok[dr, 2 * j + 1])
            if not (ok0 or ok1):
                tiles.append(masked)
                continue
            t = table_ref[hh, first - dr + 2 * j]
            tiles.append(t if ok0 and ok1 else jnp.where(left == ok0, t, NA_MASKED))
        out.append(jnp.concatenate(tiles, axis=1))
    return jnp.concatenate(out, axis=0)


def _na_kernel(q_ref, k_ref, v_ref, kc_ref, vc_ref, table_ref, o_ref, s_scr, *, rows):
    nblk = rows // NA_QROWS
    first_head = lax.broadcasted_iota(jnp.int32, (1, LANES), 1) < NA_HEAD_DIM
    kc, vc = kc_ref[0], vc_ref[0]
    nt = (((1,), (1,)), ((), ()))
    ones = jnp.ones((1, LANES), BF16)
    units = [(g, hh) for g in range(nblk) for hh in range(2)]

    def scores(unit, slot):
        g, hh = unit
        k0 = _na_key_start(g, rows) * GRID_W
        q = q_ref[0, g * NA_QBLK:(g + 1) * NA_QBLK, :]
        qm = jnp.where(first_head if hh == 0 else ~first_head, q, jnp.zeros_like(q))
        s_scr[slot, :, :NA_KBLK] = (lax.dot_general(qm, k_ref[0, k0:k0 + NA_KBLK, :], nt, preferred_element_type=F32)
                                    + _na_block_bias(table_ref, hh, g, rows))
        s_scr[slot, :, NA_KBLK:] = lax.dot_general(qm, kc, nt, preferred_element_type=F32)

    scores(units[0], 0)
    prev = None
    for t, (g, hh) in enumerate(units):
        if t + 1 < len(units):
            scores(units[t + 1], (t + 1) % 2)
        mine = first_head if hh == 0 else ~first_head
        k0 = _na_key_start(g, rows) * GRID_W
        s = s_scr[t % 2]
        p = jnp.exp(s - jnp.max(s, axis=-1, keepdims=True)).astype(BF16)
        o = (jnp.dot(p[:, :NA_KBLK], jnp.where(mine, v_ref[0, k0:k0 + NA_KBLK, :], ones), preferred_element_type=F32)
             + jnp.dot(p[:, NA_KBLK:], jnp.where(mine, vc, ones), preferred_element_type=F32))
        o = o / pltpu.roll(o, NA_HEAD_DIM, 1)
        if hh == 0:
            prev = o
        else:
            o_ref[0, g * NA_QBLK:(g + 1) * NA_QBLK, :] = jnp.where(first_head, prev, o).astype(o_ref.dtype)


def _na(slab, slab_ctx, table):
    bsz, seq, _ = slab.shape
    tctx = slab_ctx.shape[1]
    pairs = NA_HEADS // 2
    blk = lambda col0: (lambda hp, b: (b, 0, col0 // LANES + hp))
    return pl.pallas_call(
        functools.partial(_na_kernel, rows=seq // GRID_W),
        grid=(pairs, bsz),
        in_specs=[pl.BlockSpec((1, seq, LANES), blk(COL_Q)),
                  pl.BlockSpec((1, seq, LANES), blk(COL_K)),
                  pl.BlockSpec((1, seq, LANES), blk(COL_V)),
                  pl.BlockSpec((1, tctx, LANES), blk(CTX_K)),
                  pl.BlockSpec((1, tctx, LANES), blk(CTX_V)),
                  pl.BlockSpec((2, NA_TABLE_ROWS, GRID_W, LANES), lambda hp, b: (hp, 0, 0, 0))],
        out_specs=pl.BlockSpec((1, seq, LANES), lambda hp, b: (b, 0, hp)),
        out_shape=jax.ShapeDtypeStruct((bsz, seq, NA_HEADS * NA_HEAD_DIM), BF16),
        scratch_shapes=[pltpu.VMEM((2, NA_QBLK, NA_KBLK + tctx), F32)],
        compiler_params=_params(("parallel", "parallel")),
        name="na",
    )(slab, slab, slab, slab_ctx, slab_ctx, table)


SSD_GHEADS = SSD_HEADS // SSD_GROUPS
SSD_GWIDTH = SSD_GHEADS * SSD_HEAD_DIM
CONV_HALO = 8


SSD_ROWS = 2 * SSD_GHEADS
PACK_V, PACK_W, PACK_E = 0, 3 * SSD_ROWS, 5 * SSD_ROWS


def _ssd_selectors():
    k = np.arange(LANES)[:, None]
    row = k % SSD_ROWS

    def sel(first, terms, head_of_col):
        live = (k >= first) & (k < first + terms * SSD_ROWS)
        return jnp.asarray((live & (row == head_of_col[None, :])).astype(np.float32), BF16)

    col = np.arange(2 * SSD_GWIDTH)
    head_dir = np.where(col < SSD_GWIDTH, col // SSD_HEAD_DIM, SSD_GHEADS + (col - SSD_GWIDTH) // SSD_HEAD_DIM)
    sel_v = sel(PACK_V, 3, np.arange(SSD_ROWS * SSD_CHUNK) // SSD_CHUNK)
    return sel_v, sel(PACK_W, 2, head_dir), sel(PACK_E, 2, head_dir)


def _ssd_kernel(xs_ref, b_ref, c_ref, z_ref, dt_ref, xsc_ref, bc_ref, cc_ref, dtc_ref,
                cwx_ref, cwb_ref, cwc_ref, cbx_ref, cbb_ref, cbc_ref, dtb_ref, alog_ref, dskip_ref, nrm_ref,
                selv_ref, selw_ref, sele_ref,
                y_ref,
                padx, padb, padc, xs_s, bt_s, c_s, pack_s, vrow_s, dtrow_s, hprev_s, sb_s, decb_s, hf_s, hb_s,
                wexp_s, dec_s, vcol_s, cb_s, yoff_s,
                *, seq, tctx):
    Q = SSD_CHUNK
    GH = SSD_GHEADS
    GW = SSD_GWIDTH
    rows = lax.broadcasted_iota(jnp.int32, (Q, Q), 0)
    cols = lax.broadcasted_iota(jnp.int32, (Q, Q), 1)
    cum_rhs = jnp.concatenate([jnp.where(rows <= cols, 1.0, 0.0), jnp.ones((Q, Q), F32)], axis=1).astype(BF16)
    is_fwd = lax.broadcasted_iota(jnp.int32, (SSD_ROWS, 1), 0) < GH
    first_head = lax.broadcasted_iota(jnp.int32, (1, LANES), 1) < SSD_HEAD_DIM
    a_coef = -jnp.exp(alog_ref[...])
    dt_bias = dtb_ref[...]

    def bf_terms(v, n):
        out, rem = [], v
        for _ in range(n):
            t = rem.astype(BF16).astype(F32)
            out.append(t)
            rem = rem - t
        return out

    def prep(x_raw, b_raw, c_raw, dt_raw, n, store_prev):
        for pad, raw in ((padx, x_raw), (padb, b_raw), (padc, c_raw)):
            width = pad.shape[1]
            pad[0:CONV_HALO, :] = jnp.zeros((CONV_HALO, width), F32)
            pad[CONV_HALO + n:2 * CONV_HALO + n, :] = jnp.zeros((CONV_HALO, width), F32)

        def stage(c, carry):
            r0 = pl.multiple_of(c * Q, Q)
            for pad, raw in ((padx, x_raw), (padb, b_raw), (padc, c_raw)):
                pad[pl.ds(r0 + CONV_HALO, Q), :] = raw[0, pl.ds(r0, Q), :].astype(F32)
            return carry

        lax.fori_loop(0, n // Q, stage, 0)

        def conv(pad, w_ref, bias_ref, r0):
            first = CONV_HALO - SSD_CONV // 2
            tiles = []
            for lo in range(0, pad.shape[1], LANES):
                win = pad[pl.ds(r0, Q + 2 * CONV_HALO), lo:lo + LANES]
                acc = bias_ref[:, lo:lo + LANES] + win[first:first + Q, :] * w_ref[0:1, lo:lo + LANES]
                for k in range(1, SSD_CONV):
                    acc = acc + win[first + k:first + k + Q, :] * w_ref[k:k + 1, lo:lo + LANES]
                tiles.append(_silu(acc))
            return tiles[0] if len(tiles) == 1 else jnp.concatenate(tiles, axis=1)

        def scalars(c, slot):
            r0 = pl.multiple_of(c * Q, Q)
            dt = _softplus(dt_raw[0, :, pl.ds(r0, Q)] + dt_bias)
            a = dt * a_coef
            cs = jnp.dot(jnp.concatenate([t.astype(BF16) for t in bf_terms(a, 3)], axis=0), cum_rhs,
                         preferred_element_type=F32)
            cs = cs[0:SSD_ROWS] + cs[SSD_ROWS:2 * SSD_ROWS] + cs[2 * SSD_ROWS:3 * SSD_ROWS]
            acs, tot = cs[:, :Q], cs[:, Q:]
            ex = acs - a
            v = jnp.where(is_fwd, acs, ex)
            w = dt * jnp.exp(jnp.where(is_fwd, tot - acs, ex))
            e = jnp.exp(jnp.where(is_fwd, acs, tot - ex))
            log_dt = jnp.log(dt)
            vrow_s[c] = jnp.where(is_fwd, v - log_dt, v + log_dt)
            dtrow_s[c] = jnp.concatenate([jnp.log(dt[:GH] + dt[GH:]), log_dt[GH:]], axis=0)
            packed = jnp.concatenate(bf_terms(v, 3) + bf_terms(w, 2) + bf_terms(e, 2)
                                     + [jnp.zeros((SSD_ROWS, Q), F32)], axis=0)
            pk = packed.T.astype(BF16)
            pack_s[pl.ds(r0, Q), :] = pk
            wexp_s[slot] = jnp.dot(pk, selw_ref[...], preferred_element_type=F32)
            edge = jnp.concatenate([pk[0:16, :], pk[Q - 16:Q, :]], axis=0)
            dec_s[slot] = jnp.dot(edge, sele_ref[...], preferred_element_type=F32)

        def states(c, slot):
            r0 = pl.multiple_of(c * Q, Q)
            b_t = conv(padb, cwb_ref, cbb_ref, r0).T.astype(BF16)
            bt_s[c] = b_t
            c_s[pl.ds(r0, Q), :] = conv(padc, cwc_ref, cbc_ref, r0).astype(BF16)
            xs = conv(padx, cwx_ref, cbx_ref, r0)
            xs_s[pl.ds(r0, Q), :] = xs
            wexp = wexp_s[slot]
            xdec = jnp.concatenate([xs * wexp[:, :GW], xs * wexp[:, GW:]], axis=1).astype(BF16)
            s = jnp.dot(b_t, xdec, preferred_element_type=F32)
            if store_prev:
                hprev_s[c] = hf_s[...].astype(BF16)
            hf_s[...] = dec_s[slot][31:32, :GW] * hf_s[...] + s[:, :GW]
            sb_s[c] = s[:, GW:]
            decb_s[c] = jnp.broadcast_to(dec_s[slot][0:1, GW:], (8, GW))

        nchunks = n // Q
        scalars(0, 0)

        def body(k, carry):
            c = 2 * k
            scalars(c + 1, 1)
            states(c, 0)
            scalars(jnp.minimum(c + 2, nchunks - 1), 0)
            states(c + 1, 1)
            return carry

        lax.fori_loop(0, nchunks // 2, body, 0)

    def backward_step(c):
        hb_s[...] = decb_s[c][0:1, :] * hb_s[...] + sb_s[c]

    hf_s[...] = jnp.zeros(hf_s.shape, F32)
    hb_s[...] = jnp.zeros(hb_s.shape, F32)

    prep(xsc_ref, bc_ref, cc_ref, dtc_ref, tctx, False)
    nctx = tctx // Q

    def ctx_back(i, carry):
        backward_step(nctx - 1 - i)
        return carry

    lax.fori_loop(0, nctx, ctx_back, 0)

    prep(xs_ref, b_ref, c_ref, dt_ref, seq, True)
    nch = seq // Q
    dskip = dskip_ref[...]
    gain = nrm_ref[...]

    def broadcasts(c, slot):
        r0 = pl.multiple_of(c * Q, Q)
        pk = pack_s[pl.ds(r0, Q), :]
        vcol_s[slot] = jnp.dot(pk, selv_ref[...], preferred_element_type=F32)
        eexp = jnp.dot(pk, sele_ref[...], preferred_element_type=F32)
        cm = c_s[pl.ds(r0, Q), :]
        cb_s[slot] = jnp.dot(cm, bt_s[c], preferred_element_type=F32)
        yoff_s[slot] = (jnp.dot(cm, hprev_s[c], preferred_element_type=F32) * eexp[:, :GW]
                        + jnp.dot(cm, hb_s[...].astype(BF16), preferred_element_type=F32) * eexp[:, GW:])

    def outputs(c, slot):
        r0 = pl.multiple_of(c * Q, Q)
        backward_step(c)
        broadcasts(jnp.maximum(c - 1, 0), 1 - slot)
        vcol, cb = vcol_s[slot], cb_s[slot]
        v_t, dt_t = vrow_s[c], dtrow_s[c]
        xs = xs_s[pl.ds(r0, Q), :]
        xsb = xs.astype(BF16)
        gate = _silu(z_ref[0, pl.ds(r0, Q), :].astype(F32))
        tiles = []
        for j in range(GH // 2):
            xpair = xsb[:, j * LANES:(j + 1) * LANES]
            ys = []
            for hh in range(2):
                hf, hb = 2 * j + hh, GH + 2 * j + hh
                arg_f = vcol[:, hf * Q:(hf + 1) * Q] - v_t[hf:hf + 1, :]
                arg_b = v_t[hb:hb + 1, :] - vcol[:, hb * Q:(hb + 1) * Q]
                diag = dt_t[hf:hf + 1, :]
                e = jnp.exp(jnp.where(rows > cols, arg_f, jnp.where(rows < cols, arg_b, diag)))
                m = (cb * e).astype(BF16)
                ys.append(jnp.dot(m, xpair, preferred_element_type=F32))
            tiles.append(jnp.where(first_head, ys[0], ys[1]))
        y = jnp.concatenate(tiles, axis=1) + yoff_s[slot] + dskip * xs
        u = y * gate
        u = u * lax.rsqrt(jnp.mean(u * u, axis=-1, keepdims=True) + NORM_EPS) * gain
        y_ref[0, pl.ds(r0, Q), :] = u.astype(y_ref.dtype)

    broadcasts(nch - 1, 1)

    def ybody(k, carry):
        c = nch - 1 - 2 * k
        outputs(c, 1)
        outputs(c - 1, 0)
        return carry

    lax.fori_loop(0, nch // 2, ybody, 0)


def _ssd(slab, dt, slab_ctx, dt_ctx, cw, cb, dtb, alog, dskip, nrm):
    bsz, seq, _ = slab.shape
    tctx = slab_ctx.shape[1]
    gw, ns = SSD_GWIDTH, SSD_STATE
    nch = seq // SSD_CHUNK
    cwx, cwb, cwc = cw
    cbx, cbb, cbc = cb

    def tok(width, col0, n):
        return pl.BlockSpec((1, n, width), lambda b, g: (b, 0, col0 // width + g))

    def par(rows, width):
        return pl.BlockSpec((rows, width), lambda b, g: (0, g))

    def head_rows(n):
        return pl.BlockSpec((1, SSD_ROWS, n), lambda b, g: (b, g, 0))

    def const(a):
        return pl.BlockSpec(a.shape, lambda b, g: (0, 0))

    sel_v, sel_w, sel_e = _ssd_selectors()
    pad = seq + 2 * CONV_HALO
    return pl.pallas_call(
        functools.partial(_ssd_kernel, seq=seq, tctx=tctx),
        grid=(bsz, SSD_GROUPS),
        in_specs=[tok(gw, COL_X, seq), tok(ns, COL_B, seq), tok(ns, COL_C, seq), tok(gw, COL_Z, seq),
                  head_rows(seq),
                  tok(gw, CTX_X, tctx), tok(ns, CTX_B, tctx), tok(ns, CTX_C, tctx), head_rows(tctx),
                  par(SSD_CONV, gw), par(SSD_CONV, ns), par(SSD_CONV, ns),
                  par(1, gw), par(1, ns), par(1, ns),
                  pl.BlockSpec((SSD_ROWS, LANES), lambda b, g: (g, 0)),
                  pl.BlockSpec((SSD_ROWS, LANES), lambda b, g: (g, 0)),
                  par(1, gw), par(1, gw), const(sel_v), const(sel_w), const(sel_e)],
        out_specs=pl.BlockSpec((1, seq, gw), lambda b, g: (b, 0, g)),
        out_shape=jax.ShapeDtypeStruct((bsz, seq, SSD_WIDTH), BF16),
        scratch_shapes=[pltpu.VMEM((pad, gw), F32), pltpu.VMEM((pad, ns), F32), pltpu.VMEM((pad, ns), F32),
                        pltpu.VMEM((seq, gw), F32), pltpu.VMEM((nch, ns, SSD_CHUNK), BF16),
                        pltpu.VMEM((seq, ns), BF16), pltpu.VMEM((seq, LANES), BF16),
                        pltpu.VMEM((nch, SSD_ROWS, SSD_CHUNK), F32), pltpu.VMEM((nch, SSD_ROWS, SSD_CHUNK), F32),
                        pltpu.VMEM((nch, ns, gw), BF16), pltpu.VMEM((nch, ns, gw), F32),
                        pltpu.VMEM((nch, 8, gw), F32),
                        pltpu.VMEM((ns, gw), F32), pltpu.VMEM((ns, gw), F32),
                        pltpu.VMEM((2, SSD_CHUNK, 2 * gw), F32), pltpu.VMEM((2, 32, 2 * gw), F32),
                        pltpu.VMEM((2, SSD_CHUNK, SSD_ROWS * SSD_CHUNK), F32),
                        pltpu.VMEM((2, SSD_CHUNK, SSD_CHUNK), F32), pltpu.VMEM((2, SSD_CHUNK, gw), F32)],
        compiler_params=_params(("parallel", "parallel")),
        name="ssd",
    )(slab, slab, slab, slab, dt, slab_ctx, slab_ctx, slab_ctx, dt_ctx,
      cwx, cwb, cwc, cbx, cbb, cbc, dtb, alog, dskip, nrm, sel_v, sel_w, sel_e)


def _rms(v, gain):
    return v * lax.rsqrt(jnp.mean(v * v, axis=-1, keepdims=True) + NORM_EPS) * gain


def _merge_kernel(ona_ref, y_ref, gna_ref, gssd_ref, x_ref, ga1_ref, sh2_ref, sc2_ref,
                  wna_ref, wssd_ref, wout_ref, npost_ref, npre_ref, wrh_ref, wrl_ref,
                  x1_ref, h2_ref, aff_ref):
    a = jnp.dot(ona_ref[0], wna_ref[...], preferred_element_type=F32)
    s = jnp.dot(y_ref[0], wssd_ref[...], preferred_element_type=F32)
    u = _sigmoid(gna_ref[0].astype(F32)) * a + _sigmoid(gssd_ref[0].astype(F32)) * s
    mix = jnp.dot(u.astype(BF16), wout_ref[...], preferred_element_type=F32)
    x1 = x_ref[0] + ga1_ref[0] * _rms(mix, npost_ref[...])
    x1_ref[0] = x1
    h2 = _rms(x1, npre_ref[...]) * (1.0 + sc2_ref[0]) + sh2_ref[0]
    hi = h2.astype(BF16)
    h2_ref[0] = hi
    lo = (h2 - hi.astype(F32)).astype(BF16)
    nt = (((1,), (1,)), ((), ()))
    wrh, wrl = wrh_ref[...], wrl_ref[...]
    logits = (lax.dot_general(wrh, hi, nt, preferred_element_type=F32)
              + lax.dot_general(wrh, lo, nt, preferred_element_type=F32)
              + lax.dot_general(wrl, hi, nt, preferred_element_type=F32))
    e = jnp.exp(logits - jnp.max(logits, axis=0, keepdims=True))
    aff_ref[0] = e / jnp.sum(e, axis=0, keepdims=True)


def _merge(o_na, y_ssd, slab, x, mod3, w_na, w_ssd, w_o, n_post, n_pre, wr_hi, wr_lo):
    bsz, seq, d = x.shape
    tm = 512
    ne = wr_hi.shape[0]
    tok = lambda width, blk: pl.BlockSpec((1, tm, width), lambda b, i: (b, i, blk))
    modv = lambda k: pl.BlockSpec((1, 1, d), lambda b, i: (b, 0, k))
    full = lambda r, c_: pl.BlockSpec((r, c_), lambda b, i: (0, 0))
    return pl.pallas_call(
        _merge_kernel,
        grid=(bsz, seq // tm),
        in_specs=[tok(d, 0), tok(SSD_WIDTH, 0), tok(d, COL_G // d), tok(d, COL_G // d + 1), tok(d, 0),
                  modv(2), modv(3), modv(4),
                  full(d, d), full(SSD_WIDTH, d), full(d, d), full(1, d), full(1, d), full(ne, d), full(ne, d)],
        out_specs=[tok(d, 0), tok(d, 0), pl.BlockSpec((1, ne, tm), lambda b, i: (b, 0, i))],
        out_shape=[jax.ShapeDtypeStruct((bsz, seq, d), F32), jax.ShapeDtypeStruct((bsz, seq, d), BF16),
                   jax.ShapeDtypeStruct((bsz, ne, seq), F32)],
        compiler_params=_params(("parallel", "parallel")),
        name="merge",
    )(o_na, y_ssd, slab, slab, x, mod3, mod3, mod3, w_na, w_ssd, w_o, n_post, n_pre, wr_hi, wr_lo)


def _prefix_count(mask_bf16, strict_upper):
    r, s = mask_bf16.shape
    offset = jnp.zeros((r, 1), F32)
    parts = []
    for j in range(s // LANES):
        seg = mask_bf16[:, j * LANES:(j + 1) * LANES]
        parts.append(jnp.dot(seg, strict_upper, preferred_element_type=F32) + offset)
        offset = offset + jnp.sum(seg.astype(F32), axis=1, keepdims=True)
    return jnp.concatenate(parts, axis=1)


def _route_kernel(aff_ref, h2_ref, pos_ref, xg_ref, gate_ref, *, cap):
    aff = aff_ref[0]
    bits = pltpu.bitcast(aff, jnp.int32)
    ne, t = aff.shape
    thr = jnp.zeros((ne, 1), jnp.int32)
    for bit in range(30, -1, -1):
        cand = thr | (1 << bit)
        cnt = jnp.sum(jnp.where(bits >= cand, 1.0, 0.0), axis=1, keepdims=True)
        thr = jnp.where(cnt >= cap, cand, thr)
    above = bits > thr
    tied = bits == thr
    need = cap - jnp.sum(jnp.where(above, 1.0, 0.0), axis=1, keepdims=True)
    ri = lax.broadcasted_iota(jnp.int32, (LANES, LANES), 0)
    ci = lax.broadcasted_iota(jnp.int32, (LANES, LANES), 1)
    strict_upper = jnp.where(ri < ci, 1.0, 0.0).astype(BF16)
    tie_rank = _prefix_count(jnp.where(tied, 1.0, 0.0).astype(BF16), strict_upper)
    sel = above | (tied & (tie_rank < need))
    slot = _prefix_count(jnp.where(sel, 1.0, 0.0).astype(BF16), strict_upper)
    slot = jnp.where(sel, slot, -1.0)
    pos = slot.astype(jnp.int32)
    padded = jnp.concatenate([slot, jnp.full((LANES - ne, t), -1.0, F32)], axis=0)
    for j in range(t // LANES):
        pos_ref[0, j * LANES:(j + 1) * LANES, :] = padded[:, j * LANES:(j + 1) * LANES].T.astype(jnp.int32)
    h2 = h2_ref[0]
    slot_id = lax.broadcasted_iota(jnp.int32, (cap, t), 0)
    for e in range(ne):
        hit = slot_id == pos[e:e + 1, :]
        onehot = jnp.where(hit, 1.0, 0.0).astype(BF16)
        xg_ref[e] = jnp.dot(onehot, h2, preferred_element_type=F32).astype(xg_ref.dtype)
        gate_ref[e] = jnp.sum(jnp.where(hit, aff[e:e + 1, :], 0.0), axis=1, keepdims=True)


def _route(aff_t, h2):
    bsz, ne, t = aff_t.shape
    d = h2.shape[2]
    cap = EC_CAPACITY_FACTOR * t // ne
    return pl.pallas_call(
        functools.partial(_route_kernel, cap=cap),
        grid=(bsz,),
        in_specs=[pl.BlockSpec((1, ne, t), lambda b: (b, 0, 0)),
                  pl.BlockSpec((1, t, d), lambda b: (b, 0, 0))],
        out_specs=[pl.BlockSpec((1, t, LANES), lambda b: (b, 0, 0)),
                   pl.BlockSpec((ne, cap, d), lambda b: (0, b, 0)),
                   pl.BlockSpec((ne, cap, 1), lambda b: (0, b, 0))],
        out_shape=[jax.ShapeDtypeStruct((bsz, t, LANES), jnp.int32),
                   jax.ShapeDtypeStruct((ne, bsz * cap, d), BF16),
                   jax.ShapeDtypeStruct((ne, bsz * cap, 1), F32)],
        compiler_params=_params(("parallel",)),
        name="route",
    )(aff_t, h2)


EXPERT_FF_TILE = 512
EXPERT_ROW_TILE = 512


def _expert_kernel(xg_ref, gate_ref, wg_ref, wu_ref, wd_ref, yo_ref, acc_ref):
    f = pl.program_id(1)

    @pl.when(f == 0)
    def _():
        acc_ref[...] = jnp.zeros(acc_ref.shape, F32)

    wg, wu, wd = wg_ref[0].astype(BF16), wu_ref[0].astype(BF16), wd_ref[0].astype(BF16)
    n = xg_ref.shape[1]
    rt = min(EXPERT_ROW_TILE, n)
    for r in range(n // rt):
        rows = pl.ds(r * rt, rt)
        xg = xg_ref[0, rows, :]
        hid = _silu(jnp.dot(xg, wg, preferred_element_type=F32)) * jnp.dot(xg, wu, preferred_element_type=F32)
        acc_ref[rows, :] += jnp.dot(hid.astype(BF16), wd, preferred_element_type=F32)

    @pl.when(f == pl.num_programs(1) - 1)
    def _():
        cap = yo_ref.shape[1]
        for b in range(yo_ref.shape[0]):
            rows = pl.ds(b * cap, cap)
            yo_ref[b] = (acc_ref[rows, :] * gate_ref[0, rows, :]).astype(yo_ref.dtype)


def _experts(xg, gate, w_g, w_u, w_d, bsz):
    ne, n, d = xg.shape
    cap = n // bsz
    ff = w_g.shape[2]
    tf = EXPERT_FF_TILE
    return pl.pallas_call(
        _expert_kernel,
        grid=(ne, ff // tf),
        in_specs=[pl.BlockSpec((1, n, d), lambda e, f: (e, 0, 0)),
                  pl.BlockSpec((1, n, 1), lambda e, f: (e, 0, 0)),
                  pl.BlockSpec((1, d, tf), lambda e, f: (e, 0, f)),
                  pl.BlockSpec((1, d, tf), lambda e, f: (e, 0, f)),
                  pl.BlockSpec((1, tf, d), lambda e, f: (e, f, 0))],
        out_specs=pl.BlockSpec((bsz, cap, d), lambda e, f: (0, e, 0)),
        out_shape=jax.ShapeDtypeStruct((bsz, ne * cap, d), BF16),
        scratch_shapes=[pltpu.VMEM((n, d), F32)],
        compiler_params=_params(("parallel", "arbitrary")),
        name="experts",
    )(xg, gate, w_g, w_u, w_d)


def _combine_kernel(pos_ref, yo_ref, x1_ref, ga2_ref, npost_ref, o_ref, *, cap, ne):
    pos = pos_ref[0]
    slot_id = lax.broadcasted_iota(jnp.int32, (1, cap), 1)
    onehot = jnp.concatenate(
        [jnp.where(pos[:, e:e + 1] == slot_id, 1.0, 0.0).astype(BF16) for e in range(ne)], axis=1)
    moe = jnp.dot(onehot, yo_ref[0], preferred_element_type=F32)
    o_ref[0] = x1_ref[0] + ga2_ref[0] * _rms(moe, npost_ref[...])


def _combine(pos_t, yo, x1, mod3, n_post, ne):
    bsz, seq, d = x1.shape
    cap = yo.shape[1] // ne
    tm = 512
    return pl.pallas_call(
        functools.partial(_combine_kernel, cap=cap, ne=ne),
        grid=(bsz, seq // tm),
        in_specs=[pl.BlockSpec((1, tm, LANES), lambda b, i: (b, i, 0)),
                  pl.BlockSpec((1, ne * cap, d), lambda b, i: (b, 0, 0)),
                  pl.BlockSpec((1, tm, d), lambda b, i: (b, i, 0)),
                  pl.BlockSpec((1, 1, d), lambda b, i: (b, 0, 5)),
                  pl.BlockSpec((1, d), lambda b, i: (0, 0))],
        out_specs=pl.BlockSpec((1, tm, d), lambda b, i: (b, i, 0)),
        out_shape=jax.ShapeDtypeStruct((bsz, seq, d), F32),
        compiler_params=_params(("parallel", "parallel")),
        name="combine",
    )(pos_t, yo, x1, mod3, n_post)


def kernel(x, c, ctx, c_ctx, w_ada, b_ada, norm_pre_mix, norm_post_mix, norm_pre_ffn, norm_post_ffn,
           w_in, na_rpb, ssd_conv_w, ssd_conv_b, ssd_a_log_fwd, ssd_a_log_bwd, ssd_dt_bias_fwd,
           ssd_dt_bias_bwd, ssd_d_skip, ssd_norm, w_branch_na, w_branch_ssd, w_out, w_router,
           w_exp_gate, w_exp_up, w_exp_down):
    mod, slab, o_na, y_ssd = _front(x, c, ctx, c_ctx, w_ada[0], b_ada[0], norm_pre_mix[0], w_in[0], na_rpb[0],
                                    ssd_conv_w[0], ssd_conv_b[0], ssd_a_log_fwd[0], ssd_a_log_bwd[0],
                                    ssd_dt_bias_fwd[0], ssd_dt_bias_bwd[0], ssd_d_skip[0], ssd_norm[0])
    return _back(x, mod, slab, o_na, y_ssd, norm_post_mix[0], norm_pre_ffn[0], norm_post_ffn[0],
                 w_branch_na[0], w_branch_ssd[0], w_out[0], w_router[0], w_exp_gate[0], w_exp_up[0], w_exp_down[0])


def _back(x, mod, slab, o_na, y_ssd, norm_post_mix, norm_pre_ffn, norm_post_ffn,
          w_branch_na, w_branch_ssd, w_out, w_router, w_exp_gate, w_exp_up, w_exp_down):
    bsz, seq, d = x.shape
    mod3 = mod.reshape(16, 1, N_MOD * d)
    wr_t = w_router.T
    wr_hi = wr_t.astype(BF16)
    wr_lo = (wr_t - wr_hi.astype(F32)).astype(BF16)
    x1, h2, aff_t = _merge(o_na, y_ssd, slab, x, mod3, w_branch_na.astype(BF16), w_branch_ssd.astype(BF16),
                           w_out.astype(BF16), norm_post_mix.reshape(1, d), norm_pre_ffn.reshape(1, d),
                           wr_hi, wr_lo)
    pos_t, xg, gate = _route(aff_t, h2)
    yo = _experts(xg, gate, w_exp_gate, w_exp_up, w_exp_down, bsz)
    return _combine(pos_t, yo, x1, mod3, norm_post_ffn.reshape(1, d), w_router.shape[1])


def _group_rows(fwd, bwd):
    rows = []
    for g in range(SSD_GROUPS):
        sl = slice(g * SSD_GHEADS, (g + 1) * SSD_GHEADS)
        rows += [fwd[sl], bwd[sl]]
    return jnp.concatenate(rows, axis=0)


def _front(x, c, ctx, c_ctx, w_ada, b_ada, norm_pre_mix, w_in, na_rpb, ssd_conv_w, ssd_conv_b,
           ssd_a_log_fwd, ssd_a_log_bwd, ssd_dt_bias_fwd, ssd_dt_bias_bwd, ssd_d_skip, ssd_norm):
    bsz, seq, d = x.shape
    tctx = ctx.shape[1]
    assert bsz <= 8
    cc =jnp.zeros((16, d), F32).at[:bsz].set(c).at[8].set(c_ctx)
    mod = _ada(cc, w_ada, b_ada)
    mod3 = mod.reshape(16, 1, N_MOD * d)

    dt0 = COL_G
    w_main = w_in.astype(BF16)
    w_gate = w_main[:, dt0 + 2 * SSD_HEADS:]
    main_tiles = dt0 // 1024
    w_dt = _group_rows(w_in[:, dt0:dt0 + SSD_HEADS].T, w_in[:, dt0 + SSD_HEADS:dt0 + 2 * SSD_HEADS].T).astype(BF16)
    rope = _rope_tables(seq)
    g_pre = norm_pre_mix.reshape(1, d)

    slab, dt = _inproj(x, mod3, 0, g_pre, w_main, w_gate, w_dt, rope, 2, main_tiles, LAT_COLS // 1024, lambda j: j)
    nctx = bsz * tctx
    ctx_rows = -(-nctx // INPROJ_ROWS) * INPROJ_ROWS
    ctx_flat = ctx.reshape(1, nctx, d)
    if ctx_rows != nctx:
        ctx_flat = jnp.pad(ctx_flat, ((0, 0), (0, ctx_rows - nctx), (0, 0)))
    ctx_tile = lambda j: jnp.where(j < 2, j + COL_K // 1024, j - 2 + COL_X // 1024)
    ctx_tiles = CTX_COLS // 1024
    slab_c, dt_c = _inproj(ctx_flat, mod3, 8, g_pre, w_main, w_gate, w_dt, rope, 0, ctx_tiles, ctx_tiles, ctx_tile)
    slab_c = slab_c[0, :nctx].reshape(bsz, tctx, CTX_COLS)
    dt_c = jnp.swapaxes(dt_c[0, :, :nctx].reshape(-1, bsz, tctx), 0, 1)

    o_na = _na(slab, slab_c, _na_pair_table(na_rpb))

    nx, nb = SSD_WIDTH, SSD_GROUPS * SSD_STATE
    cw = (ssd_conv_w[:, :nx], ssd_conv_w[:, nx:nx + nb], ssd_conv_w[:, nx + nb:])
    cb = tuple(v.reshape(1, -1) for v in (ssd_conv_b[:nx], ssd_conv_b[nx:nx + nb], ssd_conv_b[nx + nb:]))
    lanes = lambda v: jnp.broadcast_to(v[:, None], (v.shape[0], LANES))
    dtb = _group_rows(lanes(ssd_dt_bias_fwd), lanes(ssd_dt_bias_bwd))
    alog = _group_rows(lanes(ssd_a_log_fwd), lanes(ssd_a_log_bwd))
    dskip = jnp.repeat(ssd_d_skip, SSD_HEAD_DIM).reshape(1, -1)
    y_ssd = _ssd(slab, dt, slab_c, dt_c, cw, cb, dtb, alog, dskip, ssd_norm.reshape(1, -1))
    return mod, slab, o_na, y_ssd
```

```python
import functools
import math

import jax
import jax.numpy as jnp
import numpy as np
from jax import lax
from jax.experimental import pallas as pl
from jax.experimental.pallas import tpu as pltpu

F32 = jnp.float32
BF16 = jnp.bfloat16

D_MODEL = 1024
GRID_W = 64
NORM_EPS = 1e-6
N_MOD = 6
NA_HEADS = 16
NA_HEAD_DIM = 64
NA_KH = 8
NA_KW = 16
ROPE_BASE = 10000.0
SSD_WIDTH = 2048
SSD_HEAD_DIM = 64
SSD_HEADS = 32
SSD_GROUPS = 4
SSD_STATE = 128
SSD_CONV = 5
SSD_CHUNK = 128
N_EXPERTS = 16
EXPERT_FF = 2048
EC_CAPACITY_FACTOR = 2

VMEM_LIMIT_BYTES = 56 * 1024 * 1024
LANES = 128

COL_Q, COL_K, COL_V, COL_Z, COL_X, COL_B, COL_C, COL_G = 0, 1024, 2048, 3072, 5120, 7168, 7680, 8192
LAT_COLS = 10240
CTX_K, CTX_V, CTX_X, CTX_B, CTX_C = 0, 1024, 2048, 4096, 4608
CTX_COLS = 5120


def _params(semantics):
    return pltpu.CompilerParams(dimension_semantics=semantics, vmem_limit_bytes=VMEM_LIMIT_BYTES)


def _sigmoid(v):
    return 0.5 + 0.5 * jnp.tanh(0.5 * v)


def _silu(v):
    h = 0.5 * v
    return h + h * jnp.tanh(h)


def _softplus(v):
    return jnp.maximum(v, 0.0) + jnp.log1p(jnp.exp(-jnp.abs(v)))


def _ada_kernel(c_ref, w_ref, b_ref, o_ref):
    o_ref[...] = jnp.dot(_silu(c_ref[...]), w_ref[...], preferred_element_type=F32) + b_ref[...]


def _ada(cc, w_ada, b_ada):
    rows, d = cc.shape
    n = w_ada.shape[1]
    tn = 1536
    return pl.pallas_call(
        _ada_kernel,
        grid=(n // tn,),
        in_specs=[pl.BlockSpec((rows, d), lambda j: (0, 0)),
                  pl.BlockSpec((d, tn), lambda j: (0, j)),
                  pl.BlockSpec((1, tn), lambda j: (0, j))],
        out_specs=pl.BlockSpec((rows, tn), lambda j: (0, j)),
        out_shape=jax.ShapeDtypeStruct((rows, n), F32),
        compiler_params=_params(("arbitrary",)),
        name="ada",
    )(cc, w_ada, b_ada.reshape(1, n))


def _rope_tables(seq):
    lane = np.arange(LANES)
    axis = (lane % NA_HEAD_DIM) // 32
    within = lane % 32
    half = 16
    inv_freq = ROPE_BASE ** (-(within % half).astype(np.float64) / half)
    pos = np.arange(seq)
    coord = np.where(axis[None, :] == 0, (pos // GRID_W)[:, None], (pos % GRID_W)[:, None])
    ang = coord.astype(np.float32) * inv_freq.astype(np.float32)[None, :]
    cos, sin = jnp.cos(jnp.asarray(ang, F32)), jnp.sin(jnp.asarray(ang, F32))
    first = jnp.asarray(within < half)[None, :]
    return cos, jnp.where(first, -sin, 0.0), jnp.where(first, 0.0, sin)


def _inproj_kernel(x_ref, sh_ref, sc_ref, g_ref, w_ref, wgate_ref, wdt_ref, cos_ref, s1_ref, s2_ref,
                   o_ref, dt_ref, h_scr, *, rope_tiles, main_tiles):
    j = pl.program_id(2)

    @pl.when(j == 0)
    def _():
        x = x_ref[0]
        xn = x * lax.rsqrt(jnp.mean(x * x, axis=-1, keepdims=True) + NORM_EPS) * g_ref[...]
        h = (xn * (1.0 + sc_ref[0]) + sh_ref[0]).astype(BF16)
        h_scr[...] = h
        dt_ref[0] = lax.dot_general(wdt_ref[...], h, (((1,), (1,)), ((), ())), preferred_element_type=F32)

    h = h_scr[...]
    chunks = range(0, o_ref.shape[2], INPROJ_CHUNK)

    nt = (((1,), (1,)), ((), ()))

    def plain(weights):
        for c in chunks:
            acc = lax.dot_general(h, weights[c:c + INPROJ_CHUNK, :], nt, preferred_element_type=F32)
            o_ref[0, :, c:c + INPROJ_CHUNK] = acc.astype(o_ref.dtype)

    pl.when((j >= rope_tiles) & (j < main_tiles))(lambda: plain(w_ref))
    pl.when(j >= main_tiles)(lambda: plain(wgate_ref))

    if rope_tiles:
        @pl.when(j < rope_tiles)
        def _():
            scale = jnp.where(j == 0, NA_HEAD_DIM ** -0.5, 1.0).astype(F32)
            cos, s1, s2 = cos_ref[...] * scale, s1_ref[...] * scale, s2_ref[...] * scale
            for c in chunks:
                acc = lax.dot_general(h, w_ref[c:c + INPROJ_CHUNK, :], nt, preferred_element_type=F32)
                for l in range(0, INPROJ_CHUNK, LANES):
                    t = acc[:, l:l + LANES]
                    r = t * cos + pltpu.roll(t, LANES - 16, 1) * s1 + pltpu.roll(t, 16, 1) * s2
                    o_ref[0, :, c + l:c + l + LANES] = r.astype(o_ref.dtype)


INPROJ_CHUNK = 256
INPROJ_ROWS = 2048


def _inproj(x, mod3, mod_row0, g_pre, w, w_gate, w_dt, rope, rope_tiles, main_tiles, n_tiles, w_tile):
    bsz, seq, d = x.shape
    ndt = w_dt.shape[0]
    tm, tn = INPROJ_ROWS, 1024
    n = n_tiles * tn
    cos, s1, s2 = rope
    kern = functools.partial(_inproj_kernel, rope_tiles=rope_tiles, main_tiles=main_tiles)
    return pl.pallas_call(
        kern,
        grid=(bsz, seq // tm, n_tiles),
        in_specs=[pl.BlockSpec((1, tm, d), lambda b, i, j: (b, i, 0)),
                  pl.BlockSpec((1, 1, d), lambda b, i, j: (b + mod_row0, 0, 0)),
                  pl.BlockSpec((1, 1, d), lambda b, i, j: (b + mod_row0, 0, 1)),
                  pl.BlockSpec((1, d), lambda b, i, j: (0, 0)),
                  pl.BlockSpec((tn, d), lambda b, i, j: (w_tile(jnp.minimum(j, main_tiles - 1)), 0)),
                  pl.BlockSpec((tn, d), lambda b, i, j: (jnp.maximum(j - main_tiles, 0), 0)),
                  pl.BlockSpec((ndt, d), lambda b, i, j: (0, 0)),
                  pl.BlockSpec((tm, LANES), lambda b, i, j: (i, 0)),
                  pl.BlockSpec((tm, LANES), lambda b, i, j: (i, 0)),
                  pl.BlockSpec((tm, LANES), lambda b, i, j: (i, 0))],
        out_specs=[pl.BlockSpec((1, tm, tn), lambda b, i, j: (b, i, j)),
                   pl.BlockSpec((1, ndt, tm), lambda b, i, j: (b, 0, i))],
        out_shape=[jax.ShapeDtypeStruct((bsz, seq, n), BF16),
                   jax.ShapeDtypeStruct((bsz, ndt, seq), F32)],
        scratch_shapes=[pltpu.VMEM((tm, d), BF16)],
        compiler_params=_params(("parallel", "parallel", "arbitrary")),
        name="inproj",
    )(x, mod3, mod3, g_pre, w, w_gate, w_dt, cos, s1, s2)


NA_QROWS = 4
NA_KROWS = 12
NA_QBLK = NA_QROWS * GRID_W
NA_KBLK = NA_KROWS * GRID_W
NA_MASKED = -1e30


def _na_key_start(g, rows):
    return min(max(NA_QROWS * g - NA_KH // 2, 0), rows - NA_KROWS)


NA_TABLE_ROWS = 2 * NA_KH - 2 + 2 * NA_QROWS


def _na_pair_table(rpb):
    col = np.arange(GRID_W)
    col_start = np.clip(col - NA_KW // 2, 0, GRID_W - NA_KW)
    col_ok = (col[None, :] >= col_start[:, None]) & (col[None, :] < col_start[:, None] + NA_KW)
    col_idx = np.clip(col[None, :] - col[:, None] + NA_KW - 1, 0, 2 * NA_KW - 2)
    onehot = ((col_idx[None] == np.arange(2 * NA_KW - 1)[:, None, None]) & col_ok[None]).astype(np.float32)
    toe = jnp.einsum("hrc,cqk->hrqk", rpb, jnp.asarray(onehot), precision=lax.Precision.HIGHEST)
    toe = toe + jnp.asarray(np.where(col_ok, 0.0, NA_MASKED), F32)
    toe = jnp.pad(toe, ((0, 0), (NA_QROWS, NA_QROWS), (0, 0), (0, 0)))
    return jnp.concatenate([toe[:, :-1], toe[:, 1:]], axis=-1)


def _na_block_bias(table_ref, hh, g, rows):
    r = NA_QROWS * g + np.arange(NA_QROWS)
    kr = _na_key_start(g, rows) + np.arange(NA_KROWS)
    r0 = np.clip(r - NA_KH // 2, 0, rows - NA_KH)
    row_ok = (kr[None, :] >= r0[:, None]) & (kr[None, :] < r0[:, None] + NA_KH)
    first = int(kr[0] - r[0]) + NA_KH - 1 + NA_QROWS
    left = lax.broadcasted_iota(jnp.int32, (1, LANES), 1) < GRID_W
    masked = jnp.full((GRID_W, LANES), NA_MASKED, F32)
    out = []
    for dr in range(NA_QROWS):
        tiles = []
        for j in range(NA_KROWS // 2):
            ok0, ok1 = bool(row_ok[dr, 2 * j]), bool(row_ok[dr, 2 * j + 1])
            if not (ok0 or ok1):
                tiles.append(masked)
                continue
            t = table_ref[hh, first - dr + 2 * j]
            tiles.append(t if ok0 and ok1 else jnp.where(left == ok0, t, NA_MASKED))
        out.append(jnp.concatenate(tiles, axis=1))
    return jnp.concatenate(out, axis=0)


def _na_kernel(q_ref, k_ref, v_ref, kc_ref, vc_ref, table_ref, o_ref, s_scr, *, rows):
    nblk = rows // NA_QROWS
    first_head = lax.broadcasted_iota(jnp.int32, (1, LANES), 1) < NA_HEAD_DIM
    kc, vc = kc_ref[0], vc_ref[0]
    nt = (((1,), (1,)), ((), ()))
    ones = jnp.ones((1, LANES), BF16)
    units = [(g, hh) for g in range(nblk) for hh in range(2)]

    def scores(unit, slot):
        g, hh = unit
        k0 = _na_key_start(g, rows) * GRID_W
        q = q_ref[0, g * NA_QBLK:(g + 1) * NA_QBLK, :]
        qm = jnp.where(first_head if hh == 0 else ~first_head, q, jnp.zeros_like(q))
        s_scr[slot, :, :NA_KBLK] = (lax.dot_general(qm, k_ref[0, k0:k0 + NA_KBLK, :], nt, preferred_element_type=F32)
                                    + _na_block_bias(table_ref, hh, g, rows))
        s_scr[slot, :, NA_KBLK:] = lax.dot_general(qm, kc, nt, preferred_element_type=F32)

    scores(units[0], 0)
    prev = None
    for t, (g, hh) in enumerate(units):
        if t + 1 < len(units):
            scores(units[t + 1], (t + 1) % 2)
        mine = first_head if hh == 0 else ~first_head
        k0 = _na_key_start(g, rows) * GRID_W
        s = s_scr[t % 2]
        p = jnp.exp(s - jnp.max(s, axis=-1, keepdims=True)).astype(BF16)
        o = (jnp.dot(p[:, :NA_KBLK], jnp.where(mine, v_ref[0, k0:k0 + NA_KBLK, :], ones), preferred_element_type=F32)
             + jnp.dot(p[:, NA_KBLK:], jnp.where(mine, vc, ones), preferred_element_type=F32))
        o = o / pltpu.roll(o, NA_HEAD_DIM, 1)
        if hh == 0:
            prev = o
        else:
            o_ref[0, g * NA_QBLK:(g + 1) * NA_QBLK, :] = jnp.where(first_head, prev, o).astype(o_ref.dtype)


def _na(slab, slab_ctx, table):
    bsz, seq, _ = slab.shape
    tctx = slab_ctx.shape[1]
    pairs = NA_HEADS // 2
    blk = lambda col0: (lambda hp, b: (b, 0, col0 // LANES + hp))
    return pl.pallas_call(
        functools.partial(_na_kernel, rows=seq // GRID_W),
        grid=(pairs, bsz),
        in_specs=[pl.BlockSpec((1, seq, LANES), blk(COL_Q)),
                  pl.BlockSpec((1, seq, LANES), blk(COL_K)),
                  pl.BlockSpec((1, seq, LANES), blk(COL_V)),
                  pl.BlockSpec((1, tctx, LANES), blk(CTX_K)),
                  pl.BlockSpec((1, tctx, LANES), blk(CTX_V)),
                  pl.BlockSpec((2, NA_TABLE_ROWS, GRID_W, LANES), lambda hp, b: (hp, 0, 0, 0))],
        out_specs=pl.BlockSpec((1, seq, LANES), lambda hp, b: (b, 0, hp)),
        out_shape=jax.ShapeDtypeStruct((bsz, seq, NA_HEADS * NA_HEAD_DIM), BF16),
        scratch_shapes=[pltpu.VMEM((2, NA_QBLK, NA_KBLK + tctx), F32)],
        compiler_params=_params(("parallel", "parallel")),
        name="na",
    )(slab, slab, slab, slab_ctx, slab_ctx, table)


SSD_GHEADS = SSD_HEADS // SSD_GROUPS
SSD_GWIDTH = SSD_GHEADS * SSD_HEAD_DIM
CONV_HALO = 8


SSD_ROWS = 2 * SSD_GHEADS
PACK_V, PACK_W, PACK_E = 0, 3 * SSD_ROWS, 5 * SSD_ROWS


def _ssd_selectors():
    k = np.arange(LANES)[:, None]
    row = k % SSD_ROWS

    def sel(first, terms, head_of_col):
        live = (k >= first) & (k < first + terms * SSD_ROWS)
        return jnp.asarray((live & (row == head_of_col[None, :])).astype(np.float32), BF16)

    col = np.arange(2 * SSD_GWIDTH)
    head_dir = np.where(col < SSD_GWIDTH, col // SSD_HEAD_DIM, SSD_GHEADS + (col - SSD_GWIDTH) // SSD_HEAD_DIM)
    sel_v = sel(PACK_V, 3, np.arange(SSD_ROWS * SSD_CHUNK) // SSD_CHUNK)
    return sel_v, sel(PACK_W, 2, head_dir), sel(PACK_E, 2, head_dir)


def _ssd_kernel(xs_ref, b_ref, c_ref, z_ref, dt_ref, xsc_ref, bc_ref, cc_ref, dtc_ref,
                cwx_ref, cwb_ref, cwc_ref, cbx_ref, cbb_ref, cbc_ref, dtb_ref, alog_ref, dskip_ref, nrm_ref,
                selv_ref, selw_ref, sele_ref,
                y_ref,
                padx, padb, padc, xs_s, bt_s, c_s, pack_s, vrow_s, dtrow_s, hprev_s, sb_s, decb_s, hf_s, hb_s,
                wexp_s, dec_s, vcol_s, cb_s, yoff_s,
                *, seq, tctx):
    Q = SSD_CHUNK
    GH = SSD_GHEADS
    GW = SSD_GWIDTH
    rows = lax.broadcasted_iota(jnp.int32, (Q, Q), 0)
    cols = lax.broadcasted_iota(jnp.int32, (Q, Q), 1)
    cum_rhs = jnp.concatenate([jnp.where(rows <= cols, 1.0, 0.0), jnp.ones((Q, Q), F32)], axis=1).astype(BF16)
    is_fwd = lax.broadcasted_iota(jnp.int32, (SSD_ROWS, 1), 0) < GH
    first_head = lax.broadcasted_iota(jnp.int32, (1, LANES), 1) < SSD_HEAD_DIM
    a_coef = -jnp.exp(alog_ref[...])
    dt_bias = dtb_ref[...]

    def bf_terms(v, n):
        out, rem = [], v
        for _ in range(n):
            t = rem.astype(BF16).astype(F32)
            out.append(t)
            rem = rem - t
        return out

    def prep(x_raw, b_raw, c_raw, dt_raw, n, store_prev):
        for pad, raw in ((padx, x_raw), (padb, b_raw), (padc, c_raw)):
            width = pad.shape[1]
            pad[0:CONV_HALO, :] = jnp.zeros((CONV_HALO, width), F32)
            pad[CONV_HALO + n:2 * CONV_HALO + n, :] = jnp.zeros((CONV_HALO, width), F32)

        def stage(c, carry):
            r0 = pl.multiple_of(c * Q, Q)
            for pad, raw in ((padx, x_raw), (padb, b_raw), (padc, c_raw)):
                pad[pl.ds(r0 + CONV_HALO, Q), :] = raw[0, pl.ds(r0, Q), :].astype(F32)
            return carry

        lax.fori_loop(0, n // Q, stage, 0)

        def conv(pad, w_ref, bias_ref, r0):
            first = CONV_HALO - SSD_CONV // 2
            tiles = []
            for lo in range(0, pad.shape[1], LANES):
                win = pad[pl.ds(r0, Q + 2 * CONV_HALO), lo:lo + LANES]
                acc = bias_ref[:, lo:lo + LANES] + win[first:first + Q, :] * w_ref[0:1, lo:lo + LANES]
                for k in range(1, SSD_CONV):
                    acc = acc + win[first + k:first + k + Q, :] * w_ref[k:k + 1, lo:lo + LANES]
                tiles.append(_silu(acc))
            return tiles[0] if len(tiles) == 1 else jnp.concatenate(tiles, axis=1)

        def scalars(c, slot):
            r0 = pl.multiple_of(c * Q, Q)
            dt = _softplus(dt_raw[0, :, pl.ds(r0, Q)] + dt_bias)
            a = dt * a_coef
            cs = jnp.dot(jnp.concatenate([t.astype(BF16) for t in bf_terms(a, 3)], axis=0), cum_rhs,
                         preferred_element_type=F32)
            cs = cs[0:SSD_ROWS] + cs[SSD_ROWS:2 * SSD_ROWS] + cs[2 * SSD_ROWS:3 * SSD_ROWS]
            acs, tot = cs[:, :Q], cs[:, Q:]
            ex = acs - a
            v = jnp.where(is_fwd, acs, ex)
            w = dt * jnp.exp(jnp.where(is_fwd, tot - acs, ex))
            e = jnp.exp(jnp.where(is_fwd, acs, tot - ex))
            log_dt = jnp.log(dt)
            vrow_s[c] = jnp.where(is_fwd, v - log_dt, v + log_dt)
            dtrow_s[c] = jnp.concatenate([jnp.log(dt[:GH] + dt[GH:]), log_dt[GH:]], axis=0)
            packed = jnp.concatenate(bf_terms(v, 3) + bf_terms(w, 2) + bf_terms(e, 2)
                                     + [jnp.zeros((SSD_ROWS, Q), F32)], axis=0)
            pk = packed.T.astype(BF16)
            pack_s[pl.ds(r0, Q), :] = pk
            wexp_s[slot] = jnp.dot(pk, selw_ref[...], preferred_element_type=F32)
            edge = jnp.concatenate([pk[0:16, :], pk[Q - 16:Q, :]], axis=0)
            dec_s[slot] = jnp.dot(edge, sele_ref[...], preferred_element_type=F32)

        def states(c, slot):
            r0 = pl.multiple_of(c * Q, Q)
            b_t = conv(padb, cwb_ref, cbb_ref, r0).T.astype(BF16)
            bt_s[c] = b_t
            c_s[pl.ds(r0, Q), :] = conv(padc, cwc_ref, cbc_ref, r0).astype(BF16)
            xs = conv(padx, cwx_ref, cbx_ref, r0)
            xs_s[pl.ds(r0, Q), :] = xs
            wexp = wexp_s[slot]
            xdec = jnp.concatenate([xs * wexp[:, :GW], xs * wexp[:, GW:]], axis=1).astype(BF16)
            s = jnp.dot(b_t, xdec, preferred_element_type=F32)
            if store_prev:
                hprev_s[c] = hf_s[...].astype(BF16)
            hf_s[...] = dec_s[slot][31:32, :GW] * hf_s[...] + s[:, :GW]
            sb_s[c] = s[:, GW:]
            decb_s[c] = jnp.broadcast_to(dec_s[slot][0:1, GW:], (8, GW))

        nchunks = n // Q
        scalars(0, 0)

        def body(k, carry):
            c = 2 * k
            scalars(c + 1, 1)
            states(c, 0)
            scalars(jnp.minimum(c + 2, nchunks - 1), 0)
            states(c + 1, 1)
            return carry

        lax.fori_loop(0, nchunks // 2, body, 0)

    def backward_step(c):
        hb_s[...] = decb_s[c][0:1, :] * hb_s[...] + sb_s[c]

    hf_s[...] = jnp.zeros(hf_s.shape, F32)
    hb_s[...] = jnp.zeros(hb_s.shape, F32)

    prep(xsc_ref, bc_ref, cc_ref, dtc_ref, tctx, False)
    nctx = tctx // Q

    def ctx_back(i, carry):
        backward_step(nctx - 1 - i)
        return carry

    lax.fori_loop(0, nctx, ctx_back, 0)

    prep(xs_ref, b_ref, c_ref, dt_ref, seq, True)
    nch = seq // Q
    dskip = dskip_ref[...]
    gain = nrm_ref[...]

    def broadcasts(c, slot):
        r0 = pl.multiple_of(c * Q, Q)
        pk = pack_s[pl.ds(r0, Q), :]
        vcol_s[slot] = jnp.dot(pk, selv_ref[...], preferred_element_type=F32)
        eexp = jnp.dot(pk, sele_ref[...], preferred_element_type=F32)
        cm = c_s[pl.ds(r0, Q), :]
        cb_s[slot] = jnp.dot(cm, bt_s[c], preferred_element_type=F32)
        yoff_s[slot] = (jnp.dot(cm, hprev_s[c], preferred_element_type=F32) * eexp[:, :GW]
                        + jnp.dot(cm, hb_s[...].astype(BF16), preferred_element_type=F32) * eexp[:, GW:])

    def outputs(c, slot):
        r0 = pl.multiple_of(c * Q, Q)
        backward_step(c)
        broadcasts(jnp.maximum(c - 1, 0), 1 - slot)
        vcol, cb = vcol_s[slot], cb_s[slot]
        v_t, dt_t = vrow_s[c], dtrow_s[c]
        xs = xs_s[pl.ds(r0, Q), :]
        xsb = xs.astype(BF16)
        gate = _silu(z_ref[0, pl.ds(r0, Q), :].astype(F32))
        tiles = []
        for j in range(GH // 2):
            xpair = xsb[:, j * LANES:(j + 1) * LANES]
            ys = []
            for hh in range(2):
                hf, hb = 2 * j + hh, GH + 2 * j + hh
                arg_f = vcol[:, hf * Q:(hf + 1) * Q] - v_t[hf:hf + 1, :]
                arg_b = v_t[hb:hb + 1, :] - vcol[:, hb * Q:(hb + 1) * Q]
                diag = dt_t[hf:hf + 1, :]
                e = jnp.exp(jnp.where(rows > cols, arg_f, jnp.where(rows < cols, arg_b, diag)))
                m = (cb * e).astype(BF16)
                ys.append(jnp.dot(m, xpair, preferred_element_type=F32))
            tiles.append(jnp.where(first_head, ys[0], ys[1]))
        y = jnp.concatenate(tiles, axis=1) + yoff_s[slot] + dskip * xs
        u = y * gate
        u = u * lax.rsqrt(jnp.mean(u * u, axis=-1, keepdims=True) + NORM_EPS) * gain
        y_ref[0, pl.ds(r0, Q), :] = u.astype(y_ref.dtype)

    broadcasts(nch - 1, 1)

    def ybody(k, carry):
        c = nch - 1 - 2 * k
        outputs(c, 1)
        outputs(c - 1, 0)
        return carry

    lax.fori_loop(0, nch // 2, ybody, 0)


def _ssd(slab, dt, slab_ctx, dt_ctx, cw, cb, dtb, alog, dskip, nrm):
    bsz, seq, _ = slab.shape
    tctx = slab_ctx.shape[1]
    gw, ns = SSD_GWIDTH, SSD_STATE
    nch = seq // SSD_CHUNK
    cwx, cwb, cwc = cw
    cbx, cbb, cbc = cb

    def tok(width, col0, n):
        return pl.BlockSpec((1, n, width), lambda b, g: (b, 0, col0 // width + g))

    def par(rows, width):
        return pl.BlockSpec((rows, width), lambda b, g: (0, g))

    def head_rows(n):
        return pl.BlockSpec((1, SSD_ROWS, n), lambda b, g: (b, g, 0))

    def const(a):
        return pl.BlockSpec(a.shape, lambda b, g: (0, 0))

    sel_v, sel_w, sel_e = _ssd_selectors()
    pad = seq + 2 * CONV_HALO
    return pl.pallas_call(
        functools.partial(_ssd_kernel, seq=seq, tctx=tctx),
        grid=(bsz, SSD_GROUPS),
        in_specs=[tok(gw, COL_X, seq), tok(ns, COL_B, seq), tok(ns, COL_C, seq), tok(gw, COL_Z, seq),
                  head_rows(seq),
                  tok(gw, CTX_X, tctx), tok(ns, CTX_B, tctx), tok(ns, CTX_C, tctx), head_rows(tctx),
                  par(SSD_CONV, gw), par(SSD_CONV, ns), par(SSD_CONV, ns),
                  par(1, gw), par(1, ns), par(1, ns),
                  pl.BlockSpec((SSD_ROWS, LANES), lambda b, g: (g, 0)),
                  pl.BlockSpec((SSD_ROWS, LANES), lambda b, g: (g, 0)),
                  par(1, gw), par(1, gw), const(sel_v), const(sel_w), const(sel_e)],
        out_specs=pl.BlockSpec((1, seq, gw), lambda b, g: (b, 0, g)),
        out_shape=jax.ShapeDtypeStruct((bsz, seq, SSD_WIDTH), BF16),
        scratch_shapes=[pltpu.VMEM((pad, gw), F32), pltpu.VMEM((pad, ns), F32), pltpu.VMEM((pad, ns), F32),
                        pltpu.VMEM((seq, gw), F32), pltpu.VMEM((nch, ns, SSD_CHUNK), BF16),
                        pltpu.VMEM((seq, ns), BF16), pltpu.VMEM((seq, LANES), BF16),
                        pltpu.VMEM((nch, SSD_ROWS, SSD_CHUNK), F32), pltpu.VMEM((nch, SSD_ROWS, SSD_CHUNK), F32),
                        pltpu.VMEM((nch, ns, gw), BF16), pltpu.VMEM((nch, ns, gw), F32),
                        pltpu.VMEM((nch, 8, gw), F32),
                        pltpu.VMEM((ns, gw), F32), pltpu.VMEM((ns, gw), F32),
                        pltpu.VMEM((2, SSD_CHUNK, 2 * gw), F32), pltpu.VMEM((2, 32, 2 * gw), F32),
                        pltpu.VMEM((2, SSD_CHUNK, SSD_ROWS * SSD_CHUNK), F32),
                        pltpu.VMEM((2, SSD_CHUNK, SSD_CHUNK), F32), pltpu.VMEM((2, SSD_CHUNK, gw), F32)],
        compiler_params=_params(("parallel", "parallel")),
        name="ssd",
    )(slab, slab, slab, slab, dt, slab_ctx, slab_ctx, slab_ctx, dt_ctx,
      cwx, cwb, cwc, cbx, cbb, cbc, dtb, alog, dskip, nrm, sel_v, sel_w, sel_e)


def _rms(v, gain):
    return v * lax.rsqrt(jnp.mean(v * v, axis=-1, keepdims=True) + NORM_EPS) * gain


def _merge_kernel(ona_ref, y_ref, gna_ref, gssd_ref, x_ref, ga1_ref, sh2_ref, sc2_ref,
                  wna_ref, wssd_ref, wout_ref, npost_ref, npre_ref, wrh_ref, wrl_ref,
                  x1_ref, h2_ref, aff_ref):
    a = jnp.dot(ona_ref[0], wna_ref[...], preferred_element_type=F32)
    s = jnp.dot(y_ref[0], wssd_ref[...], preferred_element_type=F32)
    u = _sigmoid(gna_ref[0].astype(F32)) * a + _sigmoid(gssd_ref[0].astype(F32)) * s
    mix = jnp.dot(u.astype(BF16), wout_ref[...], preferred_element_type=F32)
    x1 = x_ref[0] + ga1_ref[0] * _rms(mix, npost_ref[...])
    x1_ref[0] = x1
    h2 = _rms(x1, npre_ref[...]) * (1.0 + sc2_ref[0]) + sh2_ref[0]
    hi = h2.astype(BF16)
    h2_ref[0] = hi
    lo = (h2 - hi.astype(F32)).astype(BF16)
    nt = (((1,), (1,)), ((), ()))
    wrh, wrl = wrh_ref[...], wrl_ref[...]
    logits = (lax.dot_general(wrh, hi, nt, preferred_element_type=F32)
              + lax.dot_general(wrh, lo, nt, preferred_element_type=F32)
              + lax.dot_general(wrl, hi, nt, preferred_element_type=F32))
    e = jnp.exp(logits - jnp.max(logits, axis=0, keepdims=True))
    aff_ref[0] = e / jnp.sum(e, axis=0, keepdims=True)


def _merge(o_na, y_ssd, slab, x, mod3, w_na, w_ssd, w_o, n_post, n_pre, wr_hi, wr_lo):
    bsz, seq, d = x.shape
    tm = 512
    ne = wr_hi.shape[0]
    tok = lambda width, blk: pl.BlockSpec((1, tm, width), lambda b, i: (b, i, blk))
    modv = lambda k: pl.BlockSpec((1, 1, d), lambda b, i: (b, 0, k))
    full = lambda r, c_: pl.BlockSpec((r, c_), lambda b, i: (0, 0))
    return pl.pallas_call(
        _merge_kernel,
        grid=(bsz, seq // tm),
        in_specs=[tok(d, 0), tok(SSD_WIDTH, 0), tok(d, COL_G // d), tok(d, COL_G // d + 1), tok(d, 0),
                  modv(2), modv(3), modv(4),
                  full(d, d), full(SSD_WIDTH, d), full(d, d), full(1, d), full(1, d), full(ne, d), full(ne, d)],
        out_specs=[tok(d, 0), tok(d, 0), pl.BlockSpec((1, ne, tm), lambda b, i: (b, 0, i))],
        out_shape=[jax.ShapeDtypeStruct((bsz, seq, d), F32), jax.ShapeDtypeStruct((bsz, seq, d), BF16),
                   jax.ShapeDtypeStruct((bsz, ne, seq), F32)],
        compiler_params=_params(("parallel", "parallel")),
        name="merge",
    )(o_na, y_ssd, slab, slab, x, mod3, mod3, mod3, w_na, w_ssd, w_o, n_post, n_pre, wr_hi, wr_lo)


def _prefix_count(mask_bf16, strict_upper):
    r, s = mask_bf16.shape
    offset = jnp.zeros((r, 1), F32)
    parts = []
    for j in range(s // LANES):
        seg = mask_bf16[:, j * LANES:(j + 1) * LANES]
        parts.append(jnp.dot(seg, strict_upper, preferred_element_type=F32) + offset)
        offset = offset + jnp.sum(seg.astype(F32), axis=1, keepdims=True)
    return jnp.concatenate(parts, axis=1)


def _route_kernel(aff_ref, h2_ref, pos_ref, xg_ref, gate_ref, *, cap):
    aff = aff_ref[0]
    bits = pltpu.bitcast(aff, jnp.int32)
    ne, t = aff.shape
    thr = jnp.zeros((ne, 1), jnp.int32)
    for bit in range(30, -1, -1):
        cand = thr | (1 << bit)
        cnt = jnp.sum(jnp.where(bits >= cand, 1.0, 0.0), axis=1, keepdims=True)
        thr = jnp.where(cnt >= cap, cand, thr)
    above = bits > thr
    tied = bits == thr
    need = cap - jnp.sum(jnp.where(above, 1.0, 0.0), axis=1, keepdims=True)
    ri = lax.broadcasted_iota(jnp.int32, (LANES, LANES), 0)
    ci = lax.broadcasted_iota(jnp.int32, (LANES, LANES), 1)
    strict_upper = jnp.where(ri < ci, 1.0, 0.0).astype(BF16)
    tie_rank = _prefix_count(jnp.where(tied, 1.0, 0.0).astype(BF16), strict_upper)
    sel = above | (tied & (tie_rank < need))
    slot = _prefix_count(jnp.where(sel, 1.0, 0.0).astype(BF16), strict_upper)
    slot = jnp.where(sel, slot, -1.0)
    pos = slot.astype(jnp.int32)
    padded = jnp.concatenate([slot, jnp.full((LANES - ne, t), -1.0, F32)], axis=0)
    for j in range(t // LANES):
        pos_ref[0, j * LANES:(j + 1) * LANES, :] = padded[:, j * LANES:(j + 1) * LANES].T.astype(jnp.int32)
    h2 = h2_ref[0]
    slot_id = lax.broadcasted_iota(jnp.int32, (cap, t), 0)
    for e in range(ne):
        hit = slot_id == pos[e:e + 1, :]
        onehot = jnp.where(hit, 1.0, 0.0).astype(BF16)
        xg_ref[e] = jnp.dot(onehot, h2, preferred_element_type=F32).astype(xg_ref.dtype)
        gate_ref[e] = jnp.sum(jnp.where(hit, aff[e:e + 1, :], 0.0), axis=1, keepdims=True)


def _route(aff_t, h2):
    bsz, ne, t = aff_t.shape
    d = h2.shape[2]
    cap = EC_CAPACITY_FACTOR * t // ne
    return pl.pallas_call(
        functools.partial(_route_kernel, cap=cap),
        grid=(bsz,),
        in_specs=[pl.BlockSpec((1, ne, t), lambda b: (b, 0, 0)),
                  pl.BlockSpec((1, t, d), lambda b: (b, 0, 0))],
        out_specs=[pl.BlockSpec((1, t, LANES), lambda b: (b, 0, 0)),
                   pl.BlockSpec((ne, cap, d), lambda b: (0, b, 0)),
                   pl.BlockSpec((ne, cap, 1), lambda b: (0, b, 0))],
        out_shape=[jax.ShapeDtypeStruct((bsz, t, LANES), jnp.int32),
                   jax.ShapeDtypeStruct((ne, bsz * cap, d), BF16),
                   jax.ShapeDtypeStruct((ne, bsz * cap, 1), F32)],
        compiler_params=_params(("parallel",)),
        name="route",
    )(aff_t, h2)


EXPERT_FF_TILE = 512
EXPERT_ROW_TILE = 512


def _expert_kernel(xg_ref, gate_ref, wg_ref, wu_ref, wd_ref, yo_ref, acc_ref):
    f = pl.program_id(1)

    @pl.when(f == 0)
    def _():
        acc_ref[...] = jnp.zeros(acc_ref.shape, F32)

    wg, wu, wd = wg_ref[0].astype(BF16), wu_ref[0].astype(BF16), wd_ref[0].astype(BF16)
    n = xg_ref.shape[1]
    rt = min(EXPERT_ROW_TILE, n)
    for r in range(n // rt):
        rows = pl.ds(r * rt, rt)
        xg = xg_ref[0, rows, :]
        hid = _silu(jnp.dot(xg, wg, preferred_element_type=F32)) * jnp.dot(xg, wu, preferred_element_type=F32)
        acc_ref[rows, :] += jnp.dot(hid.astype(BF16), wd, preferred_element_type=F32)

    @pl.when(f == pl.num_programs(1) - 1)
    def _():
        cap = yo_ref.shape[1]
        for b in range(yo_ref.shape[0]):
            rows = pl.ds(b * cap, cap)
            yo_ref[b] = (acc_ref[rows, :] * gate_ref[0, rows, :]).astype(yo_ref.dtype)


def _experts(xg, gate, w_g, w_u, w_d, bsz):
    ne, n, d = xg.shape
    cap = n // bsz
    ff = w_g.shape[2]
    tf = EXPERT_FF_TILE
    return pl.pallas_call(
        _expert_kernel,
        grid=(ne, ff // tf),
        in_specs=[pl.BlockSpec((1, n, d), lambda e, f: (e, 0, 0)),
                  pl.BlockSpec((1, n, 1), lambda e, f: (e, 0, 0)),
                  pl.BlockSpec((1, d, tf), lambda e, f: (e, 0, f)),
                  pl.BlockSpec((1, d, tf), lambda e, f: (e, 0, f)),
                  pl.BlockSpec((1, tf, d), lambda e, f: (e, f, 0))],
        out_specs=pl.BlockSpec((bsz, cap, d), lambda e, f: (0, e, 0)),
        out_shape=jax.ShapeDtypeStruct((bsz, ne * cap, d), BF16),
        scratch_shapes=[pltpu.VMEM((n, d), F32)],
        compiler_params=_params(("parallel", "arbitrary")),
        name="experts",
    )(xg, gate, w_g, w_u, w_d)


def _combine_kernel(pos_ref, yo_ref, x1_ref, ga2_ref, npost_ref, o_ref, *, cap, ne):
    pos = pos_ref[0]
    slot_id = lax.broadcasted_iota(jnp.int32, (1, cap), 1)
    onehot = jnp.concatenate(
        [jnp.where(pos[:, e:e + 1] == slot_id, 1.0, 0.0).astype(BF16) for e in range(ne)], axis=1)
    moe = jnp.dot(onehot, yo_ref[0], preferred_element_type=F32)
    o_ref[0] = x1_ref[0] + ga2_ref[0] * _rms(moe, npost_ref[...])


def _combine(pos_t, yo, x1, mod3, n_post, ne):
    bsz, seq, d = x1.shape
    cap = yo.shape[1] // ne
    tm = 512
    return pl.pallas_call(
        functools.partial(_combine_kernel, cap=cap, ne=ne),
        grid=(bsz, seq // tm),
        in_specs=[pl.BlockSpec((1, tm, LANES), lambda b, i: (b, i, 0)),
                  pl.BlockSpec((1, ne * cap, d), lambda b, i: (b, 0, 0)),
                  pl.BlockSpec((1, tm, d), lambda b, i: (b, i, 0)),
                  pl.BlockSpec((1, 1, d), lambda b, i: (b, 0, 5)),
                  pl.BlockSpec((1, d), lambda b, i: (0, 0))],
        out_specs=pl.BlockSpec((1, tm, d), lambda b, i: (b, i, 0)),
        out_shape=jax.ShapeDtypeStruct((bsz, seq, d), F32),
        compiler_params=_params(("parallel", "parallel")),
        name="combine",
    )(pos_t, yo, x1, mod3, n_post)


def kernel(x, c, ctx, c_ctx, w_ada, b_ada, norm_pre_mix, norm_post_mix, norm_pre_ffn, norm_post_ffn,
           w_in, na_rpb, ssd_conv_w, ssd_conv_b, ssd_a_log_fwd, ssd_a_log_bwd, ssd_dt_bias_fwd,
           ssd_dt_bias_bwd, ssd_d_skip, ssd_norm, w_branch_na, w_branch_ssd, w_out, w_router,
           w_exp_gate, w_exp_up, w_exp_down):
    mod, slab, o_na, y_ssd = _front(x, c, ctx, c_ctx, w_ada[0], b_ada[0], norm_pre_mix[0], w_in[0], na_rpb[0],
                                    ssd_conv_w[0], ssd_conv_b[0], ssd_a_log_fwd[0], ssd_a_log_bwd[0],
                                    ssd_dt_bias_fwd[0], ssd_dt_bias_bwd[0], ssd_d_skip[0], ssd_norm[0])
    return _back(x, mod, slab, o_na, y_ssd, norm_post_mix[0], norm_pre_ffn[0], norm_post_ffn[0],
                 w_branch_na[0], w_branch_ssd[0], w_out[0], w_router[0], w_exp_gate[0], w_exp_up[0], w_exp_down[0])


def _back(x, mod, slab, o_na, y_ssd, norm_post_mix, norm_pre_ffn, norm_post_ffn,
          w_branch_na, w_branch_ssd, w_out, w_router, w_exp_gate, w_exp_up, w_exp_down):
    bsz, seq, d = x.shape
    mod3 = mod.reshape(16, 1, N_MOD * d)
    wr_t = w_router.T
    wr_hi = wr_t.astype(BF16)
    wr_lo = (wr_t - wr_hi.astype(F32)).astype(BF16)
    x1, h2, aff_t = _merge(o_na, y_ssd, slab, x, mod3, w_branch_na.astype(BF16), w_branch_ssd.astype(BF16),
                           w_out.astype(BF16), norm_post_mix.reshape(1, d), norm_pre_ffn.reshape(1, d),
                           wr_hi, wr_lo)
    pos_t, xg, gate = _route(aff_t, h2)
    yo = _experts(xg, gate, w_exp_gate, w_exp_up, w_exp_down, bsz)
    return _combine(pos_t, yo, x1, mod3, norm_post_ffn.reshape(1, d), w_router.shape[1])


def _group_rows(fwd, bwd):
    rows = []
    for g in range(SSD_GROUPS):
        sl = slice(g * SSD_GHEADS, (g + 1) * SSD_GHEADS)
        rows += [fwd[sl], bwd[sl]]
    return jnp.concatenate(rows, axis=0)


def _front(x, c, ctx, c_ctx, w_ada, b_ada, norm_pre_mix, w_in, na_rpb, ssd_conv_w, ssd_conv_b,
           ssd_a_log_fwd, ssd_a_log_bwd, ssd_dt_bias_fwd, ssd_dt_bias_bwd, ssd_d_skip, ssd_norm):
    bsz, seq, d = x.shape
    tctx = ctx.shape[1]
    assert bsz <= 8
    cc =jnp.zeros((16, d), F32).at[:bsz].set(c).at[8].set(c_ctx)
    mod = _ada(cc, w_ada, b_ada)
    mod3 = mod.reshape(16, 1, N_MOD * d)

    dt0 = COL_G
    w_main = jnp.swapaxes(w_in, 0, 1).astype(BF16)
    w_gate = w_main[dt0 + 2 * SSD_HEADS:]
    main_tiles = dt0 // 1024
    w_dt = _group_rows(w_main[dt0:dt0 + SSD_HEADS], w_main[dt0 + SSD_HEADS:dt0 + 2 * SSD_HEADS])
    rope = _rope_tables(seq)
    g_pre = norm_pre_mix.reshape(1, d)

    slab, dt = _inproj(x, mod3, 0, g_pre, w_main, w_gate, w_dt, rope, 2, main_tiles, LAT_COLS // 1024, lambda j: j)
    nctx = bsz * tctx
    ctx_rows = -(-nctx // INPROJ_ROWS) * INPROJ_ROWS
    ctx_flat = ctx.reshape(1, nctx, d)
    if ctx_rows != nctx:
        ctx_flat = jnp.pad(ctx_flat, ((0, 0), (0, ctx_rows - nctx), (0, 0)))
    ctx_tile = lambda j: jnp.where(j < 2, j + COL_K // 1024, j - 2 + COL_X // 1024)
    ctx_tiles = CTX_COLS // 1024
    slab_c, dt_c = _inproj(ctx_flat, mod3, 8, g_pre, w_main, w_gate, w_dt, rope, 0, ctx_tiles, ctx_tiles, ctx_tile)
    slab_c = slab_c[0, :nctx].reshape(bsz, tctx, CTX_COLS)
    dt_c = jnp.swapaxes(dt_c[0, :, :nctx].reshape(-1, bsz, tctx), 0, 1)

    o_na = _na(slab, slab_c, _na_pair_table(na_rpb))

    nx, nb = SSD_WIDTH, SSD_GROUPS * SSD_STATE
    cw = (ssd_conv_w[:, :nx], ssd_conv_w[:, nx:nx + nb], ssd_conv_w[:, nx + nb:])
    cb = tuple(v.reshape(1, -1) for v in (ssd_conv_b[:nx], ssd_conv_b[nx:nx + nb], ssd_conv_b[nx + nb:]))
    lanes = lambda v: jnp.broadcast_to(v[:, None], (v.shape[0], LANES))
    dtb = _group_rows(lanes(ssd_dt_bias_fwd), lanes(ssd_dt_bias_bwd))
    alog = _group_rows(lanes(ssd_a_log_fwd), lanes(ssd_a_log_bwd))
    dskip = jnp.repeat(ssd_d_skip, SSD_HEAD_DIM).reshape(1, -1)
    y_ssd = _ssd(slab, dt, slab_c, dt_c, cw, cb, dtb, alog, dskip, ssd_norm.reshape(1, -1))
    return mod, slab, o_na, y_ssd
```

```python
import functools

import jax
import jax.numpy as jnp
import numpy as np
from jax import lax
from jax.experimental import pallas as pl
from jax.experimental.pallas import tpu as pltpu

F32 = jnp.float32
BF16 = jnp.bfloat16

GRID_W = 64
NORM_EPS = 1e-6
N_MOD = 6
NA_HEADS = 16
NA_HEAD_DIM = 64
NA_KH = 8
NA_KW = 16
ROPE_BASE = 10000.0
SSD_WIDTH = 2048
SSD_HEAD_DIM = 64
SSD_HEADS = 32
SSD_GROUPS = 4
SSD_STATE = 128
SSD_CONV = 5
SSD_CHUNK = 128
EC_CAPACITY_FACTOR = 2

VMEM_LIMIT_BYTES = 56 * 1024 * 1024
LANES = 128

COL_Q, COL_K, COL_V, COL_Z, COL_X, COL_B, COL_C, COL_G = 0, 1024, 2048, 3072, 5120, 7168, 7680, 8192
LAT_COLS = 10240
CTX_K, CTX_V, CTX_X, CTX_B, CTX_C = 0, 1024, 2048, 4096, 4608
CTX_COLS = 5120


def _params(semantics):
    return pltpu.CompilerParams(dimension_semantics=semantics, vmem_limit_bytes=VMEM_LIMIT_BYTES)


def _sigmoid(v):
    return 0.5 + 0.5 * jnp.tanh(0.5 * v)


def _silu(v):
    h = 0.5 * v
    return h + h * jnp.tanh(h)


def _softplus(v):
    return jnp.maximum(v, 0.0) + jnp.log1p(jnp.exp(-jnp.abs(v)))


def _ada_kernel(c_ref, w_ref, b_ref, o_ref):
    o_ref[...] = jnp.dot(_silu(c_ref[...]), w_ref[...], preferred_element_type=F32) + b_ref[...]


def _ada(cc, w_ada, b_ada):
    rows, d = cc.shape
    n = w_ada.shape[1]
    tn = 1536
    return pl.pallas_call(
        _ada_kernel,
        grid=(n // tn,),
        in_specs=[pl.BlockSpec((rows, d), lambda j: (0, 0)),
                  pl.BlockSpec((d, tn), lambda j: (0, j)),
                  pl.BlockSpec((1, tn), lambda j: (0, j))],
        out_specs=pl.BlockSpec((rows, tn), lambda j: (0, j)),
        out_shape=jax.ShapeDtypeStruct((rows, n), F32),
        compiler_params=_params(("arbitrary",)),
        name="ada",
    )(cc, w_ada, b_ada.reshape(1, n))


def _rope_tables(seq):
    lane = np.arange(LANES)
    axis = (lane % NA_HEAD_DIM) // 32
    within = lane % 32
    half = 16
    inv_freq = ROPE_BASE ** (-(within % half).astype(np.float64) / half)
    pos = np.arange(seq)
    coord = np.where(axis[None, :] == 0, (pos // GRID_W)[:, None], (pos % GRID_W)[:, None])
    ang = coord.astype(np.float32) * inv_freq.astype(np.float32)[None, :]
    cos, sin = jnp.cos(jnp.asarray(ang, F32)), jnp.sin(jnp.asarray(ang, F32))
    first = jnp.asarray(within < half)[None, :]
    return cos, jnp.where(first, -sin, 0.0), jnp.where(first, 0.0, sin)


def _inproj_kernel(x_ref, sh_ref, sc_ref, g_ref, w_ref, wgate_ref, wdt_ref, cos_ref, s1_ref, s2_ref,
                   o_ref, dt_ref, h_scr, *, rope_tiles, main_tiles):
    j = pl.program_id(2)

    @pl.when(j == 0)
    def _():
        x = x_ref[0]
        xn = x * lax.rsqrt(jnp.mean(x * x, axis=-1, keepdims=True) + NORM_EPS) * g_ref[...]
        h = (xn * (1.0 + sc_ref[0]) + sh_ref[0]).astype(BF16)
        h_scr[...] = h
        dt_ref[0] = lax.dot_general(wdt_ref[...], h, (((1,), (1,)), ((), ())), preferred_element_type=F32)

    h = h_scr[...]
    chunks = range(0, o_ref.shape[2], INPROJ_CHUNK)

    nt = (((1,), (1,)), ((), ()))

    def plain(weights):
        for c in chunks:
            acc = lax.dot_general(h, weights[c:c + INPROJ_CHUNK, :], nt, preferred_element_type=F32)
            o_ref[0, :, c:c + INPROJ_CHUNK] = acc.astype(o_ref.dtype)

    pl.when((j >= rope_tiles) & (j < main_tiles))(lambda: plain(w_ref))
    pl.when(j >= main_tiles)(lambda: plain(wgate_ref))

    if rope_tiles:
        @pl.when(j < rope_tiles)
        def _():
            scale = jnp.where(j == 0, NA_HEAD_DIM ** -0.5, 1.0).astype(F32)
            cos, s1, s2 = cos_ref[...] * scale, s1_ref[...] * scale, s2_ref[...] * scale
            for c in chunks:
                acc = lax.dot_general(h, w_ref[c:c + INPROJ_CHUNK, :], nt, preferred_element_type=F32)
                for l in range(0, INPROJ_CHUNK, LANES):
                    t = acc[:, l:l + LANES]
                    r = t * cos + pltpu.roll(t, LANES - 16, 1) * s1 + pltpu.roll(t, 16, 1) * s2
                    o_ref[0, :, c + l:c + l + LANES] = r.astype(o_ref.dtype)


INPROJ_CHUNK = 256
INPROJ_ROWS = 2048


def _inproj(x, mod3, mod_row0, g_pre, w, w_gate, w_dt, rope, rope_tiles, main_tiles, n_tiles, w_tile):
    bsz, seq, d = x.shape
    ndt = w_dt.shape[0]
    tm, tn = INPROJ_ROWS, 1024
    n = n_tiles * tn
    cos, s1, s2 = rope
    kern = functools.partial(_inproj_kernel, rope_tiles=rope_tiles, main_tiles=main_tiles)
    return pl.pallas_call(
        kern,
        grid=(bsz, seq // tm, n_tiles),
        in_specs=[pl.BlockSpec((1, tm, d), lambda b, i, j: (b, i, 0)),
                  pl.BlockSpec((1, 1, d), lambda b, i, j: (b + mod_row0, 0, 0)),
                  pl.BlockSpec((1, 1, d), lambda b, i, j: (b + mod_row0, 0, 1)),
                  pl.BlockSpec((1, d), lambda b, i, j: (0, 0)),
                  pl.BlockSpec((tn, d), lambda b, i, j: (w_tile(jnp.minimum(j, main_tiles - 1)), 0)),
                  pl.BlockSpec((tn, d), lambda b, i, j: (jnp.maximum(j - main_tiles, 0), 0)),
                  pl.BlockSpec((ndt, d), lambda b, i, j: (0, 0)),
                  pl.BlockSpec((tm, LANES), lambda b, i, j: (i, 0)),
                  pl.BlockSpec((tm, LANES), lambda b, i, j: (i, 0)),
                  pl.BlockSpec((tm, LANES), lambda b, i, j: (i, 0))],
        out_specs=[pl.BlockSpec((1, tm, tn), lambda b, i, j: (b, i, j)),
                   pl.BlockSpec((1, ndt, tm), lambda b, i, j: (b, 0, i))],
        out_shape=[jax.ShapeDtypeStruct((bsz, seq, n), BF16),
                   jax.ShapeDtypeStruct((bsz, ndt, seq), F32)],
        scratch_shapes=[pltpu.VMEM((tm, d), BF16)],
        compiler_params=_params(("parallel", "parallel", "arbitrary")),
        name="inproj",
    )(x, mod3, mod3, g_pre, w, w_gate, w_dt, cos, s1, s2)


NA_QROWS = 4
NA_KROWS = 12
NA_QBLK = NA_QROWS * GRID_W
NA_KBLK = NA_KROWS * GRID_W
NA_MASKED = -1e30


def _na_key_start(g, rows):
    return min(max(NA_QROWS * g - NA_KH // 2, 0), rows - NA_KROWS)


NA_TABLE_ROWS = 2 * NA_KH - 2 + 2 * NA_QROWS


def _na_pair_table(rpb):
    col = np.arange(GRID_W)
    col_start = np.clip(col - NA_KW // 2, 0, GRID_W - NA_KW)
    col_ok = (col[None, :] >= col_start[:, None]) & (col[None, :] < col_start[:, None] + NA_KW)
    col_idx = np.clip(col[None, :] - col[:, None] + NA_KW - 1, 0, 2 * NA_KW - 2)
    onehot = ((col_idx[None] == np.arange(2 * NA_KW - 1)[:, None, None]) & col_ok[None]).astype(np.float32)
    toe = jnp.einsum("hrc,cqk->hrqk", rpb, jnp.asarray(onehot), precision=lax.Precision.HIGHEST)
    toe = toe + jnp.asarray(np.where(col_ok, 0.0, NA_MASKED), F32)
    toe = jnp.pad(toe, ((0, 0), (NA_QROWS, NA_QROWS), (0, 0), (0, 0)))
    return jnp.concatenate([toe[:, :-1], toe[:, 1:]], axis=-1)


def _na_block_bias(table_ref, hh, g, rows):
    r = NA_QROWS * g + np.arange(NA_QROWS)
    kr = _na_key_start(g, rows) + np.arange(NA_KROWS)
    r0 = np.clip(r - NA_KH // 2, 0, rows - NA_KH)
    row_ok = (kr[None, :] >= r0[:, None]) & (kr[None, :] < r0[:, None] + NA_KH)
    first = int(kr[0] - r[0]) + NA_KH - 1 + NA_QROWS
    left = lax.broadcasted_iota(jnp.int32, (1, LANES), 1) < GRID_W
    masked = jnp.full((GRID_W, LANES), NA_MASKED, F32)
    out = []
    for dr in range(NA_QROWS):
        tiles = []
        for j in range(NA_KROWS // 2):
            ok0, ok1 = bool(row_ok[dr, 2 * j]), bool(row_ok[dr, 2 * j + 1])
            if not (ok0 or ok1):
                tiles.append(masked)
                continue
            t = table_ref[hh, first - dr + 2 * j]
            tiles.append(t if ok0 and ok1 else jnp.where(left == ok0, t, NA_MASKED))
        out.append(jnp.concatenate(tiles, axis=1))
    return jnp.concatenate(out, axis=0)


def _na_kernel(q_ref, k_ref, v_ref, kc_ref, vc_ref, table_ref, o_ref, s_scr, *, rows):
    nblk = rows // NA_QROWS
    first_head = lax.broadcasted_iota(jnp.int32, (1, LANES), 1) < NA_HEAD_DIM
    kc, vc = kc_ref[0], vc_ref[0]
    nt = (((1,), (1,)), ((), ()))
    ones = jnp.ones((1, LANES), BF16)
    units = [(g, hh) for g in range(nblk) for hh in range(2)]

    def scores(unit, slot):
        g, hh = unit
        k0 = _na_key_start(g, rows) * GRID_W
        q = q_ref[0, g * NA_QBLK:(g + 1) * NA_QBLK, :]
        qm = jnp.where(first_head if hh == 0 else ~first_head, q, jnp.zeros_like(q))
        s_scr[slot, :, :NA_KBLK] = (lax.dot_general(qm, k_ref[0, k0:k0 + NA_KBLK, :], nt, preferred_element_type=F32)
                                    + _na_block_bias(table_ref, hh, g, rows))
        s_scr[slot, :, NA_KBLK:] = lax.dot_general(qm, kc, nt, preferred_element_type=F32)

    scores(units[0], 0)
    prev = None
    for t, (g, hh) in enumerate(units):
        if t + 1 < len(units):
            scores(units[t + 1], (t + 1) % 2)
        mine = first_head if hh == 0 else ~first_head
        k0 = _na_key_start(g, rows) * GRID_W
        s = s_scr[t % 2]
        p = jnp.exp(s - jnp.max(s, axis=-1, keepdims=True)).astype(BF16)
        o = (jnp.dot(p[:, :NA_KBLK], jnp.where(mine, v_ref[0, k0:k0 + NA_KBLK, :], ones), preferred_element_type=F32)
             + jnp.dot(p[:, NA_KBLK:], jnp.where(mine, vc, ones), preferred_element_type=F32))
        o = o / pltpu.roll(o, NA_HEAD_DIM, 1)
        if hh == 0:
            prev = o
        else:
            o_ref[0, g * NA_QBLK:(g + 1) * NA_QBLK, :] = jnp.where(first_head, prev, o).astype(o_ref.dtype)


def _na(slab, slab_ctx, table):
    bsz, seq, _ = slab.shape
    tctx = slab_ctx.shape[1]
    pairs = NA_HEADS // 2
    blk = lambda col0: (lambda hp, b: (b, 0, col0 // LANES + hp))
    return pl.pallas_call(
        functools.partial(_na_kernel, rows=seq // GRID_W),
        grid=(pairs, bsz),
        in_specs=[pl.BlockSpec((1, seq, LANES), blk(COL_Q)),
                  pl.BlockSpec((1, seq, LANES), blk(COL_K)),
                  pl.BlockSpec((1, seq, LANES), blk(COL_V)),
                  pl.BlockSpec((1, tctx, LANES), blk(CTX_K)),
                  pl.BlockSpec((1, tctx, LANES), blk(CTX_V)),
                  pl.BlockSpec((2, NA_TABLE_ROWS, GRID_W, LANES), lambda hp, b: (hp, 0, 0, 0))],
        out_specs=pl.BlockSpec((1, seq, LANES), lambda hp, b: (b, 0, hp)),
        out_shape=jax.ShapeDtypeStruct((bsz, seq, NA_HEADS * NA_HEAD_DIM), BF16),
        scratch_shapes=[pltpu.VMEM((2, NA_QBLK, NA_KBLK + tctx), F32)],
        compiler_params=_params(("parallel", "parallel")),
        name="na",
    )(slab, slab, slab, slab_ctx, slab_ctx, table)


SSD_GHEADS = SSD_HEADS // SSD_GROUPS
SSD_GWIDTH = SSD_GHEADS * SSD_HEAD_DIM
CONV_HALO = 8


SSD_ROWS = 2 * SSD_GHEADS
PACK_V, PACK_W, PACK_E = 0, 3 * SSD_ROWS, 5 * SSD_ROWS


def _ssd_selectors():
    k = np.arange(LANES)[:, None]
    row = k % SSD_ROWS

    def sel(first, terms, head_of_col):
        live = (k >= first) & (k < first + terms * SSD_ROWS)
        return jnp.asarray((live & (row == head_of_col[None, :])).astype(np.float32), BF16)

    col = np.arange(2 * SSD_GWIDTH)
    head_dir = np.where(col < SSD_GWIDTH, col // SSD_HEAD_DIM, SSD_GHEADS + (col - SSD_GWIDTH) // SSD_HEAD_DIM)
    sel_v = sel(PACK_V, 3, np.arange(SSD_ROWS * SSD_CHUNK) // SSD_CHUNK)
    return sel_v, sel(PACK_W, 2, head_dir), sel(PACK_E, 2, head_dir)


def _ssd_kernel(xs_ref, b_ref, c_ref, z_ref, dt_ref, xsc_ref, bc_ref, cc_ref, dtc_ref,
                cwx_ref, cwb_ref, cwc_ref, cbx_ref, cbb_ref, cbc_ref, dtb_ref, alog_ref, dskip_ref, nrm_ref,
                selv_ref, selw_ref, sele_ref,
                y_ref,
                padx, padb, padc, xs_s, bt_s, c_s, pack_s, vrow_s, dtrow_s, hprev_s, sb_s, decb_s, hf_s, hb_s,
                wexp_s, dec_s, vcol_s, cb_s, yoff_s,
                *, seq, tctx):
    Q = SSD_CHUNK
    GH = SSD_GHEADS
    GW = SSD_GWIDTH
    rows = lax.broadcasted_iota(jnp.int32, (Q, Q), 0)
    cols = lax.broadcasted_iota(jnp.int32, (Q, Q), 1)
    cum_rhs = jnp.concatenate([jnp.where(rows <= cols, 1.0, 0.0), jnp.ones((Q, Q), F32)], axis=1).astype(BF16)
    is_fwd = lax.broadcasted_iota(jnp.int32, (SSD_ROWS, 1), 0) < GH
    first_head = lax.broadcasted_iota(jnp.int32, (1, LANES), 1) < SSD_HEAD_DIM
    a_coef = -jnp.exp(alog_ref[...])
    dt_bias = dtb_ref[...]

    def bf_terms(v, n):
        out, rem = [], v
        for _ in range(n):
            t = rem.astype(BF16).astype(F32)
            out.append(t)
            rem = rem - t
        return out

    def prep(x_raw, b_raw, c_raw, dt_raw, n, store_prev):
        for pad, raw in ((padx, x_raw), (padb, b_raw), (padc, c_raw)):
            width = pad.shape[1]
            pad[0:CONV_HALO, :] = jnp.zeros((CONV_HALO, width), F32)
            pad[CONV_HALO + n:2 * CONV_HALO + n, :] = jnp.zeros((CONV_HALO, width), F32)

        def stage(c, carry):
            r0 = pl.multiple_of(c * Q, Q)
            for pad, raw in ((padx, x_raw), (padb, b_raw), (padc, c_raw)):
                pad[pl.ds(r0 + CONV_HALO, Q), :] = raw[0, pl.ds(r0, Q), :].astype(F32)
            return carry

        lax.fori_loop(0, n // Q, stage, 0)

        def conv(pad, w_ref, bias_ref, r0):
            first = CONV_HALO - SSD_CONV // 2
            tiles = []
            for lo in range(0, pad.shape[1], LANES):
                win = pad[pl.ds(r0, Q + 2 * CONV_HALO), lo:lo + LANES]
                acc = bias_ref[:, lo:lo + LANES] + win[first:first + Q, :] * w_ref[0:1, lo:lo + LANES]
                for k in range(1, SSD_CONV):
                    acc = acc + win[first + k:first + k + Q, :] * w_ref[k:k + 1, lo:lo + LANES]
                tiles.append(_silu(acc))
            return tiles[0] if len(tiles) == 1 else jnp.concatenate(tiles, axis=1)

        def scalars(c, slot):
            r0 = pl.multiple_of(c * Q, Q)
            dt = _softplus(dt_raw[0, :, pl.ds(r0, Q)] + dt_bias)
            a = dt * a_coef
            cs = jnp.dot(jnp.concatenate([t.astype(BF16) for t in bf_terms(a, 3)], axis=0), cum_rhs,
                         preferred_element_type=F32)
            cs = cs[0:SSD_ROWS] + cs[SSD_ROWS:2 * SSD_ROWS] + cs[2 * SSD_ROWS:3 * SSD_ROWS]
            acs, tot = cs[:, :Q], cs[:, Q:]
            ex = acs - a
            v = jnp.where(is_fwd, acs, ex)
            w = dt * jnp.exp(jnp.where(is_fwd, tot - acs, ex))
            e = jnp.exp(jnp.where(is_fwd, acs, tot - ex))
            log_dt = jnp.log(dt)
            vrow_s[c] = jnp.where(is_fwd, v - log_dt, v + log_dt)
            dtrow_s[c] = jnp.concatenate([jnp.log(dt[:GH] + dt[GH:]), log_dt[GH:]], axis=0)
            packed = jnp.concatenate(bf_terms(v, 3) + bf_terms(w, 2) + bf_terms(e, 2)
                                     + [jnp.zeros((SSD_ROWS, Q), F32)], axis=0)
            pk = packed.T.astype(BF16)
            pack_s[pl.ds(r0, Q), :] = pk
            wexp_s[slot] = jnp.dot(pk, selw_ref[...], preferred_element_type=F32)
            edge = jnp.concatenate([pk[0:16, :], pk[Q - 16:Q, :]], axis=0)
            dec_s[slot] = jnp.dot(edge, sele_ref[...], preferred_element_type=F32)

        def states(c, slot):
            r0 = pl.multiple_of(c * Q, Q)
            b_t = conv(padb, cwb_ref, cbb_ref, r0).T.astype(BF16)
            bt_s[c] = b_t
            c_s[pl.ds(r0, Q), :] = conv(padc, cwc_ref, cbc_ref, r0).astype(BF16)
            xs = conv(padx, cwx_ref, cbx_ref, r0)
            xs_s[pl.ds(r0, Q), :] = xs
            wexp = wexp_s[slot]
            xdec = jnp.concatenate([xs * wexp[:, :GW], xs * wexp[:, GW:]], axis=1).astype(BF16)
            s = jnp.dot(b_t, xdec, preferred_element_type=F32)
            if store_prev:
                hprev_s[c] = hf_s[...].astype(BF16)
            hf_s[...] = dec_s[slot][31:32, :GW] * hf_s[...] + s[:, :GW]
            sb_s[c] = s[:, GW:]
            decb_s[c] = jnp.broadcast_to(dec_s[slot][0:1, GW:], (8, GW))

        nchunks = n // Q
        scalars(0, 0)

        def body(k, carry):
            c = 2 * k
            scalars(c + 1, 1)
            states(c, 0)
            scalars(jnp.minimum(c + 2, nchunks - 1), 0)
            states(c + 1, 1)
            return carry

        lax.fori_loop(0, nchunks // 2, body, 0)

    def backward_step(c):
        hb_s[...] = decb_s[c][0:1, :] * hb_s[...] + sb_s[c]

    hf_s[...] = jnp.zeros(hf_s.shape, F32)
    hb_s[...] = jnp.zeros(hb_s.shape, F32)

    prep(xsc_ref, bc_ref, cc_ref, dtc_ref, tctx, False)
    nctx = tctx // Q

    def ctx_back(i, carry):
        backward_step(nctx - 1 - i)
        return carry

    lax.fori_loop(0, nctx, ctx_back, 0)

    prep(xs_ref, b_ref, c_ref, dt_ref, seq, True)
    nch = seq // Q
    dskip = dskip_ref[...]
    gain = nrm_ref[...]

    def broadcasts(c, slot):
        r0 = pl.multiple_of(c * Q, Q)
        pk = pack_s[pl.ds(r0, Q), :]
        vcol_s[slot] = jnp.dot(pk, selv_ref[...], preferred_element_type=F32)
        eexp = jnp.dot(pk, sele_ref[...], preferred_element_type=F32)
        cm = c_s[pl.ds(r0, Q), :]
        cb_s[slot] = jnp.dot(cm, bt_s[c], preferred_element_type=F32)
        yoff_s[slot] = (jnp.dot(cm, hprev_s[c], preferred_element_type=F32) * eexp[:, :GW]
                        + jnp.dot(cm, hb_s[...].astype(BF16), preferred_element_type=F32) * eexp[:, GW:])

    def outputs(c, slot):
        r0 = pl.multiple_of(c * Q, Q)
        backward_step(c)
        broadcasts(jnp.maximum(c - 1, 0), 1 - slot)
        vcol, cb = vcol_s[slot], cb_s[slot]
        v_t, dt_t = vrow_s[c], dtrow_s[c]
        xs = xs_s[pl.ds(r0, Q), :]
        xsb = xs.astype(BF16)
        gate = _silu(z_ref[0, pl.ds(r0, Q), :].astype(F32))
        tiles = []
        for j in range(GH // 2):
            xpair = xsb[:, j * LANES:(j + 1) * LANES]
            ys = []
            for hh in range(2):
                hf, hb = 2 * j + hh, GH + 2 * j + hh
                arg_f = vcol[:, hf * Q:(hf + 1) * Q] - v_t[hf:hf + 1, :]
                arg_b = v_t[hb:hb + 1, :] - vcol[:, hb * Q:(hb + 1) * Q]
                diag = dt_t[hf:hf + 1, :]
                e = jnp.exp(jnp.where(rows > cols, arg_f, jnp.where(rows < cols, arg_b, diag)))
                m = (cb * e).astype(BF16)
                ys.append(jnp.dot(m, xpair, preferred_element_type=F32))
            tiles.append(jnp.where(first_head, ys[0], ys[1]))
        y = jnp.concatenate(tiles, axis=1) + yoff_s[slot] + dskip * xs
        u = y * gate
        u = u * lax.rsqrt(jnp.mean(u * u, axis=-1, keepdims=True) + NORM_EPS) * gain
        y_ref[0, pl.ds(r0, Q), :] = u.astype(y_ref.dtype)

    broadcasts(nch - 1, 1)

    def ybody(k, carry):
        c = nch - 1 - 2 * k
        outputs(c, 1)
        outputs(c - 1, 0)
        return carry

    lax.fori_loop(0, nch // 2, ybody, 0)


def _ssd(slab, dt, slab_ctx, dt_ctx, cw, cb, dtb, alog, dskip, nrm):
    bsz, seq, _ = slab.shape
    tctx = slab_ctx.shape[1]
    gw, ns = SSD_GWIDTH, SSD_STATE
    nch = seq // SSD_CHUNK
    cwx, cwb, cwc = cw
    cbx, cbb, cbc = cb

    def tok(width, col0, n):
        return pl.BlockSpec((1, n, width), lambda b, g: (b, 0, col0 // width + g))

    def par(rows, width):
        return pl.BlockSpec((rows, width), lambda b, g: (0, g))

    def head_rows(n):
        return pl.BlockSpec((1, SSD_ROWS, n), lambda b, g: (b, g, 0))

    def const(a):
        return pl.BlockSpec(a.shape, lambda b, g: (0, 0))

    sel_v, sel_w, sel_e = _ssd_selectors()
    pad = seq + 2 * CONV_HALO
    return pl.pallas_call(
        functools.partial(_ssd_kernel, seq=seq, tctx=tctx),
        grid=(bsz, SSD_GROUPS),
        in_specs=[tok(gw, COL_X, seq), tok(ns, COL_B, seq), tok(ns, COL_C, seq), tok(gw, COL_Z, seq),
                  head_rows(seq),
                  tok(gw, CTX_X, tctx), tok(ns, CTX_B, tctx), tok(ns, CTX_C, tctx), head_rows(tctx),
                  par(SSD_CONV, gw), par(SSD_CONV, ns), par(SSD_CONV, ns),
                  par(1, gw), par(1, ns), par(1, ns),
                  pl.BlockSpec((SSD_ROWS, LANES), lambda b, g: (g, 0)),
                  pl.BlockSpec((SSD_ROWS, LANES), lambda b, g: (g, 0)),
                  par(1, gw), par(1, gw), const(sel_v), const(sel_w), const(sel_e)],
        out_specs=pl.BlockSpec((1, seq, gw), lambda b, g: (b, 0, g)),
        out_shape=jax.ShapeDtypeStruct((bsz, seq, SSD_WIDTH), BF16),
        scratch_shapes=[pltpu.VMEM((pad, gw), F32), pltpu.VMEM((pad, ns), F32), pltpu.VMEM((pad, ns), F32),
                        pltpu.VMEM((seq, gw), F32), pltpu.VMEM((nch, ns, SSD_CHUNK), BF16),
                        pltpu.VMEM((seq, ns), BF16), pltpu.VMEM((seq, LANES), BF16),
                        pltpu.VMEM((nch, SSD_ROWS, SSD_CHUNK), F32), pltpu.VMEM((nch, SSD_ROWS, SSD_CHUNK), F32),
                        pltpu.VMEM((nch, ns, gw), BF16), pltpu.VMEM((nch, ns, gw), F32),
                        pltpu.VMEM((nch, 8, gw), F32),
                        pltpu.VMEM((ns, gw), F32), pltpu.VMEM((ns, gw), F32),
                        pltpu.VMEM((2, SSD_CHUNK, 2 * gw), F32), pltpu.VMEM((2, 32, 2 * gw), F32),
                        pltpu.VMEM((2, SSD_CHUNK, SSD_ROWS * SSD_CHUNK), F32),
                        pltpu.VMEM((2, SSD_CHUNK, SSD_CHUNK), F32), pltpu.VMEM((2, SSD_CHUNK, gw), F32)],
        compiler_params=_params(("parallel", "parallel")),
        name="ssd",
    )(slab, slab, slab, slab, dt, slab_ctx, slab_ctx, slab_ctx, dt_ctx,
      cwx, cwb, cwc, cbx, cbb, cbc, dtb, alog, dskip, nrm, sel_v, sel_w, sel_e)


def _rms(v, gain):
    return v * lax.rsqrt(jnp.mean(v * v, axis=-1, keepdims=True) + NORM_EPS) * gain


def _merge_kernel(ona_ref, y_ref, gna_ref, gssd_ref, x_ref, ga1_ref, sh2_ref, sc2_ref,
                  wna_ref, wssd_ref, wout_ref, npost_ref, npre_ref, wrh_ref, wrl_ref,
                  x1_ref, h2_ref, aff_ref):
    a = jnp.dot(ona_ref[0], wna_ref[...], preferred_element_type=F32)
    s = jnp.dot(y_ref[0], wssd_ref[...], preferred_element_type=F32)
    u = _sigmoid(gna_ref[0].astype(F32)) * a + _sigmoid(gssd_ref[0].astype(F32)) * s
    mix = jnp.dot(u.astype(BF16), wout_ref[...], preferred_element_type=F32)
    x1 = x_ref[0] + ga1_ref[0] * _rms(mix, npost_ref[...])
    x1_ref[0] = x1
    h2 = _rms(x1, npre_ref[...]) * (1.0 + sc2_ref[0]) + sh2_ref[0]
    hi = h2.astype(BF16)
    h2_ref[0] = hi
    lo = (h2 - hi.astype(F32)).astype(BF16)
    nt = (((1,), (1,)), ((), ()))
    wrh, wrl = wrh_ref[...], wrl_ref[...]
    logits = (lax.dot_general(wrh, hi, nt, preferred_element_type=F32)
              + lax.dot_general(wrh, lo, nt, preferred_element_type=F32)
              + lax.dot_general(wrl, hi, nt, preferred_element_type=F32))
    e = jnp.exp(logits - jnp.max(logits, axis=0, keepdims=True))
    aff_ref[0] = e / jnp.sum(e, axis=0, keepdims=True)


def _merge(o_na, y_ssd, slab, x, mod3, w_na, w_ssd, w_o, n_post, n_pre, wr_hi, wr_lo):
    bsz, seq, d = x.shape
    tm = 512
    ne = wr_hi.shape[0]
    tok = lambda width, blk: pl.BlockSpec((1, tm, width), lambda b, i: (b, i, blk))
    modv = lambda k: pl.BlockSpec((1, 1, d), lambda b, i: (b, 0, k))
    full = lambda r, c_: pl.BlockSpec((r, c_), lambda b, i: (0, 0))
    return pl.pallas_call(
        _merge_kernel,
        grid=(bsz, seq // tm),
        in_specs=[tok(d, 0), tok(SSD_WIDTH, 0), tok(d, COL_G // d), tok(d, COL_G // d + 1), tok(d, 0),
                  modv(2), modv(3), modv(4),
                  full(d, d), full(SSD_WIDTH, d), full(d, d), full(1, d), full(1, d), full(ne, d), full(ne, d)],
        out_specs=[tok(d, 0), tok(d, 0), pl.BlockSpec((1, ne, tm), lambda b, i: (b, 0, i))],
        out_shape=[jax.ShapeDtypeStruct((bsz, seq, d), F32), jax.ShapeDtypeStruct((bsz, seq, d), BF16),
                   jax.ShapeDtypeStruct((bsz, ne, seq), F32)],
        compiler_params=_params(("parallel", "parallel")),
        name="merge",
    )(o_na, y_ssd, slab, slab, x, mod3, mod3, mod3, w_na, w_ssd, w_o, n_post, n_pre, wr_hi, wr_lo)


def _prefix_count(mask_bf16, strict_upper):
    r, s = mask_bf16.shape
    offset = jnp.zeros((r, 1), F32)
    parts = []
    for j in range(s // LANES):
        seg = mask_bf16[:, j * LANES:(j + 1) * LANES]
        parts.append(jnp.dot(seg, strict_upper, preferred_element_type=F32) + offset)
        offset = offset + jnp.sum(seg.astype(F32), axis=1, keepdims=True)
    return jnp.concatenate(parts, axis=1)


def _route_kernel(aff_ref, h2_ref, pos_ref, xg_ref, gate_ref, *, cap):
    aff = aff_ref[0]
    bits = pltpu.bitcast(aff, jnp.int32)
    ne, t = aff.shape
    thr = jnp.zeros((ne, 1), jnp.int32)
    for bit in range(30, -1, -1):
        cand = thr | (1 << bit)
        cnt = jnp.sum(jnp.where(bits >= cand, 1.0, 0.0), axis=1, keepdims=True)
        thr = jnp.where(cnt >= cap, cand, thr)
    above = bits > thr
    tied = bits == thr
    need = cap - jnp.sum(jnp.where(above, 1.0, 0.0), axis=1, keepdims=True)
    ri = lax.broadcasted_iota(jnp.int32, (LANES, LANES), 0)
    ci = lax.broadcasted_iota(jnp.int32, (LANES, LANES), 1)
    strict_upper = jnp.where(ri < ci, 1.0, 0.0).astype(BF16)
    tie_rank = _prefix_count(jnp.where(tied, 1.0, 0.0).astype(BF16), strict_upper)
    sel = above | (tied & (tie_rank < need))
    slot = _prefix_count(jnp.where(sel, 1.0, 0.0).astype(BF16), strict_upper)
    slot = jnp.where(sel, slot, -1.0)
    pos = slot.astype(jnp.int32)
    padded = jnp.concatenate([slot, jnp.full((LANES - ne, t), -1.0, F32)], axis=0)
    for j in range(t // LANES):
        pos_ref[0, j * LANES:(j + 1) * LANES, :] = padded[:, j * LANES:(j + 1) * LANES].T.astype(jnp.int32)
    h2 = h2_ref[0]
    slot_id = lax.broadcasted_iota(jnp.int32, (cap, t), 0)
    for e in range(ne):
        hit = slot_id == pos[e:e + 1, :]
        onehot = jnp.where(hit, 1.0, 0.0).astype(BF16)
        xg_ref[e] = jnp.dot(onehot, h2, preferred_element_type=F32).astype(xg_ref.dtype)
        gate_ref[e] = jnp.sum(jnp.where(hit, aff[e:e + 1, :], 0.0), axis=1, keepdims=True)


def _route(aff_t, h2):
    bsz, ne, t = aff_t.shape
    d = h2.shape[2]
    cap = EC_CAPACITY_FACTOR * t // ne
    return pl.pallas_call(
        functools.partial(_route_kernel, cap=cap),
        grid=(bsz,),
        in_specs=[pl.BlockSpec((1, ne, t), lambda b: (b, 0, 0)),
                  pl.BlockSpec((1, t, d), lambda b: (b, 0, 0))],
        out_specs=[pl.BlockSpec((1, t, LANES), lambda b: (b, 0, 0)),
                   pl.BlockSpec((ne, cap, d), lambda b: (0, b, 0)),
                   pl.BlockSpec((ne, cap, 1), lambda b: (0, b, 0))],
        out_shape=[jax.ShapeDtypeStruct((bsz, t, LANES), jnp.int32),
                   jax.ShapeDtypeStruct((ne, bsz * cap, d), BF16),
                   jax.ShapeDtypeStruct((ne, bsz * cap, 1), F32)],
        compiler_params=_params(("parallel",)),
        name="route",
    )(aff_t, h2)


EXPERT_FF_TILE = 512
EXPERT_ROW_TILE = 512


def _expert_kernel(xg_ref, gate_ref, wg_ref, wu_ref, wd_ref, yo_ref, acc_ref):
    f = pl.program_id(1)

    @pl.when(f == 0)
    def _():
        acc_ref[...] = jnp.zeros(acc_ref.shape, F32)

    wg, wu, wd = wg_ref[0].astype(BF16), wu_ref[0].astype(BF16), wd_ref[0].astype(BF16)
    n = xg_ref.shape[1]
    rt = min(EXPERT_ROW_TILE, n)
    for r in range(n // rt):
        rows = pl.ds(r * rt, rt)
        xg = xg_ref[0, rows, :]
        hid = _silu(jnp.dot(xg, wg, preferred_element_type=F32)) * jnp.dot(xg, wu, preferred_element_type=F32)
        acc_ref[rows, :] += jnp.dot(hid.astype(BF16), wd, preferred_element_type=F32)

    @pl.when(f == pl.num_programs(1) - 1)
    def _():
        cap = yo_ref.shape[1]
        for b in range(yo_ref.shape[0]):
            rows = pl.ds(b * cap, cap)
            yo_ref[b] = (acc_ref[rows, :] * gate_ref[0, rows, :]).astype(yo_ref.dtype)


def _experts(xg, gate, w_g, w_u, w_d, bsz):
    ne, n, d = xg.shape
    cap = n // bsz
    ff = w_g.shape[2]
    tf = EXPERT_FF_TILE
    return pl.pallas_call(
        _expert_kernel,
        grid=(ne, ff // tf),
        in_specs=[pl.BlockSpec((1, n, d), lambda e, f: (e, 0, 0)),
                  pl.BlockSpec((1, n, 1), lambda e, f: (e, 0, 0)),
                  pl.BlockSpec((1, d, tf), lambda e, f: (e, 0, f)),
                  pl.BlockSpec((1, d, tf), lambda e, f: (e, 0, f)),
                  pl.BlockSpec((1, tf, d), lambda e, f: (e, f, 0))],
        out_specs=pl.BlockSpec((bsz, cap, d), lambda e, f: (0, e, 0)),
        out_shape=jax.ShapeDtypeStruct((bsz, ne * cap, d), BF16),
        scratch_shapes=[pltpu.VMEM((n, d), F32)],
        compiler_params=_params(("parallel", "arbitrary")),
        name="experts",
    )(xg, gate, w_g, w_u, w_d)


def _combine_kernel(pos_ref, yo_ref, x1_ref, ga2_ref, npost_ref, o_ref, *, cap, ne):
    pos = pos_ref[0]
    slot_id = lax.broadcasted_iota(jnp.int32, (1, cap), 1)
    onehot = jnp.concatenate(
        [jnp.where(pos[:, e:e + 1] == slot_id, 1.0, 0.0).astype(BF16) for e in range(ne)], axis=1)
    moe = jnp.dot(onehot, yo_ref[0], preferred_element_type=F32)
    o_ref[0] = x1_ref[0] + ga2_ref[0] * _rms(moe, npost_ref[...])


def _combine(pos_t, yo, x1, mod3, n_post, ne):
    bsz, seq, d = x1.shape
    cap = yo.shape[1] // ne
    tm = 512
    return pl.pallas_call(
        functools.partial(_combine_kernel, cap=cap, ne=ne),
        grid=(bsz, seq // tm),
        in_specs=[pl.BlockSpec((1, tm, LANES), lambda b, i: (b, i, 0)),
                  pl.BlockSpec((1, ne * cap, d), lambda b, i: (b, 0, 0)),
                  pl.BlockSpec((1, tm, d), lambda b, i: (b, i, 0)),
                  pl.BlockSpec((1, 1, d), lambda b, i: (b, 0, 5)),
                  pl.BlockSpec((1, d), lambda b, i: (0, 0))],
        out_specs=pl.BlockSpec((1, tm, d), lambda b, i: (b, i, 0)),
        out_shape=jax.ShapeDtypeStruct((bsz, seq, d), F32),
        compiler_params=_params(("parallel", "parallel")),
        name="combine",
    )(pos_t, yo, x1, mod3, n_post)


def kernel(x, c, ctx, c_ctx, w_ada, b_ada, norm_pre_mix, norm_post_mix, norm_pre_ffn, norm_post_ffn,
           w_in, na_rpb, ssd_conv_w, ssd_conv_b, ssd_a_log_fwd, ssd_a_log_bwd, ssd_dt_bias_fwd,
           ssd_dt_bias_bwd, ssd_d_skip, ssd_norm, w_branch_na, w_branch_ssd, w_out, w_router,
           w_exp_gate, w_exp_up, w_exp_down):
    assert w_ada.shape[0] == 1 and w_in.shape[0] == 1, "only depth 1 is implemented"
    assert x.shape[1] % (NA_QROWS * GRID_W) == 0 and x.shape[1] % INPROJ_ROWS == 0 and ctx.shape[1] % SSD_CHUNK == 0
    mod, slab, o_na, y_ssd = _front(x, c, ctx, c_ctx, w_ada[0], b_ada[0], norm_pre_mix[0], w_in[0], na_rpb[0],
                                    ssd_conv_w[0], ssd_conv_b[0], ssd_a_log_fwd[0], ssd_a_log_bwd[0],
                                    ssd_dt_bias_fwd[0], ssd_dt_bias_bwd[0], ssd_d_skip[0], ssd_norm[0])
    return _back(x, mod, slab, o_na, y_ssd, norm_post_mix[0], norm_pre_ffn[0], norm_post_ffn[0],
                 w_branch_na[0], w_branch_ssd[0], w_out[0], w_router[0], w_exp_gate[0], w_exp_up[0], w_exp_down[0])


def _back(x, mod, slab, o_na, y_ssd, norm_post_mix, norm_pre_ffn, norm_post_ffn,
          w_branch_na, w_branch_ssd, w_out, w_router, w_exp_gate, w_exp_up, w_exp_down):
    bsz, seq, d = x.shape
    mod3 = mod.reshape(16, 1, N_MOD * d)
    wr_t = w_router.T
    wr_hi = wr_t.astype(BF16)
    wr_lo = (wr_t - wr_hi.astype(F32)).astype(BF16)
    x1, h2, aff_t = _merge(o_na, y_ssd, slab, x, mod3, w_branch_na.astype(BF16), w_branch_ssd.astype(BF16),
                           w_out.astype(BF16), norm_post_mix.reshape(1, d), norm_pre_ffn.reshape(1, d),
                           wr_hi, wr_lo)
    pos_t, xg, gate = _route(aff_t, h2)
    yo = _experts(xg, gate, w_exp_gate, w_exp_up, w_exp_down, bsz)
    return _combine(pos_t, yo, x1, mod3, norm_post_ffn.reshape(1, d), w_router.shape[1])


def _group_rows(fwd, bwd):
    rows = []
    for g in range(SSD_GROUPS):
        sl = slice(g * SSD_GHEADS, (g + 1) * SSD_GHEADS)
        rows += [fwd[sl], bwd[sl]]
    return jnp.concatenate(rows, axis=0)


def _front(x, c, ctx, c_ctx, w_ada, b_ada, norm_pre_mix, w_in, na_rpb, ssd_conv_w, ssd_conv_b,
           ssd_a_log_fwd, ssd_a_log_bwd, ssd_dt_bias_fwd, ssd_dt_bias_bwd, ssd_d_skip, ssd_norm):
    bsz, seq, d = x.shape
    tctx = ctx.shape[1]
    assert bsz <= 8
    cc =jnp.zeros((16, d), F32).at[:bsz].set(c).at[8].set(c_ctx)
    mod = _ada(cc, w_ada, b_ada)
    mod3 = mod.reshape(16, 1, N_MOD * d)

    dt0 = COL_G
    w_main = jnp.swapaxes(w_in, 0, 1).astype(BF16)
    w_gate = w_main[dt0 + 2 * SSD_HEADS:]
    main_tiles = dt0 // 1024
    w_dt = _group_rows(w_main[dt0:dt0 + SSD_HEADS], w_main[dt0 + SSD_HEADS:dt0 + 2 * SSD_HEADS])
    rope = _rope_tables(seq)
    g_pre = norm_pre_mix.reshape(1, d)

    slab, dt = _inproj(x, mod3, 0, g_pre, w_main, w_gate, w_dt, rope, 2, main_tiles, LAT_COLS // 1024, lambda j: j)
    nctx = bsz * tctx
    ctx_rows = -(-nctx // INPROJ_ROWS) * INPROJ_ROWS
    ctx_flat = ctx.reshape(1, nctx, d)
    if ctx_rows != nctx:
        ctx_flat = jnp.pad(ctx_flat, ((0, 0), (0, ctx_rows - nctx), (0, 0)))
    ctx_tile = lambda j: jnp.where(j < 2, j + COL_K // 1024, j - 2 + COL_X // 1024)
    ctx_tiles = CTX_COLS // 1024
    slab_c, dt_c = _inproj(ctx_flat, mod3, 8, g_pre, w_main, w_gate, w_dt, rope, 0, ctx_tiles, ctx_tiles, ctx_tile)
    slab_c = slab_c[0, :nctx].reshape(bsz, tctx, CTX_COLS)
    dt_c = jnp.swapaxes(dt_c[0, :, :nctx].reshape(-1, bsz, tctx), 0, 1)

    o_na = _na(slab, slab_c, _na_pair_table(na_rpb))

    nx, nb = SSD_WIDTH, SSD_GROUPS * SSD_STATE
    cw = (ssd_conv_w[:, :nx], ssd_conv_w[:, nx:nx + nb], ssd_conv_w[:, nx + nb:])
    cb = tuple(v.reshape(1, -1) for v in (ssd_conv_b[:nx], ssd_conv_b[nx:nx + nb], ssd_conv_b[nx + nb:]))
    lanes = lambda v: jnp.broadcast_to(v[:, None], (v.shape[0], LANES))
    dtb = _group_rows(lanes(ssd_dt_bias_fwd), lanes(ssd_dt_bias_bwd))
    alog = _group_rows(lanes(ssd_a_log_fwd), lanes(ssd_a_log_bwd))
    dskip = jnp.repeat(ssd_d_skip, SSD_HEAD_DIM).reshape(1, -1)
    y_ssd = _ssd(slab, dt, slab_c, dt_c, cw, cb, dtb, alog, dskip, ssd_norm.reshape(1, -1))
    return mod, slab, o_na, y_ssd
```

```python
import functools

import jax
import jax.numpy as jnp
import numpy as np
from jax import lax
from jax.experimental import pallas as pl
from jax.experimental.pallas import tpu as pltpu

F32 = jnp.float32
BF16 = jnp.bfloat16

GRID_W = 64
NORM_EPS = 1e-6
N_MOD = 6
NA_HEADS = 16
NA_HEAD_DIM = 64
NA_KH = 8
NA_KW = 16
ROPE_BASE = 10000.0
SSD_WIDTH = 2048
SSD_HEAD_DIM = 64
SSD_HEADS = 32
SSD_GROUPS = 4
SSD_STATE = 128
SSD_CONV = 5
SSD_CHUNK = 128
EC_CAPACITY_FACTOR = 2

VMEM_LIMIT_BYTES = 56 * 1024 * 1024
LANES = 128

COL_Q, COL_K, COL_V, COL_Z, COL_X, COL_B, COL_C, COL_G = 0, 1024, 2048, 3072, 5120, 7168, 7680, 8192
LAT_COLS = 10240
CTX_K, CTX_V, CTX_X, CTX_B, CTX_C = 0, 1024, 2048, 4096, 4608
CTX_COLS = 5120


def _params(semantics):
    return pltpu.CompilerParams(dimension_semantics=semantics, vmem_limit_bytes=VMEM_LIMIT_BYTES)


def _sigmoid(v):
    return 0.5 + 0.5 * jnp.tanh(0.5 * v)


def _silu(v):
    h = 0.5 * v
    return h + h * jnp.tanh(h)


def _softplus(v):
    return jnp.maximum(v, 0.0) + jnp.log1p(jnp.exp(-jnp.abs(v)))


def _ada_kernel(c_ref, w_ref, b_ref, o_ref):
    o_ref[...] = jnp.dot(_silu(c_ref[...]), w_ref[...], preferred_element_type=F32) + b_ref[...]


def _ada(cc, w_ada, b_ada):
    rows, d = cc.shape
    n = w_ada.shape[1]
    tn = 1536
    return pl.pallas_call(
        _ada_kernel,
        grid=(n // tn,),
        in_specs=[pl.BlockSpec((rows, d), lambda j: (0, 0)),
                  pl.BlockSpec((d, tn), lambda j: (0, j)),
                  pl.BlockSpec((1, tn), lambda j: (0, j))],
        out_specs=pl.BlockSpec((rows, tn), lambda j: (0, j)),
        out_shape=jax.ShapeDtypeStruct((rows, n), F32),
        compiler_params=_params(("arbitrary",)),
        name="ada",
    )(cc, w_ada, b_ada.reshape(1, n))


def _rope_tables(seq):
    lane = np.arange(LANES)
    axis = (lane % NA_HEAD_DIM) // 32
    within = lane % 32
    half = 16
    inv_freq = ROPE_BASE ** (-(within % half).astype(np.float64) / half)
    pos = np.arange(seq)
    coord = np.where(axis[None, :] == 0, (pos // GRID_W)[:, None], (pos % GRID_W)[:, None])
    ang = coord.astype(np.float32) * inv_freq.astype(np.float32)[None, :]
    cos, sin = jnp.cos(jnp.asarray(ang, F32)), jnp.sin(jnp.asarray(ang, F32))
    first = jnp.asarray(within < half)[None, :]
    return cos, jnp.where(first, -sin, 0.0), jnp.where(first, 0.0, sin)


def _inproj_kernel(x_ref, sh_ref, sc_ref, g_ref, w_ref, wgate_ref, wdt_ref, cos_ref, s1_ref, s2_ref,
                   o_ref, dt_ref, h_scr, *, rope_tiles, main_tiles):
    j = pl.program_id(2)

    @pl.when(j == 0)
    def _():
        x = x_ref[0]
        xn = x * lax.rsqrt(jnp.mean(x * x, axis=-1, keepdims=True) + NORM_EPS) * g_ref[...]
        h = (xn * (1.0 + sc_ref[0]) + sh_ref[0]).astype(BF16)
        h_scr[...] = h
        dt_ref[0] = lax.dot_general(wdt_ref[...], h, (((1,), (1,)), ((), ())), preferred_element_type=F32)

    h = h_scr[...]
    chunks = range(0, o_ref.shape[2], INPROJ_CHUNK)

    nt = (((1,), (1,)), ((), ()))

    def plain(weights):
        for c in chunks:
            acc = lax.dot_general(h, weights[c:c + INPROJ_CHUNK, :], nt, preferred_element_type=F32)
            o_ref[0, :, c:c + INPROJ_CHUNK] = acc.astype(o_ref.dtype)

    pl.when((j >= rope_tiles) & (j < main_tiles))(lambda: plain(w_ref))
    pl.when(j >= main_tiles)(lambda: plain(wgate_ref))

    if rope_tiles:
        @pl.when(j < rope_tiles)
        def _():
            scale = jnp.where(j == 0, NA_HEAD_DIM ** -0.5, 1.0).astype(F32)
            cos, s1, s2 = cos_ref[...] * scale, s1_ref[...] * scale, s2_ref[...] * scale
            for c in chunks:
                acc = lax.dot_general(h, w_ref[c:c + INPROJ_CHUNK, :], nt, preferred_element_type=F32)
                for l in range(0, INPROJ_CHUNK, LANES):
                    t = acc[:, l:l + LANES]
                    r = t * cos + pltpu.roll(t, LANES - 16, 1) * s1 + pltpu.roll(t, 16, 1) * s2
                    o_ref[0, :, c + l:c + l + LANES] = r.astype(o_ref.dtype)


INPROJ_CHUNK = 256
INPROJ_ROWS = 2048


def _inproj(x, mod3, mod_row0, g_pre, w, w_gate, w_dt, rope, rope_tiles, main_tiles, n_tiles, w_tile):
    bsz, seq, d = x.shape
    ndt = w_dt.shape[0]
    tm, tn = INPROJ_ROWS, 1024
    n = n_tiles * tn
    cos, s1, s2 = rope
    kern = functools.partial(_inproj_kernel, rope_tiles=rope_tiles, main_tiles=main_tiles)
    return pl.pallas_call(
        kern,
        grid=(bsz, seq // tm, n_tiles),
        in_specs=[pl.BlockSpec((1, tm, d), lambda b, i, j: (b, i, 0)),
                  pl.BlockSpec((1, 1, d), lambda b, i, j: (b + mod_row0, 0, 0)),
                  pl.BlockSpec((1, 1, d), lambda b, i, j: (b + mod_row0, 0, 1)),
                  pl.BlockSpec((1, d), lambda b, i, j: (0, 0)),
                  pl.BlockSpec((tn, d), lambda b, i, j: (w_tile(jnp.minimum(j, main_tiles - 1)), 0)),
                  pl.BlockSpec((tn, d), lambda b, i, j: (jnp.maximum(j - main_tiles, 0), 0)),
                  pl.BlockSpec((ndt, d), lambda b, i, j: (0, 0)),
                  pl.BlockSpec((tm, LANES), lambda b, i, j: (i, 0)),
                  pl.BlockSpec((tm, LANES), lambda b, i, j: (i, 0)),
                  pl.BlockSpec((tm, LANES), lambda b, i, j: (i, 0))],
        out_specs=[pl.BlockSpec((1, tm, tn), lambda b, i, j: (b, i, j)),
                   pl.BlockSpec((1, ndt, tm), lambda b, i, j: (b, 0, i))],
        out_shape=[jax.ShapeDtypeStruct((bsz, seq, n), BF16),
                   jax.ShapeDtypeStruct((bsz, ndt, seq), F32)],
        scratch_shapes=[pltpu.VMEM((tm, d), BF16)],
        compiler_params=_params(("parallel", "parallel", "arbitrary")),
        name="inproj",
    )(x, mod3, mod3, g_pre, w, w_gate, w_dt, cos, s1, s2)


NA_QROWS = 4
NA_KROWS = 12
NA_QBLK = NA_QROWS * GRID_W
NA_KBLK = NA_KROWS * GRID_W
NA_MASKED = -1e30


def _na_key_start(g, rows):
    return min(max(NA_QROWS * g - NA_KH // 2, 0), rows - NA_KROWS)


NA_TABLE_ROWS = 2 * NA_KH - 2 + 2 * NA_QROWS


def _na_pair_table(rpb):
    col = np.arange(GRID_W)
    col_start = np.clip(col - NA_KW // 2, 0, GRID_W - NA_KW)
    col_ok = (col[None, :] >= col_start[:, None]) & (col[None, :] < col_start[:, None] + NA_KW)
    col_idx = np.clip(col[None, :] - col[:, None] + NA_KW - 1, 0, 2 * NA_KW - 2)
    onehot = ((col_idx[None] == np.arange(2 * NA_KW - 1)[:, None, None]) & col_ok[None]).astype(np.float32)
    toe = jnp.einsum("hrc,cqk->hrqk", rpb, jnp.asarray(onehot), precision=lax.Precision.HIGHEST)
    toe = toe + jnp.asarray(np.where(col_ok, 0.0, NA_MASKED), F32)
    toe = jnp.pad(toe, ((0, 0), (NA_QROWS, NA_QROWS), (0, 0), (0, 0)))
    return jnp.concatenate([toe[:, :-1], toe[:, 1:]], axis=-1)


def _na_block_bias(table_ref, hh, g, rows):
    r = NA_QROWS * g + np.arange(NA_QROWS)
    kr = _na_key_start(g, rows) + np.arange(NA_KROWS)
    r0 = np.clip(r - NA_KH // 2, 0, rows - NA_KH)
    row_ok = (kr[None, :] >= r0[:, None]) & (kr[None, :] < r0[:, None] + NA_KH)
    first = int(kr[0] - r[0]) + NA_KH - 1 + NA_QROWS
    left = lax.broadcasted_iota(jnp.int32, (1, LANES), 1) < GRID_W
    masked = jnp.full((GRID_W, LANES), NA_MASKED, F32)
    out = []
    for dr in range(NA_QROWS):
        tiles = []
        for j in range(NA_KROWS // 2):
            ok0, ok1 = bool(row_ok[dr, 2 * j]), bool(row_ok[dr, 2 * j + 1])
            if not (ok0 or ok1):
                tiles.append(masked)
                continue
            t = table_ref[hh, first - dr + 2 * j]
            tiles.append(t if ok0 and ok1 else jnp.where(left == ok0, t, NA_MASKED))
        out.append(jnp.concatenate(tiles, axis=1))
    return jnp.concatenate(out, axis=0)


def _na_kernel(q_ref, k_ref, v_ref, kc_ref, vc_ref, table_ref, o_ref, s_scr, *, rows):
    nblk = rows // NA_QROWS
    first_head = lax.broadcasted_iota(jnp.int32, (1, LANES), 1) < NA_HEAD_DIM
    kc, vc = kc_ref[0], vc_ref[0]
    nt = (((1,), (1,)), ((), ()))
    ones = jnp.ones((1, LANES), BF16)
    units = [(g, hh) for g in range(nblk) for hh in range(2)]

    def scores(unit, slot):
        g, hh = unit
        k0 = _na_key_start(g, rows) * GRID_W
        q = q_ref[0, g * NA_QBLK:(g + 1) * NA_QBLK, :]
        qm = jnp.where(first_head if hh == 0 else ~first_head, q, jnp.zeros_like(q))
        s_scr[slot, :, :NA_KBLK] = (lax.dot_general(qm, k_ref[0, k0:k0 + NA_KBLK, :], nt, preferred_element_type=F32)
                                    + _na_block_bias(table_ref, hh, g, rows))
        s_scr[slot, :, NA_KBLK:] = lax.dot_general(qm, kc, nt, preferred_element_type=F32)

    scores(units[0], 0)
    prev = None
    for t, (g, hh) in enumerate(units):
        if t + 1 < len(units):
            scores(units[t + 1], (t + 1) % 2)
        mine = first_head if hh == 0 else ~first_head
        k0 = _na_key_start(g, rows) * GRID_W
        s = s_scr[t % 2]
        p = jnp.exp(s - jnp.max(s, axis=-1, keepdims=True)).astype(BF16)
        vals = jnp.concatenate([v_ref[0, k0:k0 + NA_KBLK, :], vc], axis=0)
        o = jnp.dot(p, jnp.where(mine, vals, ones), preferred_element_type=F32)
        o = o / pltpu.roll(o, NA_HEAD_DIM, 1)
        if hh == 0:
            prev = o
        else:
            o_ref[0, g * NA_QBLK:(g + 1) * NA_QBLK, :] = jnp.where(first_head, prev, o).astype(o_ref.dtype)


def _na(slab, slab_ctx, table):
    bsz, seq, _ = slab.shape
    tctx = slab_ctx.shape[1]
    pairs = NA_HEADS // 2
    blk = lambda col0: (lambda hp, b: (b, 0, col0 // LANES + hp))
    return pl.pallas_call(
        functools.partial(_na_kernel, rows=seq // GRID_W),
        grid=(pairs, bsz),
        in_specs=[pl.BlockSpec((1, seq, LANES), blk(COL_Q)),
                  pl.BlockSpec((1, seq, LANES), blk(COL_K)),
                  pl.BlockSpec((1, seq, LANES), blk(COL_V)),
                  pl.BlockSpec((1, tctx, LANES), blk(CTX_K)),
                  pl.BlockSpec((1, tctx, LANES), blk(CTX_V)),
                  pl.BlockSpec((2, NA_TABLE_ROWS, GRID_W, LANES), lambda hp, b: (hp, 0, 0, 0))],
        out_specs=pl.BlockSpec((1, seq, LANES), lambda hp, b: (b, 0, hp)),
        out_shape=jax.ShapeDtypeStruct((bsz, seq, NA_HEADS * NA_HEAD_DIM), BF16),
        scratch_shapes=[pltpu.VMEM((2, NA_QBLK, NA_KBLK + tctx), F32)],
        compiler_params=_params(("parallel", "parallel")),
        name="na",
    )(slab, slab, slab, slab_ctx, slab_ctx, table)


SSD_GHEADS = SSD_HEADS // SSD_GROUPS
SSD_GWIDTH = SSD_GHEADS * SSD_HEAD_DIM
CONV_HALO = 8


SSD_ROWS = 2 * SSD_GHEADS
PACK_V, PACK_W, PACK_E = 0, 3 * SSD_ROWS, 5 * SSD_ROWS


def _ssd_selectors():
    k = np.arange(LANES)[:, None]
    row = k % SSD_ROWS

    def sel(first, terms, head_of_col):
        live = (k >= first) & (k < first + terms * SSD_ROWS)
        return jnp.asarray((live & (row == head_of_col[None, :])).astype(np.float32), BF16)

    col = np.arange(2 * SSD_GWIDTH)
    head_dir = np.where(col < SSD_GWIDTH, col // SSD_HEAD_DIM, SSD_GHEADS + (col - SSD_GWIDTH) // SSD_HEAD_DIM)
    sel_v = sel(PACK_V, 3, np.arange(SSD_ROWS * SSD_CHUNK) // SSD_CHUNK)
    return sel_v, sel(PACK_W, 2, head_dir), sel(PACK_E, 2, head_dir)


def _ssd_kernel(xs_ref, b_ref, c_ref, z_ref, dt_ref, xsc_ref, bc_ref, cc_ref, dtc_ref,
                cwx_ref, cwb_ref, cwc_ref, cbx_ref, cbb_ref, cbc_ref, dtb_ref, alog_ref, dskip_ref, nrm_ref,
                selv_ref, selw_ref, sele_ref,
                y_ref,
                padx, padb, padc, xs_s, bt_s, c_s, pack_s, vrow_s, dtrow_s, hprev_s, sb_s, decb_s, hf_s, hb_s,
                wexp_s, dec_s, vcol_s, cb_s, yoff_s,
                *, seq, tctx):
    Q = SSD_CHUNK
    GH = SSD_GHEADS
    GW = SSD_GWIDTH
    rows = lax.broadcasted_iota(jnp.int32, (Q, Q), 0)
    cols = lax.broadcasted_iota(jnp.int32, (Q, Q), 1)
    cum_rhs = jnp.concatenate([jnp.where(rows <= cols, 1.0, 0.0), jnp.ones((Q, Q), F32)], axis=1).astype(BF16)
    is_fwd = lax.broadcasted_iota(jnp.int32, (SSD_ROWS, 1), 0) < GH
    first_head = lax.broadcasted_iota(jnp.int32, (1, LANES), 1) < SSD_HEAD_DIM
    a_coef = -jnp.exp(alog_ref[...])
    dt_bias = dtb_ref[...]

    def bf_terms(v, n):
        out, rem = [], v
        for _ in range(n):
            t = rem.astype(BF16).astype(F32)
            out.append(t)
            rem = rem - t
        return out

    def prep(x_raw, b_raw, c_raw, dt_raw, n, store_prev):
        for pad, raw in ((padx, x_raw), (padb, b_raw), (padc, c_raw)):
            width = pad.shape[1]
            pad[0:CONV_HALO, :] = jnp.zeros((CONV_HALO, width), F32)
            pad[CONV_HALO + n:2 * CONV_HALO + n, :] = jnp.zeros((CONV_HALO, width), F32)

        def stage(c, carry):
            r0 = pl.multiple_of(c * Q, Q)
            for pad, raw in ((padx, x_raw), (padb, b_raw), (padc, c_raw)):
                pad[pl.ds(r0 + CONV_HALO, Q), :] = raw[0, pl.ds(r0, Q), :].astype(F32)
            return carry

        lax.fori_loop(0, n // Q, stage, 0)

        def conv(pad, w_ref, bias_ref, r0):
            first = CONV_HALO - SSD_CONV // 2
            tiles = []
            for lo in range(0, pad.shape[1], LANES):
                win = pad[pl.ds(r0, Q + 2 * CONV_HALO), lo:lo + LANES]
                acc = bias_ref[:, lo:lo + LANES] + win[first:first + Q, :] * w_ref[0:1, lo:lo + LANES]
                for k in range(1, SSD_CONV):
                    acc = acc + win[first + k:first + k + Q, :] * w_ref[k:k + 1, lo:lo + LANES]
                tiles.append(_silu(acc))
            return tiles[0] if len(tiles) == 1 else jnp.concatenate(tiles, axis=1)

        def scalars(c, slot):
            r0 = pl.multiple_of(c * Q, Q)
            dt = _softplus(dt_raw[0, :, pl.ds(r0, Q)] + dt_bias)
            a = dt * a_coef
            cs = jnp.dot(jnp.concatenate([t.astype(BF16) for t in bf_terms(a, 3)], axis=0), cum_rhs,
                         preferred_element_type=F32)
            cs = cs[0:SSD_ROWS] + cs[SSD_ROWS:2 * SSD_ROWS] + cs[2 * SSD_ROWS:3 * SSD_ROWS]
            acs, tot = cs[:, :Q], cs[:, Q:]
            ex = acs - a
            v = jnp.where(is_fwd, acs, ex)
            w = dt * jnp.exp(jnp.where(is_fwd, tot - acs, ex))
            e = jnp.exp(jnp.where(is_fwd, acs, tot - ex))
            log_dt = jnp.log(dt)
            vrow_s[c] = jnp.where(is_fwd, v - log_dt, v + log_dt)
            dtrow_s[c] = jnp.concatenate([jnp.log(dt[:GH] + dt[GH:]), log_dt[GH:]], axis=0)
            packed = jnp.concatenate(bf_terms(v, 3) + bf_terms(w, 2) + bf_terms(e, 2)
                                     + [jnp.zeros((SSD_ROWS, Q), F32)], axis=0)
            pk = packed.T.astype(BF16)
            pack_s[pl.ds(r0, Q), :] = pk
            wexp_s[slot] = jnp.dot(pk, selw_ref[...], preferred_element_type=F32)
            edge = jnp.concatenate([pk[0:16, :], pk[Q - 16:Q, :]], axis=0)
            dec_s[slot] = jnp.dot(edge, sele_ref[...], preferred_element_type=F32)

        def states(c, slot):
            r0 = pl.multiple_of(c * Q, Q)
            b_t = conv(padb, cwb_ref, cbb_ref, r0).T.astype(BF16)
            bt_s[c] = b_t
            c_s[pl.ds(r0, Q), :] = conv(padc, cwc_ref, cbc_ref, r0).astype(BF16)
            xs = conv(padx, cwx_ref, cbx_ref, r0)
            xs_s[pl.ds(r0, Q), :] = xs
            wexp = wexp_s[slot]
            xdec = jnp.concatenate([xs * wexp[:, :GW], xs * wexp[:, GW:]], axis=1).astype(BF16)
            s = jnp.dot(b_t, xdec, preferred_element_type=F32)
            if store_prev:
                hprev_s[c] = hf_s[...].astype(BF16)
            hf_s[...] = dec_s[slot][31:32, :GW] * hf_s[...] + s[:, :GW]
            sb_s[c] = s[:, GW:]
            decb_s[c] = jnp.broadcast_to(dec_s[slot][0:1, GW:], (8, GW))

        nchunks = n // Q
        scalars(0, 0)

        def body(k, carry):
            c = 2 * k
            scalars(c + 1, 1)
            states(c, 0)
            scalars(jnp.minimum(c + 2, nchunks - 1), 0)
            states(c + 1, 1)
            return carry

        lax.fori_loop(0, nchunks // 2, body, 0)

    def backward_step(c):
        hb_s[...] = decb_s[c][0:1, :] * hb_s[...] + sb_s[c]

    hf_s[...] = jnp.zeros(hf_s.shape, F32)
    hb_s[...] = jnp.zeros(hb_s.shape, F32)

    prep(xsc_ref, bc_ref, cc_ref, dtc_ref, tctx, False)
    nctx = tctx // Q

    def ctx_back(i, carry):
        backward_step(nctx - 1 - i)
        return carry

    lax.fori_loop(0, nctx, ctx_back, 0)

    prep(xs_ref, b_ref, c_ref, dt_ref, seq, True)
    nch = seq // Q
    dskip = dskip_ref[...]
    gain = nrm_ref[...]

    def broadcasts(c, slot):
        r0 = pl.multiple_of(c * Q, Q)
        pk = pack_s[pl.ds(r0, Q), :]
        vcol_s[slot] = jnp.dot(pk, selv_ref[...], preferred_element_type=F32)
        eexp = jnp.dot(pk, sele_ref[...], preferred_element_type=F32)
        cm = c_s[pl.ds(r0, Q), :]
        cb_s[slot] = jnp.dot(cm, bt_s[c], preferred_element_type=F32)
        yoff_s[slot] = (jnp.dot(cm, hprev_s[c], preferred_element_type=F32) * eexp[:, :GW]
                        + jnp.dot(cm, hb_s[...].astype(BF16), preferred_element_type=F32) * eexp[:, GW:])

    def outputs(c, slot):
        r0 = pl.multiple_of(c * Q, Q)
        backward_step(c)
        broadcasts(jnp.maximum(c - 1, 0), 1 - slot)
        vcol, cb = vcol_s[slot], cb_s[slot]
        v_t, dt_t = vrow_s[c], dtrow_s[c]
        xs = xs_s[pl.ds(r0, Q), :]
        xsb = xs.astype(BF16)
        gate = _silu(z_ref[0, pl.ds(r0, Q), :].astype(F32))
        tiles = []
        for j in range(GH // 2):
            xpair = xsb[:, j * LANES:(j + 1) * LANES]
            ys = []
            for hh in range(2):
                hf, hb = 2 * j + hh, GH + 2 * j + hh
                arg_f = vcol[:, hf * Q:(hf + 1) * Q] - v_t[hf:hf + 1, :]
                arg_b = v_t[hb:hb + 1, :] - vcol[:, hb * Q:(hb + 1) * Q]
                diag = dt_t[hf:hf + 1, :]
                e = jnp.exp(jnp.where(rows > cols, arg_f, jnp.where(rows < cols, arg_b, diag)))
                m = (cb * e).astype(BF16)
                ys.append(jnp.dot(m, xpair, preferred_element_type=F32))
            tiles.append(jnp.where(first_head, ys[0], ys[1]))
        y = jnp.concatenate(tiles, axis=1) + yoff_s[slot] + dskip * xs
        u = y * gate
        u = u * lax.rsqrt(jnp.mean(u * u, axis=-1, keepdims=True) + NORM_EPS) * gain
        y_ref[0, pl.ds(r0, Q), :] = u.astype(y_ref.dtype)

    broadcasts(nch - 1, 1)

    def ybody(k, carry):
        c = nch - 1 - 2 * k
        outputs(c, 1)
        outputs(c - 1, 0)
        return carry

    lax.fori_loop(0, nch // 2, ybody, 0)


def _ssd(slab, dt, slab_ctx, dt_ctx, cw, cb, dtb, alog, dskip, nrm):
    bsz, seq, _ = slab.shape
    tctx = slab_ctx.shape[1]
    gw, ns = SSD_GWIDTH, SSD_STATE
    nch = seq // SSD_CHUNK
    cwx, cwb, cwc = cw
    cbx, cbb, cbc = cb

    def tok(width, col0, n):
        return pl.BlockSpec((1, n, width), lambda b, g: (b, 0, col0 // width + g))

    def par(rows, width):
        return pl.BlockSpec((rows, width), lambda b, g: (0, g))

    def head_rows(n):
        return pl.BlockSpec((1, SSD_ROWS, n), lambda b, g: (b, g, 0))

    def const(a):
        return pl.BlockSpec(a.shape, lambda b, g: (0, 0))

    sel_v, sel_w, sel_e = _ssd_selectors()
    pad = seq + 2 * CONV_HALO
    return pl.pallas_call(
        functools.partial(_ssd_kernel, seq=seq, tctx=tctx),
        grid=(bsz, SSD_GROUPS),
        in_specs=[tok(gw, COL_X, seq), tok(ns, COL_B, seq), tok(ns, COL_C, seq), tok(gw, COL_Z, seq),
                  head_rows(seq),
                  tok(gw, CTX_X, tctx), tok(ns, CTX_B, tctx), tok(ns, CTX_C, tctx), head_rows(tctx),
                  par(SSD_CONV, gw), par(SSD_CONV, ns), par(SSD_CONV, ns),
                  par(1, gw), par(1, ns), par(1, ns),
                  pl.BlockSpec((SSD_ROWS, LANES), lambda b, g: (g, 0)),
                  pl.BlockSpec((SSD_ROWS, LANES), lambda b, g: (g, 0)),
                  par(1, gw), par(1, gw), const(sel_v), const(sel_w), const(sel_e)],
        out_specs=pl.BlockSpec((1, seq, gw), lambda b, g: (b, 0, g)),
        out_shape=jax.ShapeDtypeStruct((bsz, seq, SSD_WIDTH), BF16),
        scratch_shapes=[pltpu.VMEM((pad, gw), F32), pltpu.VMEM((pad, ns), F32), pltpu.VMEM((pad, ns), F32),
                        pltpu.VMEM((seq, gw), F32), pltpu.VMEM((nch, ns, SSD_CHUNK), BF16),
                        pltpu.VMEM((seq, ns), BF16), pltpu.VMEM((seq, LANES), BF16),
                        pltpu.VMEM((nch, SSD_ROWS, SSD_CHUNK), F32), pltpu.VMEM((nch, SSD_ROWS, SSD_CHUNK), F32),
                        pltpu.VMEM((nch, ns, gw), BF16), pltpu.VMEM((nch, ns, gw), F32),
                        pltpu.VMEM((nch, 8, gw), F32),
                        pltpu.VMEM((ns, gw), F32), pltpu.VMEM((ns, gw), F32),
                        pltpu.VMEM((2, SSD_CHUNK, 2 * gw), F32), pltpu.VMEM((2, 32, 2 * gw), F32),
                        pltpu.VMEM((2, SSD_CHUNK, SSD_ROWS * SSD_CHUNK), F32),
                        pltpu.VMEM((2, SSD_CHUNK, SSD_CHUNK), F32), pltpu.VMEM((2, SSD_CHUNK, gw), F32)],
        compiler_params=_params(("parallel", "parallel")),
        name="ssd",
    )(slab, slab, slab, slab, dt, slab_ctx, slab_ctx, slab_ctx, dt_ctx,
      cwx, cwb, cwc, cbx, cbb, cbc, dtb, alog, dskip, nrm, sel_v, sel_w, sel_e)


def _rms(v, gain):
    return v * lax.rsqrt(jnp.mean(v * v, axis=-1, keepdims=True) + NORM_EPS) * gain


def _merge_kernel(ona_ref, y_ref, gna_ref, gssd_ref, x_ref, ga1_ref, sh2_ref, sc2_ref,
                  wna_ref, wssd_ref, wout_ref, npost_ref, npre_ref, wrh_ref, wrl_ref,
                  x1_ref, h2_ref, aff_ref):
    a = jnp.dot(ona_ref[0], wna_ref[...], preferred_element_type=F32)
    s = jnp.dot(y_ref[0], wssd_ref[...], preferred_element_type=F32)
    u = _sigmoid(gna_ref[0].astype(F32)) * a + _sigmoid(gssd_ref[0].astype(F32)) * s
    mix = jnp.dot(u.astype(BF16), wout_ref[...], preferred_element_type=F32)
    x1 = x_ref[0] + ga1_ref[0] * _rms(mix, npost_ref[...])
    x1_ref[0] = x1
    h2 = _rms(x1, npre_ref[...]) * (1.0 + sc2_ref[0]) + sh2_ref[0]
    hi = h2.astype(BF16)
    h2_ref[0] = hi
    lo = (h2 - hi.astype(F32)).astype(BF16)
    nt = (((1,), (1,)), ((), ()))
    wrh, wrl = wrh_ref[...], wrl_ref[...]
    logits = (lax.dot_general(wrh, hi, nt, preferred_element_type=F32)
              + lax.dot_general(wrh, lo, nt, preferred_element_type=F32)
              + lax.dot_general(wrl, hi, nt, preferred_element_type=F32))
    e = jnp.exp(logits - jnp.max(logits, axis=0, keepdims=True))
    aff_ref[0] = e / jnp.sum(e, axis=0, keepdims=True)


def _merge(o_na, y_ssd, slab, x, mod3, w_na, w_ssd, w_o, n_post, n_pre, wr_hi, wr_lo):
    bsz, seq, d = x.shape
    tm = 512
    ne = wr_hi.shape[0]
    tok = lambda width, blk: pl.BlockSpec((1, tm, width), lambda b, i: (b, i, blk))
    modv = lambda k: pl.BlockSpec((1, 1, d), lambda b, i: (b, 0, k))
    full = lambda r, c_: pl.BlockSpec((r, c_), lambda b, i: (0, 0))
    return pl.pallas_call(
        _merge_kernel,
        grid=(bsz, seq // tm),
        in_specs=[tok(d, 0), tok(SSD_WIDTH, 0), tok(d, COL_G // d), tok(d, COL_G // d + 1), tok(d, 0),
                  modv(2), modv(3), modv(4),
                  full(d, d), full(SSD_WIDTH, d), full(d, d), full(1, d), full(1, d), full(ne, d), full(ne, d)],
        out_specs=[tok(d, 0), tok(d, 0), pl.BlockSpec((1, ne, tm), lambda b, i: (b, 0, i))],
        out_shape=[jax.ShapeDtypeStruct((bsz, seq, d), F32), jax.ShapeDtypeStruct((bsz, seq, d), BF16),
                   jax.ShapeDtypeStruct((bsz, ne, seq), F32)],
        compiler_params=_params(("parallel", "parallel")),
        name="merge",
    )(o_na, y_ssd, slab, slab, x, mod3, mod3, mod3, w_na, w_ssd, w_o, n_post, n_pre, wr_hi, wr_lo)


def _prefix_count(mask_bf16, strict_upper):
    r, s = mask_bf16.shape
    offset = jnp.zeros((r, 1), F32)
    parts = []
    for j in range(s // LANES):
        seg = mask_bf16[:, j * LANES:(j + 1) * LANES]
        parts.append(jnp.dot(seg, strict_upper, preferred_element_type=F32) + offset)
        offset = offset + jnp.sum(seg.astype(F32), axis=1, keepdims=True)
    return jnp.concatenate(parts, axis=1)


def _route_kernel(aff_ref, h2_ref, pos_ref, xg_ref, gate_ref, *, cap):
    aff = aff_ref[0]
    bits = pltpu.bitcast(aff, jnp.int32)
    ne, t = aff.shape
    thr = jnp.zeros((ne, 1), jnp.int32)
    for bit in range(30, -1, -1):
        cand = thr | (1 << bit)
        cnt = jnp.sum(jnp.where(bits >= cand, 1.0, 0.0), axis=1, keepdims=True)
        thr = jnp.where(cnt >= cap, cand, thr)
    above = bits > thr
    tied = bits == thr
    need = cap - jnp.sum(jnp.where(above, 1.0, 0.0), axis=1, keepdims=True)
    ri = lax.broadcasted_iota(jnp.int32, (LANES, LANES), 0)
    ci = lax.broadcasted_iota(jnp.int32, (LANES, LANES), 1)
    strict_upper = jnp.where(ri < ci, 1.0, 0.0).astype(BF16)
    tie_rank = _prefix_count(jnp.where(tied, 1.0, 0.0).astype(BF16), strict_upper)
    sel = above | (tied & (tie_rank < need))
    slot = _prefix_count(jnp.where(sel, 1.0, 0.0).astype(BF16), strict_upper)
    slot = jnp.where(sel, slot, -1.0)
    pos = slot.astype(jnp.int32)
    padded = jnp.concatenate([slot, jnp.full((LANES - ne, t), -1.0, F32)], axis=0)
    for j in range(t // LANES):
        pos_ref[0, j * LANES:(j + 1) * LANES, :] = padded[:, j * LANES:(j + 1) * LANES].T.astype(jnp.int32)
    h2 = h2_ref[0]
    slot_id = lax.broadcasted_iota(jnp.int32, (cap, t), 0)
    for e in range(ne):
        hit = slot_id == pos[e:e + 1, :]
        onehot = jnp.where(hit, 1.0, 0.0).astype(BF16)
        xg_ref[e] = jnp.dot(onehot, h2, preferred_element_type=F32).astype(xg_ref.dtype)
        gate_ref[e] = jnp.sum(jnp.where(hit, aff[e:e + 1, :], 0.0), axis=1, keepdims=True)


def _route(aff_t, h2):
    bsz, ne, t = aff_t.shape
    d = h2.shape[2]
    cap = EC_CAPACITY_FACTOR * t // ne
    return pl.pallas_call(
        functools.partial(_route_kernel, cap=cap),
        grid=(bsz,),
        in_specs=[pl.BlockSpec((1, ne, t), lambda b: (b, 0, 0)),
                  pl.BlockSpec((1, t, d), lambda b: (b, 0, 0))],
        out_specs=[pl.BlockSpec((1, t, LANES), lambda b: (b, 0, 0)),
                   pl.BlockSpec((ne, cap, d), lambda b: (0, b, 0)),
                   pl.BlockSpec((ne, cap, 1), lambda b: (0, b, 0))],
        out_shape=[jax.ShapeDtypeStruct((bsz, t, LANES), jnp.int32),
                   jax.ShapeDtypeStruct((ne, bsz * cap, d), BF16),
                   jax.ShapeDtypeStruct((ne, bsz * cap, 1), F32)],
        compiler_params=_params(("parallel",)),
        name="route",
    )(aff_t, h2)


EXPERT_FF_TILE = 512
EXPERT_ROW_TILE = 512


def _expert_kernel(xg_ref, gate_ref, wg_ref, wu_ref, wd_ref, yo_ref, acc_ref):
    f = pl.program_id(1)

    @pl.when(f == 0)
    def _():
        acc_ref[...] = jnp.zeros(acc_ref.shape, F32)

    wg, wu, wd = wg_ref[0].astype(BF16), wu_ref[0].astype(BF16), wd_ref[0].astype(BF16)
    n = xg_ref.shape[1]
    rt = min(EXPERT_ROW_TILE, n)
    for r in range(n // rt):
        rows = pl.ds(r * rt, rt)
        xg = xg_ref[0, rows, :]
        hid = _silu(jnp.dot(xg, wg, preferred_element_type=F32)) * jnp.dot(xg, wu, preferred_element_type=F32)
        acc_ref[rows, :] += jnp.dot(hid.astype(BF16), wd, preferred_element_type=F32)

    @pl.when(f == pl.num_programs(1) - 1)
    def _():
        cap = yo_ref.shape[1]
        for b in range(yo_ref.shape[0]):
            rows = pl.ds(b * cap, cap)
            yo_ref[b] = (acc_ref[rows, :] * gate_ref[0, rows, :]).astype(yo_ref.dtype)


def _experts(xg, gate, w_g, w_u, w_d, bsz):
    ne, n, d = xg.shape
    cap = n // bsz
    ff = w_g.shape[2]
    tf = EXPERT_FF_TILE
    return pl.pallas_call(
        _expert_kernel,
        grid=(ne, ff // tf),
        in_specs=[pl.BlockSpec((1, n, d), lambda e, f: (e, 0, 0)),
                  pl.BlockSpec((1, n, 1), lambda e, f: (e, 0, 0)),
                  pl.BlockSpec((1, d, tf), lambda e, f: (e, 0, f)),
                  pl.BlockSpec((1, d, tf), lambda e, f: (e, 0, f)),
                  pl.BlockSpec((1, tf, d), lambda e, f: (e, f, 0))],
        out_specs=pl.BlockSpec((bsz, cap, d), lambda e, f: (0, e, 0)),
        out_shape=jax.ShapeDtypeStruct((bsz, ne * cap, d), BF16),
        scratch_shapes=[pltpu.VMEM((n, d), F32)],
        compiler_params=_params(("parallel", "arbitrary")),
        name="experts",
    )(xg, gate, w_g, w_u, w_d)


def _combine_kernel(pos_ref, yo_ref, x1_ref, ga2_ref, npost_ref, o_ref, *, cap, ne):
    pos = pos_ref[0]
    slot_id = lax.broadcasted_iota(jnp.int32, (1, cap), 1)
    onehot = jnp.concatenate(
        [jnp.where(pos[:, e:e + 1] == slot_id, 1.0, 0.0).astype(BF16) for e in range(ne)], axis=1)
    moe = jnp.dot(onehot, yo_ref[0], preferred_element_type=F32)
    o_ref[0] = x1_ref[0] + ga2_ref[0] * _rms(moe, npost_ref[...])


def _combine(pos_t, yo, x1, mod3, n_post, ne):
    bsz, seq, d = x1.shape
    cap = yo.shape[1] // ne
    tm = 512
    return pl.pallas_call(
        functools.partial(_combine_kernel, cap=cap, ne=ne),
        grid=(bsz, seq // tm),
        in_specs=[pl.BlockSpec((1, tm, LANES), lambda b, i: (b, i, 0)),
                  pl.BlockSpec((1, ne * cap, d), lambda b, i: (b, 0, 0)),
                  pl.BlockSpec((1, tm, d), lambda b, i: (b, i, 0)),
                  pl.BlockSpec((1, 1, d), lambda b, i: (b, 0, 5)),
                  pl.BlockSpec((1, d), lambda b, i: (0, 0))],
        out_specs=pl.BlockSpec((1, tm, d), lambda b, i: (b, i, 0)),
        out_shape=jax.ShapeDtypeStruct((bsz, seq, d), F32),
        compiler_params=_params(("parallel", "parallel")),
        name="combine",
    )(pos_t, yo, x1, mod3, n_post)


def kernel(x, c, ctx, c_ctx, w_ada, b_ada, norm_pre_mix, norm_post_mix, norm_pre_ffn, norm_post_ffn,
           w_in, na_rpb, ssd_conv_w, ssd_conv_b, ssd_a_log_fwd, ssd_a_log_bwd, ssd_dt_bias_fwd,
           ssd_dt_bias_bwd, ssd_d_skip, ssd_norm, w_branch_na, w_branch_ssd, w_out, w_router,
           w_exp_gate, w_exp_up, w_exp_down):
    assert w_ada.shape[0] == 1 and w_in.shape[0] == 1, "only depth 1 is implemented"
    assert x.shape[1] % (NA_QROWS * GRID_W) == 0 and x.shape[1] % INPROJ_ROWS == 0 and ctx.shape[1] % SSD_CHUNK == 0
    mod, slab, o_na, y_ssd = _front(x, c, ctx, c_ctx, w_ada[0], b_ada[0], norm_pre_mix[0], w_in[0], na_rpb[0],
                                    ssd_conv_w[0], ssd_conv_b[0], ssd_a_log_fwd[0], ssd_a_log_bwd[0],
                                    ssd_dt_bias_fwd[0], ssd_dt_bias_bwd[0], ssd_d_skip[0], ssd_norm[0])
    return _back(x, mod, slab, o_na, y_ssd, norm_post_mix[0], norm_pre_ffn[0], norm_post_ffn[0],
                 w_branch_na[0], w_branch_ssd[0], w_out[0], w_router[0], w_exp_gate[0], w_exp_up[0], w_exp_down[0])


def _back(x, mod, slab, o_na, y_ssd, norm_post_mix, norm_pre_ffn, norm_post_ffn,
          w_branch_na, w_branch_ssd, w_out, w_router, w_exp_gate, w_exp_up, w_exp_down):
    bsz, seq, d = x.shape
    mod3 = mod.reshape(16, 1, N_MOD * d)
    wr_t = w_router.T
    wr_hi = wr_t.astype(BF16)
    wr_lo = (wr_t - wr_hi.astype(F32)).astype(BF16)
    x1, h2, aff_t = _merge(o_na, y_ssd, slab, x, mod3, w_branch_na.astype(BF16), w_branch_ssd.astype(BF16),
                           w_out.astype(BF16), norm_post_mix.reshape(1, d), norm_pre_ffn.reshape(1, d),
                           wr_hi, wr_lo)
    pos_t, xg, gate = _route(aff_t, h2)
    yo = _experts(xg, gate, w_exp_gate, w_exp_up, w_exp_down, bsz)
    return _combine(pos_t, yo, x1, mod3, norm_post_ffn.reshape(1, d), w_router.shape[1])


def _group_rows(fwd, bwd):
    rows = []
    for g in range(SSD_GROUPS):
        sl = slice(g * SSD_GHEADS, (g + 1) * SSD_GHEADS)
        rows += [fwd[sl], bwd[sl]]
    return jnp.concatenate(rows, axis=0)


def _front(x, c, ctx, c_ctx, w_ada, b_ada, norm_pre_mix, w_in, na_rpb, ssd_conv_w, ssd_conv_b,
           ssd_a_log_fwd, ssd_a_log_bwd, ssd_dt_bias_fwd, ssd_dt_bias_bwd, ssd_d_skip, ssd_norm):
    bsz, seq, d = x.shape
    tctx = ctx.shape[1]
    assert bsz <= 8
    cc =jnp.zeros((16, d), F32).at[:bsz].set(c).at[8].set(c_ctx)
    mod = _ada(cc, w_ada, b_ada)
    mod3 = mod.reshape(16, 1, N_MOD * d)

    dt0 = COL_G
    w_main = jnp.swapaxes(w_in, 0, 1).astype(BF16)
    w_gate = w_main[dt0 + 2 * SSD_HEADS:]
    main_tiles = dt0 // 1024
    w_dt = _group_rows(w_main[dt0:dt0 + SSD_HEADS], w_main[dt0 + SSD_HEADS:dt0 + 2 * SSD_HEADS])
    rope = _rope_tables(seq)
    g_pre = norm_pre_mix.reshape(1, d)

    slab, dt = _inproj(x, mod3, 0, g_pre, w_main, w_gate, w_dt, rope, 2, main_tiles, LAT_COLS // 1024, lambda j: j)
    nctx = bsz * tctx
    ctx_rows = -(-nctx // INPROJ_ROWS) * INPROJ_ROWS
    ctx_flat = ctx.reshape(1, nctx, d)
    if ctx_rows != nctx:
        ctx_flat = jnp.pad(ctx_flat, ((0, 0), (0, ctx_rows - nctx), (0, 0)))
    ctx_tile = lambda j: jnp.where(j < 2, j + COL_K // 1024, j - 2 + COL_X // 1024)
    ctx_tiles = CTX_COLS // 1024
    slab_c, dt_c = _inproj(ctx_flat, mod3, 8, g_pre, w_main, w_gate, w_dt, rope, 0, ctx_tiles, ctx_tiles, ctx_tile)
    slab_c = slab_c[0, :nctx].reshape(bsz, tctx, CTX_COLS)
    dt_c = jnp.swapaxes(dt_c[0, :, :nctx].reshape(-1, bsz, tctx), 0, 1)

    o_na = _na(slab, slab_c, _na_pair_table(na_rpb))

    nx, nb = SSD_WIDTH, SSD_GROUPS * SSD_STATE
    cw = (ssd_conv_w[:, :nx], ssd_conv_w[:, nx:nx + nb], ssd_conv_w[:, nx + nb:])
    cb = tuple(v.reshape(1, -1) for v in (ssd_conv_b[:nx], ssd_conv_b[nx:nx + nb], ssd_conv_b[nx + nb:]))
    lanes = lambda v: jnp.broadcast_to(v[:, None], (v.shape[0], LANES))
    dtb = _group_rows(lanes(ssd_dt_bias_fwd), lanes(ssd_dt_bias_bwd))
    alog = _group_rows(lanes(ssd_a_log_fwd), lanes(ssd_a_log_bwd))
    dskip = jnp.repeat(ssd_d_skip, SSD_HEAD_DIM).reshape(1, -1)
    y_ssd = _ssd(slab, dt, slab_c, dt_c, cw, cb, dtb, alog, dskip, ssd_norm.reshape(1, -1))
    return mod, slab, o_na, y_ssd
```
